```python
import math
import jax, jax.numpy as jnp
from jax import lax
import numpy as np

D_MODEL = 1024
BATCH = 4
SEQ = 4096
DEPTH = 2
DEC_BATCH = 32
DEC_SEQ = 1
PAST_LEN = 16384
PAGE_SIZE = 128

N_A_LAYERS = DEPTH // 2
N_B_LAYERS = DEPTH - N_A_LAYERS
D_FF = 2816
GLA_HEADS = 4
GLA_DK = 96
GLA_DV = 192
GLA_GATE_RANK = 16
GLA_GATE_NORM = 16.0
GLA_CHUNK = 64
DIFF_HEADS = 6
DIFF_DK = 64
DIFF_DV = 2 * DIFF_DK
MEM_TOKENS = 256
MEM_HEADS = 4
MEM_DH = 64
Q_BLOCK = 128
EPS = 1e-6
D_GLA_QK = GLA_HEADS * GLA_DK
D_GLA_V = GLA_HEADS * GLA_DV
D_DIFF_QK = DIFF_HEADS * 2 * DIFF_DK
D_DIFF_V = DIFF_HEADS * DIFF_DV
MEM_W = MEM_HEADS * MEM_DH
D_A_IN = 2 * D_GLA_QK + 2 * D_GLA_V + GLA_GATE_RANK + MEM_W
D_B_IN = D_DIFF_QK + MEM_W
D_MIX = D_GLA_V + MEM_W

kernel_name = 'yoco_gla_diffattn_macaron_step'


def _rmsnorm(x, g):
    xf = x.astype(jnp.float32)
    y = xf * lax.rsqrt(jnp.mean(xf * xf, axis=-1, keepdims=True) + EPS)
    return (y * g.astype(jnp.float32)).astype(x.dtype)


def _swiglu(x, wg, wu, wd):
    return (jax.nn.silu(x @ wg) * (x @ wu)) @ wd


def _gla(q, k, v, logg, s0, chunk):
    B, L, H, dk = q.shape
    dv = v.shape[-1]
    n = L // chunk

    def blk(t):
        return t.astype(jnp.float32).reshape(B, n, chunk, H, t.shape[-1])

    qc, kc, vc = blk(q), blk(k), blk(v)
    b = jnp.cumsum(blk(logg), axis=2)
    b_last = b[:, :, -1]
    q_dec = qc * jnp.exp(b)
    k_inv = kc * jnp.exp(-b)
    att = jnp.einsum('bnihd,bnjhd->bnhij', q_dec, k_inv)
    causal = jnp.tril(jnp.ones((chunk, chunk), dtype=bool))
    att = jnp.where(causal, att, 0.0)
    o_intra = jnp.einsum('bnhij,bnjhe->bnihe', att, vc)
    k_end = kc * jnp.exp(b_last[:, :, None] - b)
    ds = jnp.einsum('bnjhd,bnjhe->bnhde', k_end, vc)

    def step(s, inp):
        ds_n, dec_n = inp
        return dec_n[..., None] * s + ds_n, s

    s_fin, s_start = lax.scan(step, s0.astype(jnp.float32),
                              (jnp.moveaxis(ds, 1, 0), jnp.moveaxis(jnp.exp(b_last), 1, 0)))
    s_start = jnp.moveaxis(s_start, 0, 1)
    o_inter = jnp.einsum('bnihd,bnhde->bnihe', q_dec, s_start)
    o = (o_intra + o_inter).reshape(B, L, H, dv)
    return o.astype(v.dtype), s_fin.astype(s0.dtype)


def _diff_scores(q, k):
    return jnp.einsum('bqhcd,bkhcd->bhcqk', q, k,
                      preferred_element_type=jnp.float32) * (DIFF_DK ** -0.5)


def _diff_attend_prompt(q, k, v, lam):
    B, L = q.shape[:2]
    nb = L // Q_BLOCK
    qb = jnp.moveaxis(q.reshape(B, nb, Q_BLOCK, DIFF_HEADS, 2, DIFF_DK), 1, 0)
    kpos = jnp.arange(L)

    def one(args):
        qi, bi = args
        qpos = bi * Q_BLOCK + jnp.arange(Q_BLOCK)
        s = jnp.where(kpos[None, :] <= qpos[:, None], _diff_scores(qi, k), -jnp.inf)
        p = jax.nn.softmax(s, axis=-1)
        pd = p[:, :, 0] - lam * p[:, :, 1]
        return jnp.einsum('bhqk,bkhe->bqhe', pd, v, preferred_element_type=jnp.float32)

    o = lax.map(one, (qb, jnp.arange(nb)))
    return jnp.moveaxis(o, 0, 1).reshape(B, L, DIFF_HEADS, DIFF_DV).astype(v.dtype)


def _diff_attend_sample(q, k_new, v_new, past_k, past_v, lam):
    L = q.shape[1]
    P = past_k.shape[1]
    causal = jnp.arange(L)[None, :] <= jnp.arange(L)[:, None]
    s = jnp.concatenate([_diff_scores(q, past_k),
                         jnp.where(causal, _diff_scores(q, k_new), -jnp.inf)], axis=-1)
    p = jax.nn.softmax(s, axis=-1)
    pd = p[:, :, 0] - lam * p[:, :, 1]
    o = (jnp.einsum('bhqk,bkhe->bqhe', pd[..., :P], past_v, preferred_element_type=jnp.float32)
         + jnp.einsum('bhqk,bkhe->bqhe', pd[..., P:], v_new, preferred_element_type=jnp.float32))
    return o.astype(v_new.dtype)


def _mem_attend(q, mk, mv):
    s = jnp.einsum('blhd,bmhd->bhlm', q, mk, preferred_element_type=jnp.float32) * (MEM_DH ** -0.5)
    p = jax.nn.softmax(s, axis=-1)
    return jnp.einsum('bhlm,bmhd->blhd', p, mv, preferred_element_type=jnp.float32).astype(q.dtype)


def _trunk(x, mem_k, mem_v, gla_s0, past_k, past_v, w):
    B, L, _ = x.shape
    new_s = []
    shared_k = None
    shared_v = None
    a_splits = [D_GLA_QK, 2 * D_GLA_QK, 2 * D_GLA_QK + D_GLA_V,
                2 * D_GLA_QK + D_GLA_V + GLA_GATE_RANK, 2 * D_GLA_QK + 2 * D_GLA_V + GLA_GATE_RANK]
    for i in range(DEPTH):
        x = x + 0.5 * _swiglu(_rmsnorm(x, w['ffn1_norm'][i]), w['ffn1_w_gate'][i],
                              w['ffn1_w_up'][i], w['ffn1_w_down'][i])
        h = _rmsnorm(x, w['mix_norm'][i])
        if i < N_A_LAYERS:
            u = h @ w['a_w_in'][i]
            q, k, v, g_lr, r, q_mem = jnp.split(u, a_splits, axis=-1)
            logg = jax.nn.log_sigmoid((g_lr @ w['a_w_gate2'][i] + w['a_b_gate2'][i]).astype(jnp.float32)) / GLA_GATE_NORM
            q = q.reshape(B, L, GLA_HEADS, GLA_DK) * (GLA_DK ** -0.5)
            k = k.reshape(B, L, GLA_HEADS, GLA_DK)
            v = v.reshape(B, L, GLA_HEADS, GLA_DV)
            logg = logg.reshape(B, L, GLA_HEADS, GLA_DK)
            chunk = GLA_CHUNK if L % GLA_CHUNK == 0 else L
            o, s_fin = _gla(q, k, v, logg, gla_s0[i], chunk)
            new_s.append(s_fin)
            o_main = _rmsnorm(o, w['a_onorm'][i]).reshape(B, L, D_GLA_V) * jax.nn.silu(r)
        else:
            j = i - N_A_LAYERS
            u = h @ w['b_w_in'][j]
            q, q_mem = jnp.split(u, [D_DIFF_QK], axis=-1)
            q = q.reshape(B, L, DIFF_HEADS, 2, DIFF_DK)
            lam_init = 0.8 - 0.6 * math.exp(-0.3 * i)
            lam = (jnp.exp(jnp.sum(w['b_lambda_q1'][j].astype(jnp.float32) * w['b_lambda_k1'][j].astype(jnp.float32)))
                   - jnp.exp(jnp.sum(w['b_lambda_q2'][j].astype(jnp.float32) * w['b_lambda_k2'][j].astype(jnp.float32)))
                   + lam_init)
            if past_k is None:
                o = _diff_attend_prompt(q, shared_k, shared_v, lam)
            else:
                o = _diff_attend_sample(q, shared_k, shared_v, past_k, past_v, lam)
            o_main = (_rmsnorm(o, w['b_subln'][j]) * (1.0 - lam_init)).reshape(B, L, D_DIFF_V)
        o_mem = _mem_attend(q_mem.reshape(B, L, MEM_HEADS, MEM_DH), mem_k[i], mem_v[i]).reshape(B, L, MEM_W)
        x = x + jnp.concatenate([o_main, o_mem], axis=-1) @ w['w_out'][i]
        x = x + 0.5 * _swiglu(_rmsnorm(x, w['ffn2_norm'][i]), w['ffn2_w_gate'][i],
                              w['ffn2_w_up'][i], w['ffn2_w_down'][i])
        if i == N_A_LAYERS - 1:
            kv = _rmsnorm(x, w['kv_norm']) @ w['w_kv']
            shared_k = kv[..., :D_DIFF_QK].reshape(B, L, DIFF_HEADS, 2, DIFF_DK)
            shared_v = kv[..., D_DIFF_QK:].reshape(B, L, DIFF_HEADS, DIFF_DV)
    return _rmsnorm(x, w['final_norm']), jnp.stack(new_s), shared_k, shared_v


def setup_inputs(seed: int = 0) -> dict:
    key = jax.random.key(seed)
    ks = iter(jax.random.split(key, 48))

    def nrm(shape, scale):
        return jax.random.normal(next(ks), shape, jnp.float32) * scale

    def gain(shape):
        return 1.0 + nrm(shape, 0.01)

    n_pages = PAST_LEN // PAGE_SIZE
    n_phys = (5 * DEC_BATCH * n_pages + 3) // 4
    x_prompt = nrm((BATCH, SEQ, D_MODEL), 1.0)
    x_sample = nrm((DEC_BATCH, DEC_SEQ, D_MODEL), 1.0)
    mem_prompt = nrm((BATCH, MEM_TOKENS, D_MODEL), 1.0)
    state_gla = nrm((N_A_LAYERS, DEC_BATCH, GLA_HEADS, GLA_DK, GLA_DV), 3.0)
    cache_k = nrm((n_phys, PAGE_SIZE, DIFF_HEADS, 2, DIFF_DK), 1.0)
    cache_v = nrm((n_phys, PAGE_SIZE, DIFF_HEADS, DIFF_DV), 1.0)
    cache_mem_k = nrm((DEPTH, DEC_BATCH, MEM_TOKENS, MEM_HEADS, MEM_DH), 1.0)
    cache_mem_v = nrm((DEPTH, DEC_BATCH, MEM_TOKENS, MEM_HEADS, MEM_DH), 1.0)
    perm = jax.random.permutation(next(ks), n_phys)
    page_table = perm[:DEC_BATCH * n_pages].reshape(DEC_BATCH, n_pages).astype(jnp.int32)
    return {
        'x_prompt': x_prompt,
        'x_sample': x_sample,
        'mem_prompt': mem_prompt,
        'state_gla': state_gla,
        'cache_k': cache_k,
        'cache_v': cache_v,
        'cache_mem_k': cache_mem_k,
        'cache_mem_v': cache_mem_v,
        'page_table': page_table,
        'ffn1_norm': gain((DEPTH, D_MODEL)),
        'ffn1_w_gate': nrm((DEPTH, D_MODEL, D_FF), D_MODEL ** -0.5),
        'ffn1_w_up': nrm((DEPTH, D_MODEL, D_FF), D_MODEL ** -0.5),
        'ffn1_w_down': nrm((DEPTH, D_FF, D_MODEL), D_FF ** -0.5),
        'mix_norm': gain((DEPTH, D_MODEL)),
        'a_w_in': nrm((N_A_LAYERS, D_MODEL, D_A_IN), D_MODEL ** -0.5),
        'a_w_gate2': nrm((N_A_LAYERS, GLA_GATE_RANK, D_GLA_QK), GLA_GATE_RANK ** -0.5),
        'a_b_gate2': nrm((N_A_LAYERS, D_GLA_QK), 0.1),
        'a_onorm': gain((N_A_LAYERS, GLA_DV)),
        'b_w_in': nrm((N_B_LAYERS, D_MODEL, D_B_IN), D_MODEL ** -0.5),
        'b_lambda_q1': nrm((N_B_LAYERS, DIFF_DK), 0.1),
        'b_lambda_k1': nrm((N_B_LAYERS, DIFF_DK), 0.1),
        'b_lambda_q2': nrm((N_B_LAYERS, DIFF_DK), 0.1),
        'b_lambda_k2': nrm((N_B_LAYERS, DIFF_DK), 0.1),
        'b_subln': gain((N_B_LAYERS, DIFF_DV)),
        'mem_norm': gain((DEPTH, D_MODEL)),
        'w_mem_kv': nrm((DEPTH, D_MODEL, 2 * MEM_W), D_MODEL ** -0.5),
        'w_out': nrm((DEPTH, D_MIX, D_MODEL), D_MIX ** -0.5),
        'ffn2_norm': gain((DEPTH, D_MODEL)),
        'ffn2_w_gate': nrm((DEPTH, D_MODEL, D_FF), D_MODEL ** -0.5),
        'ffn2_w_up': nrm((DEPTH, D_MODEL, D_FF), D_MODEL ** -0.5),
        'ffn2_w_down': nrm((DEPTH, D_FF, D_MODEL), D_FF ** -0.5),
        'kv_norm': gain((D_MODEL,)),
        'w_kv': nrm((D_MODEL, D_DIFF_QK + D_DIFF_V), D_MODEL ** -0.5),
        'final_norm': gain((D_MODEL,)),
    }


def reference(x_prompt, x_sample, mem_prompt, state_gla, cache_k, cache_v, cache_mem_k, cache_mem_v,
              page_table, ffn1_norm, ffn1_w_gate, ffn1_w_up, ffn1_w_down, mix_norm, a_w_in, a_w_gate2,
              a_b_gate2, a_onorm, b_w_in, b_lambda_q1, b_lambda_k1, b_lambda_q2, b_lambda_k2, b_subln,
              mem_norm, w_mem_kv, w_out, ffn2_norm, ffn2_w_gate, ffn2_w_up, ffn2_w_down, kv_norm, w_kv,
              final_norm):
    w = {
        'ffn1_norm': ffn1_norm, 'ffn1_w_gate': ffn1_w_gate, 'ffn1_w_up': ffn1_w_up, 'ffn1_w_down': ffn1_w_down,
        'mix_norm': mix_norm, 'a_w_in': a_w_in, 'a_w_gate2': a_w_gate2, 'a_b_gate2': a_b_gate2,
        'a_onorm': a_onorm, 'b_w_in': b_w_in, 'b_lambda_q1': b_lambda_q1, 'b_lambda_k1': b_lambda_k1,
        'b_lambda_q2': b_lambda_q2, 'b_lambda_k2': b_lambda_k2, 'b_subln': b_subln, 'w_out': w_out,
        'ffn2_norm': ffn2_norm, 'ffn2_w_gate': ffn2_w_gate, 'ffn2_w_up': ffn2_w_up, 'ffn2_w_down': ffn2_w_down,
        'kv_norm': kv_norm, 'w_kv': w_kv, 'final_norm': final_norm,
    }
    bp, m = mem_prompt.shape[0], mem_prompt.shape[1]
    mk, mv = [], []
    for i in range(DEPTH):
        kv = _rmsnorm(mem_prompt, mem_norm[i]) @ w_mem_kv[i]
        mk.append(kv[..., :MEM_W].reshape(bp, m, MEM_HEADS, MEM_DH))
        mv.append(kv[..., MEM_W:].reshape(bp, m, MEM_HEADS, MEM_DH))
    mem_k_prompt = jnp.stack(mk)
    mem_v_prompt = jnp.stack(mv)
    s0 = jnp.zeros((N_A_LAYERS, bp, GLA_HEADS, GLA_DK, GLA_DV), state_gla.dtype)
    y_prompt, state_gla_prompt, kv_k_prompt, kv_v_prompt = _trunk(
        x_prompt, mem_k_prompt, mem_v_prompt, s0, None, None, w)
    db, n_pages = page_table.shape
    past_len = n_pages * cache_k.shape[1]
    past_k = cache_k[page_table].reshape((db, past_len) + cache_k.shape[2:])
    past_v = cache_v[page_table].reshape((db, past_len) + cache_v.shape[2:])
    y_sample, state_gla_sample, kv_k_sample, kv_v_sample = _trunk(
        x_sample, cache_mem_k, cache_mem_v, state_gla, past_k, past_v, w)
    return (y_prompt, y_sample, state_gla_prompt, state_gla_sample, kv_k_prompt, kv_v_prompt,
            kv_k_sample, kv_v_sample, mem_k_prompt, mem_v_prompt)
```

```python
import functools
import math

import jax
import jax.numpy as jnp
from jax import lax
from jax.experimental import pallas as pl
from jax.experimental.pallas import tpu as pltpu

F32 = jnp.float32
BF16 = jnp.bfloat16

EPS = 1e-6
GLA_HEADS = 4
GLA_DK = 96
GLA_DV = 192
GLA_GATE_RANK = 16
GLA_GATE_NORM = 16.0
DIFF_HEADS = 6
DIFF_DK = 64
DIFF_DV = 2 * DIFF_DK
MEM_HEADS = 4
MEM_DH = 64
MEM_W = MEM_HEADS * MEM_DH

LANES = 128
SUBLANES = 8
VMEM_LIMIT_BYTES = 56 * 1024 * 1024

GLA_DK_PAD = 128
GLA_DV_PAD = 256
GLA_CHUNK = 64

A_Q0 = 0
A_K0 = A_Q0 + GLA_HEADS * GLA_DK_PAD
A_V0 = A_K0 + GLA_HEADS * GLA_DK_PAD
A_G0 = A_V0 + GLA_HEADS * GLA_DV_PAD
A_R0 = A_G0 + LANES
A_M0 = A_R0 + GLA_HEADS * GLA_DV_PAD
A_W = A_M0 + MEM_W


def _dot(a, b):
    return jnp.dot(a, b, preferred_element_type=F32)


def _dot_nt(a, b):
    return lax.dot_general(a, b, (((1,), (1,)), ((), ())), preferred_element_type=F32)


def _rms(x, g):
    ms = jnp.mean(x * x, axis=-1, keepdims=True)
    return x * lax.rsqrt(ms + EPS) * g


def _const_spec(shape):
    nd = len(shape)
    return pl.BlockSpec(shape, lambda *_: (0,) * nd, pipeline_mode=pl.Buffered(1))


def _params(sem):
    return pltpu.CompilerParams(dimension_semantics=sem, vmem_limit_bytes=VMEM_LIMIT_BYTES)


def _ffn_kernel(x_ref, g_ref, wg_ref, wu_ref, wd_ref, *rest, ff_chunk, final_norm):
    if final_norm:
        fg_ref, o_ref = rest
    else:
        (o_ref,) = rest
    x = x_ref[...]
    h = _rms(x, g_ref[...]).astype(BF16)
    d_ff = wg_ref.shape[1]
    acc = jnp.zeros(x.shape, F32)
    for c in range(d_ff // ff_chunk):
        sl = slice(c * ff_chunk, (c + 1) * ff_chunk)
        g = _dot(h, wg_ref[:, sl])
        u = _dot(h, wu_ref[:, sl])
        a = (g * jax.nn.sigmoid(g) * u).astype(BF16)
        acc = acc + _dot(a, wd_ref[sl, :])
    y = x + 0.5 * acc
    if final_norm:
        y = _rms(y, fg_ref[...])
    o_ref[...] = y


def _ffn(x, gain, wg, wu, wd, final_gain=None):
    m, d = x.shape
    d_ff = wg.shape[1]
    tm = min(m, 512)
    assert m % tm == 0 and d_ff % 256 == 0
    final_norm = final_gain is not None
    in_specs = [
        pl.BlockSpec((tm, d), lambda i: (i, 0)),
        _const_spec((1, d)),
        _const_spec((d, d_ff)),
        _const_spec((d, d_ff)),
        _const_spec((d_ff, d)),
    ]
    args = [x, gain.reshape(1, d), wg, wu, wd]
    if final_norm:
        in_specs.append(_const_spec((1, d)))
        args.append(final_gain.reshape(1, d))
    return pl.pallas_call(
        functools.partial(_ffn_kernel, ff_chunk=256, final_norm=final_norm),
        out_shape=jax.ShapeDtypeStruct((m, d), F32),
        grid=(m // tm,),
        in_specs=in_specs,
        out_specs=pl.BlockSpec((tm, d), lambda i: (i, 0)),
        compiler_params=_params(("parallel",)),
        name="ffn_final" if final_norm else "ffn",
    )(*args)


def _memkv_kernel(m_ref, g_ref, w_ref, k_ref, v_ref):
    h = _rms(m_ref[...], g_ref[0]).astype(BF16)
    kv = _dot(h, w_ref[0])
    k_ref[0] = kv[:, :MEM_W]
    v_ref[0] = kv[:, MEM_W:]


def _memkv(mem, gains, w):
    depth = w.shape[0]
    m, d = mem.shape
    out = jax.ShapeDtypeStruct((depth, m, MEM_W), F32)
    return pl.pallas_call(
        _memkv_kernel,
        out_shape=(out, out),
        grid=(depth,),
        in_specs=[
            pl.BlockSpec((m, d), lambda i: (0, 0)),
            pl.BlockSpec((1, 1, d), lambda i: (i, 0, 0)),
            pl.BlockSpec((1, d, 2 * MEM_W), lambda i: (i, 0, 0)),
        ],
        out_specs=(
            pl.BlockSpec((1, m, MEM_W), lambda i: (i, 0, 0)),
            pl.BlockSpec((1, m, MEM_W), lambda i: (i, 0, 0)),
        ),
        compiler_params=_params(("arbitrary",)),
        name="memkv",
    )(mem, gains.reshape(depth, 1, d), w)


def _log_sigmoid(z):
    return jnp.minimum(z, 0.0) - jnp.log1p(jnp.exp(-jnp.abs(z)))


def _gla_inputs(h, win_ref, wg2_ref, bg2_ref):
    q = _dot(h, win_ref[:, A_Q0:A_K0])
    k = _dot(h, win_ref[:, A_K0:A_V0])
    v = _dot(h, win_ref[:, A_V0:A_G0])
    g_lr = _dot(h, win_ref[:, A_G0:A_R0])
    r = _dot(h, win_ref[:, A_R0:A_M0])
    q_mem = _dot(h, win_ref[:, A_M0:A_W])
    z = _dot(g_lr.astype(BF16), wg2_ref[...]) + bg2_ref[...]
    logg = _log_sigmoid(z) / GLA_GATE_NORM
    return q, k, v, logg, r, q_mem


def _gla_output_gate(o, r, gain):
    parts = []
    for h in range(GLA_HEADS):
        sl = slice(h * GLA_DV_PAD, (h + 1) * GLA_DV_PAD)
        oh = o[:, sl]
        ms = jnp.sum(oh * oh, axis=-1, keepdims=True) * (1.0 / GLA_DV)
        parts.append(oh * lax.rsqrt(ms + EPS) * gain[:, sl])
    on = jnp.concatenate(parts, axis=1)
    return on * (r * jax.nn.sigmoid(r))


def _mem_attend_shared(qm, mk, mv):
    lane_head = lax.broadcasted_iota(jnp.int32, mk.shape, 1) >> 6
    out_head = lax.broadcasted_iota(jnp.int32, (qm.shape[0], MEM_W), 1) >> 6
    o = jnp.zeros((qm.shape[0], MEM_W), F32)
    for h in range(MEM_HEADS):
        mkh = jnp.where(lane_head == h, mk, 0.0).astype(BF16)
        s = _dot_nt(qm, mkh) * (MEM_DH ** -0.5)
        e = jnp.exp(s - jnp.max(s, axis=-1, keepdims=True))
        l = jnp.sum(e, axis=-1, keepdims=True)
        oh = _dot(e.astype(BF16), mv) / l
        o = o + jnp.where(out_head == h, oh, 0.0)
    return o


def _mem_attend_rows(qmem, mk_ref, mv_ref, om_scr):
    rows = qmem.shape[0]
    r8 = lax.broadcasted_iota(jnp.int32, (SUBLANES, MEM_W), 0)
    lh = lax.broadcasted_iota(jnp.int32, (SUBLANES, MEM_W), 1) >> 6
    for j in range(rows):
        mk = mk_ref[j].astype(BF16)
        mv = mv_ref[j].astype(BF16)
        qrow = jnp.broadcast_to(qmem[j:j + 1], (SUBLANES, MEM_W))
        qm = jnp.where(r8 == lh, qrow, 0.0).astype(BF16)
        s = _dot_nt(qm, mk) * (MEM_DH ** -0.5)
        e = jnp.exp(s - jnp.max(s, axis=-1, keepdims=True))
        l = jnp.sum(e, axis=-1, keepdims=True)
        oh = _dot(e.astype(BF16), mv) / l
        om_scr[j:j + 1, :] = jnp.sum(jnp.where(r8 == lh, oh, 0.0), axis=0, keepdims=True)


def _mixer_a_prompt_kernel(x_ref, g_ref, win_ref, wg2_ref, bg2_ref, on_ref, mk_ref, mv_ref,
                           wom_ref, wome_ref, xo_ref, st_ref, s_scr):
    tl = x_ref.shape[1]
    li = pl.program_id(1)

    @pl.when(li == 0)
    def _():
        s_scr[...] = jnp.zeros(s_scr.shape, F32)

    x = x_ref[0]
    h = _rms(x, g_ref[...]).astype(BF16)
    q, k, v, logg, r, q_mem = _gla_inputs(h, win_ref, wg2_ref, bg2_ref)

    row = lax.broadcasted_iota(jnp.int32, (tl, tl), 0)
    col = lax.broadcasted_iota(jnp.int32, (tl, tl), 1)
    tril = ((row >> 6) == (col >> 6)) & (col <= row)
    trilb = jnp.where(tril, 1.0, 0.0).astype(BF16)
    hi = logg.astype(BF16)
    lo = (logg - hi.astype(F32)).astype(BF16)
    b = _dot(trilb, hi) + _dot(trilb, lo)

    lane_chunk = lax.broadcasted_iota(jnp.int32, (GLA_DK_PAD, tl), 1) >> 6
    scale = GLA_DK ** -0.5
    o_parts = []
    for hd in range(GLA_HEADS):
        ks = slice(hd * GLA_DK_PAD, (hd + 1) * GLA_DK_PAD)
        vs = slice(hd * GLA_DV_PAD, (hd + 1) * GLA_DV_PAD)
        bh = b[:, ks]
        qd = (q[:, ks] * scale) * jnp.exp(bh)
        ki = k[:, ks] * jnp.exp(-bh)
        qdb = qd.astype(BF16)
        att = jnp.where(tril, _dot_nt(qdb, ki.astype(BF16)), 0.0)
        vh = v[:, vs].astype(BF16)
        o_intra = _dot(att.astype(BF16), vh)
        ki_t = ki.T
        b_t = bh.T
        state = s_scr[hd]
        inter = []
        for c in range(tl // GLA_CHUNK):
            last = c * GLA_CHUNK + GLA_CHUNK - 1
            dec = jnp.exp(b_t[:, last:last + 1])
            inter.append(_dot(qdb[c * GLA_CHUNK:(c + 1) * GLA_CHUNK], state.astype(BF16)))
            kend_t = jnp.where(lane_chunk == c, ki_t * dec, 0.0).astype(BF16)
            state = dec * state + _dot(kend_t, vh)
        s_scr[hd] = state
        o_parts.append(o_intra + jnp.concatenate(inter, axis=0))
    o = jnp.concatenate(o_parts, axis=1)

    o_main = _gla_output_gate(o, r, on_ref[...])
    o_mem = _mem_attend_shared(q_mem.astype(BF16), mk_ref[0], mv_ref[0].astype(BF16))
    xo_ref[0] = x + _dot(o_main.astype(BF16), wom_ref[...]) + _dot(o_mem.astype(BF16), wome_ref[...])

    @pl.when(li == pl.num_programs(1) - 1)
    def _():
        for hd in range(GLA_HEADS):
            st_ref[0, hd] = s_scr[hd][:GLA_DK, :GLA_DV]


def _mixer_a_prompt(x, gain, wa, mem_k, mem_v):
    b, l, d = x.shape
    tl = 256
    assert l % tl == 0 and tl % GLA_CHUNK == 0
    m = mem_k.shape[1]
    return pl.pallas_call(
        _mixer_a_prompt_kernel,
        out_shape=(
            jax.ShapeDtypeStruct((b, l, d), F32),
            jax.ShapeDtypeStruct((b, GLA_HEADS, GLA_DK, GLA_DV), F32),
        ),
        grid=(b, l // tl),
        in_specs=[
            pl.BlockSpec((1, tl, d), lambda i, j: (i, j, 0)),
            _const_spec((1, d)),
            _const_spec((d, A_W)),
            _const_spec((LANES, GLA_HEADS * GLA_DK_PAD)),
            _const_spec((1, GLA_HEADS * GLA_DK_PAD)),
            _const_spec((1, GLA_HEADS * GLA_DV_PAD)),
            pl.BlockSpec((1, m, MEM_W), lambda i, j: (i, 0, 0)),
            pl.BlockSpec((1, m, MEM_W), lambda i, j: (i, 0, 0)),
            _const_spec((GLA_HEADS * GLA_DV_PAD, d)),
            _const_spec((MEM_W, d)),
        ],
        out_specs=(
            pl.BlockSpec((1, tl, d), lambda i, j: (i, j, 0)),
            pl.BlockSpec((1, GLA_HEADS, GLA_DK, GLA_DV), lambda i, j: (i, 0, 0, 0)),
        ),
        scratch_shapes=[pltpu.VMEM((GLA_HEADS, GLA_DK_PAD, GLA_DV_PAD), F32)],
        compiler_params=_params(("parallel", "arbitrary")),
        name="mixer_a_prompt",
    )(x, gain.reshape(1, d), wa["w_in"], wa["w_gate2"], wa["b_gate2"], wa["onorm"], mem_k, mem_v,
      wa["w_out_main"], wa["w_out_mem"])


def _columns(t):
    pad = jnp.zeros((LANES - t.shape[0], t.shape[1]), F32)
    return jnp.concatenate([t, pad], axis=0).T


def _mixer_a_sample_kernel(x_ref, g_ref, win_ref, wg2_ref, bg2_ref, on_ref, st_ref, mk_ref, mv_ref,
                           wom_ref, wome_ref, xo_ref, sto_ref, o_scr, om_scr):
    rows = x_ref.shape[0]
    x = x_ref[...]
    h = _rms(x, g_ref[...]).astype(BF16)
    q, k, v, logg, r, q_mem = _gla_inputs(h, win_ref, wg2_ref, bg2_ref)
    a_t = _columns(jnp.exp(logg))
    k_t = _columns(k)
    q_t = _columns(q * (GLA_DK ** -0.5))
    o_scr[...] = jnp.zeros(o_scr.shape, F32)
    for j in range(rows):
        for hd in range(GLA_HEADS):
            rs = slice(hd * GLA_DK_PAD, hd * GLA_DK_PAD + GLA_DK)
            vs = slice(hd * GLA_DV_PAD, hd * GLA_DV_PAD + GLA_DV)
            s_new = a_t[rs, j:j + 1] * st_ref[j, hd] + k_t[rs, j:j + 1] * v[j:j + 1, vs]
            sto_ref[j, hd] = s_new
            o_scr[j:j + 1, vs] = jnp.sum(q_t[rs, j:j + 1] * s_new, axis=0, keepdims=True)
    o_main = _gla_output_gate(o_scr[...], r, on_ref[...])
    _mem_attend_rows(q_mem, mk_ref, mv_ref, om_scr)
    xo_ref[...] = (x + _dot(o_main.astype(BF16), wom_ref[...])
                   + _dot(om_scr[...].astype(BF16), wome_ref[...]))


def _mixer_a_sample(x, gain, wa, state, mem_k, mem_v):
    n, d = x.shape
    rows = SUBLANES
    assert n % rows == 0
    m = mem_k.shape[1]
    return pl.pallas_call(
        _mixer_a_sample_kernel,
        out_shape=(
            jax.ShapeDtypeStruct((n, d), F32),
            jax.ShapeDtypeStruct(state.shape, F32),
        ),
        grid=(n // rows,),
        in_specs=[
            pl.BlockSpec((rows, d), lambda i: (i, 0)),
            _const_spec((1, d)),
            _const_spec((d, A_W)),
            _const_spec((LANES, GLA_HEADS * GLA_DK_PAD)),
            _const_spec((1, GLA_HEADS * GLA_DK_PAD)),
            _const_spec((1, GLA_HEADS * GLA_DV_PAD)),
            pl.BlockSpec((rows, GLA_HEADS, GLA_DK, GLA_DV), lambda i: (i, 0, 0, 0)),
            pl.BlockSpec((rows, m, MEM_W), lambda i: (i, 0, 0)),
            pl.BlockSpec((rows, m, MEM_W), lambda i: (i, 0, 0)),
            _const_spec((GLA_HEADS * GLA_DV_PAD, d)),
            _const_spec((MEM_W, d)),
        ],
        out_specs=(
            pl.BlockSpec((rows, d), lambda i: (i, 0)),
            pl.BlockSpec((rows, GLA_HEADS, GLA_DK, GLA_DV), lambda i: (i, 0, 0, 0)),
        ),
        scratch_shapes=[
            pltpu.VMEM((rows, GLA_HEADS * GLA_DV_PAD), F32),
            pltpu.VMEM((rows, MEM_W), F32),
        ],
        compiler_params=_params(("parallel",)),
        name="mixer_a_sample",
    )(x, gain.reshape(1, d), wa["w_in"], wa["w_gate2"], wa["b_gate2"], wa["onorm"], state, mem_k, mem_v,
      wa["w_out_main"], wa["w_out_mem"])


def _kvproj_kernel(x_ref, g_ref, w_ref, k_ref, v_ref, *rest, with_attn_copies):
    h = _rms(x_ref[0], g_ref[...]).astype(BF16)
    dqk = k_ref.shape[2]
    k = _dot(h, w_ref[:, :dqk])
    v = _dot(h, w_ref[:, dqk:])
    k_ref[0] = k
    v_ref[0] = v
    if with_attn_copies:
        kb_ref, vt_ref = rest
        kb_ref[0] = k.astype(BF16)
        vt_ref[0] = v.T.astype(BF16)


def _kvproj(x, gain, w, with_attn_copies):
    b, l, d = x.shape
    dqk = DIFF_HEADS * 2 * DIFF_DK
    dv = DIFF_HEADS * DIFF_DV
    tm = min(l, 512)
    assert l % tm == 0
    out_shape = [jax.ShapeDtypeStruct((b, l, dqk), F32), jax.ShapeDtypeStruct((b, l, dv), F32)]
    out_specs = [pl.BlockSpec((1, tm, dqk), lambda i, j: (i, j, 0)),
                 pl.BlockSpec((1, tm, dv), lambda i, j: (i, j, 0))]
    if with_attn_copies:
        out_shape += [jax.ShapeDtypeStruct((b, l, dqk), BF16), jax.ShapeDtypeStruct((b, dv, l), BF16)]
        out_specs += [pl.BlockSpec((1, tm, dqk), lambda i, j: (i, j, 0)),
                      pl.BlockSpec((1, dv, tm), lambda i, j: (i, 0, j))]
    return pl.pallas_call(
        functools.partial(_kvproj_kernel, with_attn_copies=with_attn_copies),
        out_shape=tuple(out_shape),
        grid=(b, l // tm),
        in_specs=[
            pl.BlockSpec((1, tm, d), lambda i, j: (i, j, 0)),
            _const_spec((1, d)),
            _const_spec((d, dqk + dv)),
        ],
        out_specs=tuple(out_specs),
        compiler_params=_params(("parallel", "parallel")),
        name="kvproj_attn" if with_attn_copies else "kvproj",
    )(x, gain.reshape(1, d), w)


def _projb_kernel(x_ref, g_ref, wq_ref, wm_ref, q_ref, qm_ref):
    h = _rms(x_ref[...], g_ref[...]).astype(BF16)
    q_ref[...] = (_dot(h, wq_ref[...]) * (DIFF_DK ** -0.5)).astype(BF16)
    qm_ref[...] = _dot(h, wm_ref[...]).astype(BF16)


def _projb(x, gain, wq, wm):
    m, d = x.shape
    tm = min(m, 512)
    assert m % tm == 0
    nq = wq.shape[1]
    return pl.pallas_call(
        _projb_kernel,
        out_shape=(jax.ShapeDtypeStruct((m, nq), BF16), jax.ShapeDtypeStruct((m, MEM_W), BF16)),
        grid=(m // tm,),
        in_specs=[
            pl.BlockSpec((tm, d), lambda i: (i, 0)),
            _const_spec((1, d)),
            _const_spec((d, nq)),
            _const_spec((d, MEM_W)),
        ],
        out_specs=(pl.BlockSpec((tm, nq), lambda i: (i, 0)), pl.BlockSpec((tm, MEM_W), lambda i: (i, 0))),
        compiler_params=_params(("parallel",)),
        name="projb",
    )(x, gain.reshape(1, d), wq, wm)


def _lambda_full(lam_ref, lam_init):
    lv = lam_ref[...]
    s1 = jnp.sum(lv[0:1] * lv[1:2], axis=-1, keepdims=True)
    s2 = jnp.sum(lv[2:3] * lv[3:4], axis=-1, keepdims=True)
    return jnp.exp(s1) - jnp.exp(s2) + lam_init


def _flash_kernel(qi_tab, ki_tab, q_ref, k_ref, vt_ref, lam_ref, sub_ref, o_ref,
                  qm_scr, m_scr, l_scr, acc_scr, *, ratio, lam_init):
    step = pl.program_id(1)
    qi = qi_tab[step]
    ki = ki_tab[step]
    tq = q_ref.shape[1]
    tk = k_ref.shape[1]

    @pl.when(ki == 0)
    def _():
        qv = q_ref[0].astype(F32)
        half = (lax.broadcasted_iota(jnp.int32, qv.shape, 1) >> 6) & 1
        qm_scr[0] = jnp.where(half == 0, qv, 0.0).astype(BF16)
        qm_scr[1] = jnp.where(half == 1, qv, 0.0).astype(BF16)
        m_scr[...] = jnp.full(m_scr.shape, -jnp.inf, F32)
        l_scr[...] = jnp.zeros(l_scr.shape, F32)
        acc_scr[...] = jnp.zeros(acc_scr.shape, F32)

    def block(masked):
        if masked:
            kpos = ki * tk + lax.broadcasted_iota(jnp.int32, (tk, tq), 0)
            qpos = qi * tq + lax.broadcasted_iota(jnp.int32, (tk, tq), 1)
            keep = kpos <= qpos
        for hd in range(DIFF_HEADS):
            hs = slice(hd * DIFF_DV, (hd + 1) * DIFF_DV)
            kh = k_ref[0, :, hs]
            vth = vt_ref[0, hs, :]
            for c in range(2):
                idx = 2 * hd + c
                st = _dot_nt(kh, qm_scr[c, :, hs])
                if masked:
                    st = jnp.where(keep, st, -jnp.inf)
                m_old = m_scr[idx]
                m_new = jnp.maximum(m_old, jnp.max(st, axis=0, keepdims=True))
                p = jnp.exp(st - m_new)
                alpha = jnp.exp(m_old - m_new)
                l_scr[idx] = alpha * l_scr[idx] + jnp.sum(p, axis=0, keepdims=True)
                acc_scr[idx] = alpha * acc_scr[idx] + _dot(vth, p.astype(BF16))
                m_scr[idx] = m_new

    diag = ki >= qi * ratio

    @pl.when(diag)
    def _():
        block(True)

    @pl.when(jnp.logical_not(diag))
    def _():
        block(False)

    @pl.when(ki == (qi + 1) * ratio - 1)
    def _():
        lam = _lambda_full(lam_ref, lam_init)
        for hd in range(DIFF_HEADS):
            o1 = acc_scr[2 * hd] / l_scr[2 * hd]
            o2 = acc_scr[2 * hd + 1] / l_scr[2 * hd + 1]
            od = o1 - lam * o2
            ms = jnp.mean(od * od, axis=0, keepdims=True)
            on = (od * lax.rsqrt(ms + EPS) * sub_ref[...]) * (1.0 - lam_init)
            o_ref[0, :, hd * DIFF_DV:(hd + 1) * DIFF_DV] = on.T.astype(BF16)


def _flash_prompt(q, kb, vt, lam_vecs, subln, lam_init):
    b, l, dqk = q.shape
    dv = vt.shape[1]
    tq, tk = 256, 256
    assert l % tq == 0 and tq % tk == 0
    ratio = tq // tk
    pairs = [(i, j) for i in range(l // tq) for j in range((i + 1) * ratio)]
    qi_tab = jnp.asarray([p[0] for p in pairs], jnp.int32)
    ki_tab = jnp.asarray([p[1] for p in pairs], jnp.int32)
    nacc = 2 * DIFF_HEADS
    grid_spec = pltpu.PrefetchScalarGridSpec(
        num_scalar_prefetch=2,
        grid=(b, len(pairs)),
        in_specs=[
            pl.BlockSpec((1, tq, dqk), lambda i, s, qt, kt: (i, qt[s], 0)),
            pl.BlockSpec((1, tk, dqk), lambda i, s, qt, kt: (i, kt[s], 0)),
            pl.BlockSpec((1, dv, tk), lambda i, s, qt, kt: (i, 0, kt[s])),
            pl.BlockSpec((4, DIFF_DK), lambda i, s, qt, kt: (0, 0)),
            pl.BlockSpec((DIFF_DV, 1), lambda i, s, qt, kt: (0, 0)),
        ],
        out_specs=pl.BlockSpec((1, tq, dv), lambda i, s, qt, kt: (i, qt[s], 0)),
        scratch_shapes=[
            pltpu.VMEM((2, tq, dqk), BF16),
            pltpu.VMEM((nacc, 1, tq), F32),
            pltpu.VMEM((nacc, 1, tq), F32),
            pltpu.VMEM((nacc, DIFF_DV, tq), F32),
        ],
    )
    return pl.pallas_call(
        functools.partial(_flash_kernel, ratio=ratio, lam_init=lam_init),
        out_shape=jax.ShapeDtypeStruct((b, l, dv), BF16),
        grid_spec=grid_spec,
        compiler_params=_params(("parallel", "arbitrary")),
        name="flash_prompt",
    )(qi_tab, ki_tab, q, kb, vt, lam_vecs, subln.reshape(DIFF_DV, 1))


def _paged_kernel(pt_ref, q_ref, kn_ref, vn_ref, lam_ref, sub_ref, *rest, pages_per_step, lam_init):
    k_refs = rest[:pages_per_step]
    v_refs = rest[pages_per_step:2 * pages_per_step]
    o_ref, qm_scr, m_scr, l_scr, acc_scr = rest[2 * pages_per_step:]
    step = pl.program_id(1)
    nrow = qm_scr.shape[0]
    width = qm_scr.shape[1]

    @pl.when(step == 0)
    def _():
        r = lax.broadcasted_iota(jnp.int32, (nrow, width), 0)
        lane_half = lax.broadcasted_iota(jnp.int32, (nrow, width), 1) >> 6
        qrow = jnp.broadcast_to(q_ref[0].astype(F32), (nrow, width))
        qm_scr[...] = jnp.where(r == lane_half, qrow, 0.0).astype(BF16)
        m_scr[...] = jnp.full(m_scr.shape, -jnp.inf, F32)
        l_scr[...] = jnp.zeros(l_scr.shape, F32)
        acc_scr[...] = jnp.zeros(acc_scr.shape, F32)

    qm = qm_scr[...]
    s = jnp.concatenate([_dot_nt(qm, kr[0].astype(BF16)) for kr in k_refs], axis=1)
    m_old = m_scr[...]
    m_new = jnp.maximum(m_old, jnp.max(s, axis=-1, keepdims=True))
    pf = jnp.exp(s - m_new)
    p = pf.astype(BF16)
    alpha = jnp.exp(m_old - m_new)
    psize = k_refs[0].shape[1]
    pv = _dot(p[:, 0:psize], v_refs[0][0].astype(BF16))
    for i in range(1, pages_per_step):
        pv = pv + _dot(p[:, i * psize:(i + 1) * psize], v_refs[i][0].astype(BF16))
    l_scr[...] = alpha * l_scr[...] + jnp.sum(pf, axis=-1, keepdims=True)
    acc_scr[...] = alpha * acc_scr[...] + pv
    m_scr[...] = m_new

    @pl.when(step == pl.num_programs(1) - 1)
    def _():
        s_self = jnp.sum(qm.astype(F32) * kn_ref[0], axis=-1, keepdims=True)
        m_old = m_scr[...]
        m_fin = jnp.maximum(m_old, s_self)
        p_self = jnp.exp(s_self - m_fin)
        a_fin = jnp.exp(m_old - m_fin)
        l_fin = a_fin * l_scr[...] + p_self
        acc = (a_fin * acc_scr[...] + p_self * vn_ref[0]) / l_fin
        r = lax.broadcasted_iota(jnp.int32, (nrow, width), 0)
        first = (lax.broadcasted_iota(jnp.int32, (nrow, width), 1) >> 7) * 2
        o1 = jnp.sum(jnp.where(r == first, acc, 0.0), axis=0, keepdims=True)
        o2 = jnp.sum(jnp.where(r == first + 1, acc, 0.0), axis=0, keepdims=True)
        od = o1 - _lambda_full(lam_ref, lam_init) * o2
        parts = []
        for hd in range(DIFF_HEADS):
            seg = od[:, hd * DIFF_DV:(hd + 1) * DIFF_DV]
            ms = jnp.mean(seg * seg, axis=-1, keepdims=True)
            parts.append(seg * lax.rsqrt(ms + EPS))
        on = (jnp.concatenate(parts, axis=1) * sub_ref[...]) * (1.0 - lam_init)
        o_ref[0] = on.astype(BF16)


def _paged_sample(q, k_new, v_new, cache_k, cache_v, page_table, lam_vecs, subln, lam_init):
    n, dqk = q.shape
    dv = v_new.shape[1]
    n_phys, psize = cache_k.shape[:2]
    n_pages = page_table.shape[1]
    pps = 8
    assert n_pages % pps == 0 and dqk == dv
    nrow = 16
    ck = cache_k.reshape(n_phys, psize, dqk)
    cv = cache_v.reshape(n_phys, psize, dv)

    def page_spec(i):
        return pl.BlockSpec((1, psize, dqk), lambda bi, s, pt: (pt[bi * n_pages + s * pps + i], 0, 0))

    row_spec = pl.BlockSpec((1, 1, dqk), lambda bi, s, pt: (bi, 0, 0))
    grid_spec = pltpu.PrefetchScalarGridSpec(
        num_scalar_prefetch=1,
        grid=(n, n_pages // pps),
        in_specs=[row_spec, row_spec, row_spec,
                  pl.BlockSpec((4, DIFF_DK), lambda bi, s, pt: (0, 0)),
                  pl.BlockSpec((1, dv), lambda bi, s, pt: (0, 0))]
        + [page_spec(i) for i in range(pps)] + [page_spec(i) for i in range(pps)],
        out_specs=pl.BlockSpec((1, 1, dv), lambda bi, s, pt: (bi, 0, 0)),
        scratch_shapes=[
            pltpu.VMEM((nrow, dqk), BF16),
            pltpu.VMEM((nrow, 1), F32),
            pltpu.VMEM((nrow, 1), F32),
            pltpu.VMEM((nrow, dv), F32),
        ],
    )
    out = pl.pallas_call(
        functools.partial(_paged_kernel, pages_per_step=pps, lam_init=lam_init),
        out_shape=jax.ShapeDtypeStruct((n, 1, dv), BF16),
        grid_spec=grid_spec,
        compiler_params=_params(("parallel", "arbitrary")),
        name="paged_sample",
    )(page_table.reshape(-1), q.reshape(n, 1, dqk), k_new.reshape(n, 1, dqk), v_new.reshape(n, 1, dv),
      lam_vecs, jnp.tile(subln, DIFF_HEADS).reshape(1, dv), *([ck] * pps), *([cv] * pps))
    return out.reshape(n, dv)


def _post_prompt_kernel(x_ref, om_ref, qm_ref, mk_ref, mv_ref, wom_ref, wome_ref, xo_ref):
    o_mem = _mem_attend_shared(qm_ref[0], mk_ref[0], mv_ref[0].astype(BF16))
    xo_ref[0] = x_ref[0] + _dot(om_ref[0], wom_ref[...]) + _dot(o_mem.astype(BF16), wome_ref[...])


def _post_prompt(x, o_main, q_mem, mem_k, mem_v, wom, wome):
    b, l, d = x.shape
    tm = 512
    assert l % tm == 0
    dm = o_main.shape[2]
    m = mem_k.shape[1]
    return pl.pallas_call(
        _post_prompt_kernel,
        out_shape=jax.ShapeDtypeStruct((b, l, d), F32),
        grid=(b, l // tm),
        in_specs=[
            pl.BlockSpec((1, tm, d), lambda i, j: (i, j, 0)),
            pl.BlockSpec((1, tm, dm), lambda i, j: (i, j, 0)),
            pl.BlockSpec((1, tm, MEM_W), lambda i, j: (i, j, 0)),
            pl.BlockSpec((1, m, MEM_W), lambda i, j: (i, 0, 0)),
            pl.BlockSpec((1, m, MEM_W), lambda i, j: (i, 0, 0)),
            _const_spec((dm, d)),
            _const_spec((MEM_W, d)),
        ],
        out_specs=pl.BlockSpec((1, tm, d), lambda i, j: (i, j, 0)),
        compiler_params=_params(("parallel", "parallel")),
        name="post_prompt",
    )(x, o_main, q_mem, mem_k, mem_v, wom, wome)


def _post_sample_kernel(x_ref, om_ref, qm_ref, mk_ref, mv_ref, wom_ref, wome_ref, xo_ref, om_scr):
    _mem_attend_rows(qm_ref[...].astype(F32), mk_ref, mv_ref, om_scr)
    xo_ref[...] = (x_ref[...] + _dot(om_ref[...], wom_ref[...])
                   + _dot(om_scr[...].astype(BF16), wome_ref[...]))


def _post_sample(x, o_main, q_mem, mem_k, mem_v, wom, wome):
    n, d = x.shape
    rows = 2 * SUBLANES
    assert n % rows == 0
    dm = o_main.shape[1]
    m = mem_k.shape[1]
    return pl.pallas_call(
        _post_sample_kernel,
        out_shape=jax.ShapeDtypeStruct((n, d), F32),
        grid=(n // rows,),
        in_specs=[
            pl.BlockSpec((rows, d), lambda i: (i, 0)),
            pl.BlockSpec((rows, dm), lambda i: (i, 0)),
            pl.BlockSpec((rows, MEM_W), lambda i: (i, 0)),
            pl.BlockSpec((rows, m, MEM_W), lambda i: (i, 0, 0)),
            pl.BlockSpec((rows, m, MEM_W), lambda i: (i, 0, 0)),
            _const_spec((dm, d)),
            _const_spec((MEM_W, d)),
        ],
        out_specs=pl.BlockSpec((rows, d), lambda i: (i, 0)),
        scratch_shapes=[pltpu.VMEM((rows, MEM_W), F32)],
        compiler_params=_params(("parallel",)),
        name="post_sample",
    )(x, o_main, q_mem, mem_k, mem_v, wom, wome)


def _pad_heads_cols(w, heads, width, padded):
    lead = w.shape[:-1]
    w = w.reshape(lead + (heads, width))
    w = jnp.pad(w, [(0, 0)] * len(lead) + [(0, 0), (0, padded - width)])
    return w.reshape(lead + (heads * padded,))


def _prep_layer_a(w_in, w_gate2, b_gate2, onorm, w_out):
    dqk = GLA_HEADS * GLA_DK
    dv = GLA_HEADS * GLA_DV
    o = 0
    wq = w_in[:, o:o + dqk]; o += dqk
    wk = w_in[:, o:o + dqk]; o += dqk
    wv = w_in[:, o:o + dv]; o += dv
    wg = w_in[:, o:o + GLA_GATE_RANK]; o += GLA_GATE_RANK
    wr = w_in[:, o:o + dv]; o += dv
    wm = w_in[:, o:o + MEM_W]
    w_all = jnp.concatenate([
        _pad_heads_cols(wq, GLA_HEADS, GLA_DK, GLA_DK_PAD),
        _pad_heads_cols(wk, GLA_HEADS, GLA_DK, GLA_DK_PAD),
        _pad_heads_cols(wv, GLA_HEADS, GLA_DV, GLA_DV_PAD),
        jnp.pad(wg, ((0, 0), (0, LANES - GLA_GATE_RANK))),
        _pad_heads_cols(wr, GLA_HEADS, GLA_DV, GLA_DV_PAD),
        wm,
    ], axis=1).astype(BF16)
    wg2 = _pad_heads_cols(w_gate2, GLA_HEADS, GLA_DK, GLA_DK_PAD)
    wg2 = jnp.pad(wg2, ((0, LANES - GLA_GATE_RANK), (0, 0))).astype(BF16)
    bg2 = _pad_heads_cols(b_gate2.reshape(1, dqk), GLA_HEADS, GLA_DK, GLA_DK_PAD)
    gain = _pad_heads_cols(jnp.tile(onorm, GLA_HEADS).reshape(1, dv), GLA_HEADS, GLA_DV, GLA_DV_PAD)
    wo_main = _pad_heads_cols(w_out[:dv].T, GLA_HEADS, GLA_DV, GLA_DV_PAD).T.astype(BF16)
    return {
        "w_in": w_all, "w_gate2": wg2, "b_gate2": bg2, "onorm": gain,
        "w_out_main": wo_main, "w_out_mem": w_out[dv:].astype(BF16),
    }


def kernel(x_prompt, x_sample, mem_prompt, state_gla, cache_k, cache_v, cache_mem_k, cache_mem_v,
           page_table, ffn1_norm, ffn1_w_gate, ffn1_w_up, ffn1_w_down, mix_norm, a_w_in, a_w_gate2,
           a_b_gate2, a_onorm, b_w_in, b_lambda_q1, b_lambda_k1, b_lambda_q2, b_lambda_k2, b_subln,
           mem_norm, w_mem_kv, w_out, ffn2_norm, ffn2_w_gate, ffn2_w_up, ffn2_w_down, kv_norm, w_kv,
           final_norm):
    bp, seq, d = x_prompt.shape
    db, dseq, _ = x_sample.shape
    assert dseq == 1
    depth = ffn1_norm.shape[0]
    n_a = a_w_in.shape[0]
    mtok = mem_prompt.shape[1]
    dqk = DIFF_HEADS * 2 * DIFF_DK
    dv = DIFF_HEADS * DIFF_DV

    bf = lambda t: t.astype(BF16)
    f1 = (bf(ffn1_w_gate), bf(ffn1_w_up), bf(ffn1_w_down))
    f2 = (bf(ffn2_w_gate), bf(ffn2_w_up), bf(ffn2_w_down))
    w_kv_b = bf(w_kv)
    w_mem_b = bf(w_mem_kv)

    mem_k_p, mem_v_p = _memkv(mem_prompt.reshape(bp * mtok, d), mem_norm, w_mem_b)
    mem_k_p = mem_k_p.reshape(depth, bp, mtok, MEM_W)
    mem_v_p = mem_v_p.reshape(depth, bp, mtok, MEM_W)
    cmk = cache_mem_k.reshape(depth, db, mtok, MEM_W)
    cmv = cache_mem_v.reshape(depth, db, mtok, MEM_W)

    xp = x_prompt
    xs = x_sample.reshape(db, d)
    states_p, states_s = [], []
    kv_p = kv_s = None
    for i in range(depth):
        last = i == depth - 1
        xp = _ffn(xp.reshape(bp * seq, d), ffn1_norm[i], f1[0][i], f1[1][i], f1[2][i]).reshape(bp, seq, d)
        xs = _ffn(xs, ffn1_norm[i], f1[0][i], f1[1][i], f1[2][i])
        if i < n_a:
            wa = _prep_layer_a(a_w_in[i], a_w_gate2[i], a_b_gate2[i], a_onorm[i], w_out[i])
            xp, st_p = _mixer_a_prompt(xp, mix_norm[i], wa, mem_k_p[i], mem_v_p[i])
            xs, st_s = _mixer_a_sample(xs, mix_norm[i], wa, state_gla[i], cmk[i], cmv[i])
            states_p.append(st_p)
            states_s.append(st_s)
        else:
            j = i - n_a
            lam_init = 0.8 - 0.6 * math.exp(-0.3 * i)
            lam_vecs = jnp.stack([b_lambda_q1[j], b_lambda_k1[j], b_lambda_q2[j], b_lambda_k2[j]])
            wq = bf(b_w_in[j][:, :dqk])
            wm = bf(b_w_in[j][:, dqk:])
            wom = bf(w_out[i][:dv])
            wome = bf(w_out[i][dv:])
            k_p, v_p, kb_p, vt_p = kv_p
            q_p, qm_p = _projb(xp.reshape(bp * seq, d), mix_norm[i], wq, wm)
            o_p = _flash_prompt(q_p.reshape(bp, seq, dqk), kb_p, vt_p, lam_vecs, b_subln[j], lam_init)
            xp = _post_prompt(xp, o_p, qm_p.reshape(bp, seq, MEM_W), mem_k_p[i], mem_v_p[i], wom, wome)
            k_s, v_s = kv_s
            q_s, qm_s = _projb(xs, mix_norm[i], wq, wm)
            o_s = _paged_sample(q_s, k_s.reshape(db, dqk), v_s.reshape(db, dv), cache_k, cache_v,
                                page_table, lam_vecs, b_subln[j], lam_init)
            xs = _post_sample(xs, o_s, qm_s, cmk[i], cmv[i], wom, wome)
        fg = final_norm if last else None
        xp = _ffn(xp.reshape(bp * seq, d), ffn2_norm[i], f2[0][i], f2[1][i], f2[2][i], fg).reshape(bp, seq, d)
        xs = _ffn(xs, ffn2_norm[i], f2[0][i], f2[1][i], f2[2][i], fg)
        if i == n_a - 1:
            kv_p = _kvproj(xp, kv_norm, w_kv_b, True)
            kv_s = _kvproj(xs.reshape(1, db, d), kv_norm, w_kv_b, False)

    y_prompt = xp
    y_sample = xs.reshape(db, 1, d)
    k_p, v_p = kv_p[0], kv_p[1]
    k_s, v_s = kv_s
    return (
        y_prompt,
        y_sample,
        jnp.stack(states_p),
        jnp.stack(states_s),
        k_p.reshape(bp, seq, DIFF_HEADS, 2, DIFF_DK),
        v_p.reshape(bp, seq, DIFF_HEADS, DIFF_DV),
        k_s.reshape(db, 1, DIFF_HEADS, 2, DIFF_DK),
        v_s.reshape(db, 1, DIFF_HEADS, DIFF_DV),
        mem_k_p.reshape(depth, bp, mtok, MEM_HEADS, MEM_DH),
        mem_v_p.reshape(depth, bp, mtok, MEM_HEADS, MEM_DH),
    )
```

```python
import functools
import math

import jax
import jax.numpy as jnp
from jax import lax
from jax.experimental import pallas as pl
from jax.experimental.pallas import tpu as pltpu

F32 = jnp.float32
BF16 = jnp.bfloat16

EPS = 1e-6
GLA_HEADS = 4
GLA_DK = 96
GLA_DV = 192
GLA_GATE_RANK = 16
GLA_GATE_NORM = 16.0
DIFF_HEADS = 6
DIFF_DK = 64
DIFF_DV = 2 * DIFF_DK
MEM_HEADS = 4
MEM_DH = 64
MEM_W = MEM_HEADS * MEM_DH

LANES = 128
SUBLANES = 8
VMEM_LIMIT_BYTES = 56 * 1024 * 1024

GLA_DK_PAD = 128
GLA_DV_PAD = 256
GLA_CHUNK = 64

A_Q0 = 0
A_K0 = A_Q0 + GLA_HEADS * GLA_DK_PAD
A_V0 = A_K0 + GLA_HEADS * GLA_DK_PAD
A_G0 = A_V0 + GLA_HEADS * GLA_DV_PAD
A_R0 = A_G0 + LANES
A_M0 = A_R0 + GLA_HEADS * GLA_DV_PAD
A_W = A_M0 + MEM_W


def _dot(a, b):
    return jnp.dot(a, b, preferred_element_type=F32)


def _dot_nt(a, b):
    return lax.dot_general(a, b, (((1,), (1,)), ((), ())), preferred_element_type=F32)


def _rms(x, g):
    ms = jnp.mean(x * x, axis=-1, keepdims=True)
    return x * lax.rsqrt(ms + EPS) * g


def _const_spec(shape):
    nd = len(shape)
    return pl.BlockSpec(shape, lambda *_: (0,) * nd, pipeline_mode=pl.Buffered(1))


def _params(sem):
    return pltpu.CompilerParams(dimension_semantics=sem, vmem_limit_bytes=VMEM_LIMIT_BYTES)


def _ffn_kernel(x_ref, g_ref, wg_ref, wu_ref, wd_ref, *rest, ff_chunk, final_norm):
    if final_norm:
        fg_ref, o_ref = rest
    else:
        (o_ref,) = rest
    x = x_ref[...]
    h = _rms(x, g_ref[...]).astype(BF16)
    d_ff = wg_ref.shape[1]
    acc = jnp.zeros(x.shape, F32)
    for c in range(d_ff // ff_chunk):
        sl = slice(c * ff_chunk, (c + 1) * ff_chunk)
        g = _dot(h, wg_ref[:, sl])
        u = _dot(h, wu_ref[:, sl])
        a = (g * jax.nn.sigmoid(g) * u).astype(BF16)
        acc = acc + _dot(a, wd_ref[sl, :])
    y = x + 0.5 * acc
    if final_norm:
        y = _rms(y, fg_ref[...])
    o_ref[...] = y


def _ffn(x, gain, wg, wu, wd, final_gain=None):
    m, d = x.shape
    d_ff = wg.shape[1]
    tm = min(m, 512)
    assert m % tm == 0 and d_ff % 256 == 0
    final_norm = final_gain is not None
    in_specs = [
        pl.BlockSpec((tm, d), lambda i: (i, 0)),
        _const_spec((1, d)),
        _const_spec((d, d_ff)),
        _const_spec((d, d_ff)),
        _const_spec((d_ff, d)),
    ]
    args = [x, gain.reshape(1, d), wg, wu, wd]
    if final_norm:
        in_specs.append(_const_spec((1, d)))
        args.append(final_gain.reshape(1, d))
    return pl.pallas_call(
        functools.partial(_ffn_kernel, ff_chunk=256, final_norm=final_norm),
        out_shape=jax.ShapeDtypeStruct((m, d), F32),
        grid=(m // tm,),
        in_specs=in_specs,
        out_specs=pl.BlockSpec((tm, d), lambda i: (i, 0)),
        compiler_params=_params(("parallel",)),
        name="ffn_final" if final_norm else "ffn",
    )(*args)


def _memkv_kernel(m_ref, g_ref, w_ref, kt_ref, vt_ref):
    h = _rms(m_ref[0], g_ref[0]).astype(BF16)
    kv = _dot(h, w_ref[0])
    kt_ref[0, 0] = kv[:, :MEM_W].T
    vt_ref[0, 0] = kv[:, MEM_W:].T


def _memkv(mem, gains, w):
    depth = w.shape[0]
    b, m, d = mem.shape
    out = jax.ShapeDtypeStruct((depth, b, MEM_W, m), F32)
    return pl.pallas_call(
        _memkv_kernel,
        out_shape=(out, out),
        grid=(depth, b),
        in_specs=[
            pl.BlockSpec((1, m, d), lambda i, j: (j, 0, 0)),
            pl.BlockSpec((1, 1, d), lambda i, j: (i, 0, 0)),
            pl.BlockSpec((1, d, 2 * MEM_W), lambda i, j: (i, 0, 0)),
        ],
        out_specs=(
            pl.BlockSpec((1, 1, MEM_W, m), lambda i, j: (i, j, 0, 0)),
            pl.BlockSpec((1, 1, MEM_W, m), lambda i, j: (i, j, 0, 0)),
        ),
        compiler_params=_params(("arbitrary", "arbitrary")),
        name="memkv",
    )(mem, gains.reshape(depth, 1, d), w)


def _log_sigmoid(z):
    return jnp.minimum(z, 0.0) - jnp.log1p(jnp.exp(-jnp.abs(z)))


def _gla_inputs(h, win_ref, wg2_ref, bg2_ref):
    q = _dot(h, win_ref[:, A_Q0:A_K0])
    k = _dot(h, win_ref[:, A_K0:A_V0])
    v = _dot(h, win_ref[:, A_V0:A_G0])
    g_lr = _dot(h, win_ref[:, A_G0:A_R0])
    r = _dot(h, win_ref[:, A_R0:A_M0])
    q_mem = _dot(h, win_ref[:, A_M0:A_W])
    z = _dot(g_lr.astype(BF16), wg2_ref[...]) + bg2_ref[...]
    logg = _log_sigmoid(z) / GLA_GATE_NORM
    return q, k, v, logg, r, q_mem


def _gla_output_gate(o, r, gain):
    parts = []
    for h in range(GLA_HEADS):
        sl = slice(h * GLA_DV_PAD, (h + 1) * GLA_DV_PAD)
        oh = o[:, sl]
        ms = jnp.sum(oh * oh, axis=-1, keepdims=True) * (1.0 / GLA_DV)
        parts.append(oh * lax.rsqrt(ms + EPS) * gain[:, sl])
    on = jnp.concatenate(parts, axis=1)
    return on * (r * jax.nn.sigmoid(r))


def _mem_attend_shared(qm, mkt, mvt):
    row_head = lax.broadcasted_iota(jnp.int32, mkt.shape, 0) >> 6
    out_head = lax.broadcasted_iota(jnp.int32, (qm.shape[0], MEM_W), 1) >> 6
    o = jnp.zeros((qm.shape[0], MEM_W), F32)
    for h in range(MEM_HEADS):
        mkh = jnp.where(row_head == h, mkt, 0.0).astype(BF16)
        s = _dot(qm, mkh) * (MEM_DH ** -0.5)
        e = jnp.exp(s - jnp.max(s, axis=-1, keepdims=True))
        l = jnp.sum(e, axis=-1, keepdims=True)
        oh = _dot_nt(e.astype(BF16), mvt) / l
        o = o + jnp.where(out_head == h, oh, 0.0)
    return o


def _mem_attend_rows(qmem, mkt_ref, mvt_ref, om_scr):
    rows = qmem.shape[0]
    r8 = lax.broadcasted_iota(jnp.int32, (SUBLANES, MEM_W), 0)
    lh = lax.broadcasted_iota(jnp.int32, (SUBLANES, MEM_W), 1) >> 6
    for j in range(rows):
        qrow = jnp.broadcast_to(qmem[j:j + 1], (SUBLANES, MEM_W))
        qm = jnp.where(r8 == lh, qrow, 0.0).astype(BF16)
        s = _dot(qm, mkt_ref[j].astype(BF16)) * (MEM_DH ** -0.5)
        e = jnp.exp(s - jnp.max(s, axis=-1, keepdims=True))
        l = jnp.sum(e, axis=-1, keepdims=True)
        oh = _dot_nt(e.astype(BF16), mvt_ref[j].astype(BF16)) / l
        om_scr[j:j + 1, :] = jnp.sum(jnp.where(r8 == lh, oh, 0.0), axis=0, keepdims=True)


def _mixer_a_prompt_kernel(x_ref, g_ref, win_ref, wg2_ref, bg2_ref, on_ref, mk_ref, mv_ref,
                           wom_ref, wome_ref, xo_ref, st_ref, s_scr):
    tl = x_ref.shape[1]
    li = pl.program_id(1)

    @pl.when(li == 0)
    def _():
        s_scr[...] = jnp.zeros(s_scr.shape, F32)

    x = x_ref[0]
    h = _rms(x, g_ref[...]).astype(BF16)
    q, k, v, logg, r, q_mem = _gla_inputs(h, win_ref, wg2_ref, bg2_ref)

    row = lax.broadcasted_iota(jnp.int32, (tl, tl), 0)
    col = lax.broadcasted_iota(jnp.int32, (tl, tl), 1)
    tril = ((row >> 6) == (col >> 6)) & (col <= row)
    trilb = jnp.where(tril, 1.0, 0.0).astype(BF16)
    hi = logg.astype(BF16)
    lo = (logg - hi.astype(F32)).astype(BF16)
    b = _dot(trilb, hi) + _dot(trilb, lo)

    lane_chunk = lax.broadcasted_iota(jnp.int32, (GLA_DK_PAD, tl), 1) >> 6
    scale = GLA_DK ** -0.5
    o_parts = []
    for hd in range(GLA_HEADS):
        ks = slice(hd * GLA_DK_PAD, (hd + 1) * GLA_DK_PAD)
        vs = slice(hd * GLA_DV_PAD, (hd + 1) * GLA_DV_PAD)
        bh = b[:, ks]
        qd = (q[:, ks] * scale) * jnp.exp(bh)
        ki = k[:, ks] * jnp.exp(-bh)
        qdb = qd.astype(BF16)
        att = jnp.where(tril, _dot_nt(qdb, ki.astype(BF16)), 0.0)
        vh = v[:, vs].astype(BF16)
        o_intra = _dot(att.astype(BF16), vh)
        ki_t = ki.T
        b_t = bh.T
        state = s_scr[hd]
        inter = []
        for c in range(tl // GLA_CHUNK):
            last = c * GLA_CHUNK + GLA_CHUNK - 1
            dec = jnp.exp(b_t[:, last:last + 1])
            inter.append(_dot(qdb[c * GLA_CHUNK:(c + 1) * GLA_CHUNK], state.astype(BF16)))
            kend_t = jnp.where(lane_chunk == c, ki_t * dec, 0.0).astype(BF16)
            state = dec * state + _dot(kend_t, vh)
        s_scr[hd] = state
        o_parts.append(o_intra + jnp.concatenate(inter, axis=0))
    o = jnp.concatenate(o_parts, axis=1)

    o_main = _gla_output_gate(o, r, on_ref[...])
    o_mem = _mem_attend_shared(q_mem.astype(BF16), mk_ref[0], mv_ref[0].astype(BF16))
    xo_ref[0] = x + _dot(o_main.astype(BF16), wom_ref[...]) + _dot(o_mem.astype(BF16), wome_ref[...])

    @pl.when(li == pl.num_programs(1) - 1)
    def _():
        for hd in range(GLA_HEADS):
            st_ref[0, hd] = s_scr[hd][:GLA_DK, :GLA_DV]


def _mixer_a_prompt(x, gain, wa, mem_k, mem_v):
    b, l, d = x.shape
    tl = 256
    assert l % tl == 0 and tl % GLA_CHUNK == 0
    m = mem_k.shape[2]
    return pl.pallas_call(
        _mixer_a_prompt_kernel,
        out_shape=(
            jax.ShapeDtypeStruct((b, l, d), F32),
            jax.ShapeDtypeStruct((b, GLA_HEADS, GLA_DK, GLA_DV), F32),
        ),
        grid=(b, l // tl),
        in_specs=[
            pl.BlockSpec((1, tl, d), lambda i, j: (i, j, 0)),
            _const_spec((1, d)),
            _const_spec((d, A_W)),
            _const_spec((LANES, GLA_HEADS * GLA_DK_PAD)),
            _const_spec((1, GLA_HEADS * GLA_DK_PAD)),
            _const_spec((1, GLA_HEADS * GLA_DV_PAD)),
            pl.BlockSpec((1, MEM_W, m), lambda i, j: (i, 0, 0)),
            pl.BlockSpec((1, MEM_W, m), lambda i, j: (i, 0, 0)),
            _const_spec((GLA_HEADS * GLA_DV_PAD, d)),
            _const_spec((MEM_W, d)),
        ],
        out_specs=(
            pl.BlockSpec((1, tl, d), lambda i, j: (i, j, 0)),
            pl.BlockSpec((1, GLA_HEADS, GLA_DK, GLA_DV), lambda i, j: (i, 0, 0, 0)),
        ),
        scratch_shapes=[pltpu.VMEM((GLA_HEADS, GLA_DK_PAD, GLA_DV_PAD), F32)],
        compiler_params=_params(("parallel", "arbitrary")),
        name="mixer_a_prompt",
    )(x, gain.reshape(1, d), wa["w_in"], wa["w_gate2"], wa["b_gate2"], wa["onorm"], mem_k, mem_v,
      wa["w_out_main"], wa["w_out_mem"])


def _columns(t):
    pad = jnp.zeros((LANES - t.shape[0], t.shape[1]), F32)
    return jnp.concatenate([t, pad], axis=0).T


def _mixer_a_sample_kernel(x_ref, g_ref, win_ref, wg2_ref, bg2_ref, on_ref, st_ref, mk_ref, mv_ref,
                           wom_ref, wome_ref, xo_ref, sto_ref, o_scr, om_scr):
    rows = x_ref.shape[0]
    x = x_ref[...]
    h = _rms(x, g_ref[...]).astype(BF16)
    q, k, v, logg, r, q_mem = _gla_inputs(h, win_ref, wg2_ref, bg2_ref)
    a_t = _columns(jnp.exp(logg))
    k_t = _columns(k)
    q_t = _columns(q * (GLA_DK ** -0.5))
    o_scr[...] = jnp.zeros(o_scr.shape, F32)
    for j in range(rows):
        for hd in range(GLA_HEADS):
            rs = slice(hd * GLA_DK_PAD, hd * GLA_DK_PAD + GLA_DK)
            vs = slice(hd * GLA_DV_PAD, hd * GLA_DV_PAD + GLA_DV)
            s_new = a_t[rs, j:j + 1] * st_ref[j, hd] + k_t[rs, j:j + 1] * v[j:j + 1, vs]
            sto_ref[j, hd] = s_new
            o_scr[j:j + 1, vs] = jnp.sum(q_t[rs, j:j + 1] * s_new, axis=0, keepdims=True)
    o_main = _gla_output_gate(o_scr[...], r, on_ref[...])
    _mem_attend_rows(q_mem, mk_ref, mv_ref, om_scr)
    xo_ref[...] = (x + _dot(o_main.astype(BF16), wom_ref[...])
                   + _dot(om_scr[...].astype(BF16), wome_ref[...]))


def _mixer_a_sample(x, gain, wa, state, mem_k, mem_v):
    n, d = x.shape
    rows = SUBLANES
    assert n % rows == 0
    m = mem_k.shape[2]
    return pl.pallas_call(
        _mixer_a_sample_kernel,
        out_shape=(
            jax.ShapeDtypeStruct((n, d), F32),
            jax.ShapeDtypeStruct(state.shape, F32),
        ),
        grid=(n // rows,),
        in_specs=[
            pl.BlockSpec((rows, d), lambda i: (i, 0)),
            _const_spec((1, d)),
            _const_spec((d, A_W)),
            _const_spec((LANES, GLA_HEADS * GLA_DK_PAD)),
            _const_spec((1, GLA_HEADS * GLA_DK_PAD)),
            _const_spec((1, GLA_HEADS * GLA_DV_PAD)),
            pl.BlockSpec((rows, GLA_HEADS, GLA_DK, GLA_DV), lambda i: (i, 0, 0, 0)),
            pl.BlockSpec((rows, MEM_W, m), lambda i: (i, 0, 0)),
            pl.BlockSpec((rows, MEM_W, m), lambda i: (i, 0, 0)),
            _const_spec((GLA_HEADS * GLA_DV_PAD, d)),
            _const_spec((MEM_W, d)),
        ],
        out_specs=(
            pl.BlockSpec((rows, d), lambda i: (i, 0)),
            pl.BlockSpec((rows, GLA_HEADS, GLA_DK, GLA_DV), lambda i: (i, 0, 0, 0)),
        ),
        scratch_shapes=[
            pltpu.VMEM((rows, GLA_HEADS * GLA_DV_PAD), F32),
            pltpu.VMEM((rows, MEM_W), F32),
        ],
        compiler_params=_params(("parallel",)),
        name="mixer_a_sample",
    )(x, gain.reshape(1, d), wa["w_in"], wa["w_gate2"], wa["b_gate2"], wa["onorm"], state, mem_k, mem_v,
      wa["w_out_main"], wa["w_out_mem"])


def _kvproj_kernel(x_ref, g_ref, w_ref, *out_refs, prompt):
    h = _rms(x_ref[0], g_ref[...]).astype(BF16)
    dqk = DIFF_HEADS * 2 * DIFF_DK
    k = _dot(h, w_ref[:, :dqk])
    v = _dot(h, w_ref[:, dqk:])
    if prompt:
        kt_ref, vh_ref, kb_ref, vt_ref = out_refs
        vt = v.T
        kt_ref[0] = k.T
        for hd in range(DIFF_HEADS):
            vh_ref[0, hd] = v[:, hd * DIFF_DV:(hd + 1) * DIFF_DV]
        kb_ref[0] = k.astype(BF16)
        vt_ref[0] = vt.astype(BF16)
    else:
        k_ref, v_ref = out_refs
        k_ref[0] = k
        v_ref[0] = v


def _kvproj(x, gain, w, prompt):
    b, l, d = x.shape
    dqk = DIFF_HEADS * 2 * DIFF_DK
    dv = DIFF_HEADS * DIFF_DV
    tm = min(l, 512)
    assert l % tm == 0
    if prompt:
        out_shape = (jax.ShapeDtypeStruct((b, dqk, l), F32),
                     jax.ShapeDtypeStruct((b, DIFF_HEADS, l, DIFF_DV), F32),
                     jax.ShapeDtypeStruct((b, l, dqk), BF16),
                     jax.ShapeDtypeStruct((b, dv, l), BF16))
        out_specs = (pl.BlockSpec((1, dqk, tm), lambda i, j: (i, 0, j)),
                     pl.BlockSpec((1, DIFF_HEADS, tm, DIFF_DV), lambda i, j: (i, 0, j, 0)),
                     pl.BlockSpec((1, tm, dqk), lambda i, j: (i, j, 0)),
                     pl.BlockSpec((1, dv, tm), lambda i, j: (i, 0, j)))
    else:
        out_shape = (jax.ShapeDtypeStruct((b, l, dqk), F32), jax.ShapeDtypeStruct((b, l, dv), F32))
        out_specs = (pl.BlockSpec((1, tm, dqk), lambda i, j: (i, j, 0)),
                     pl.BlockSpec((1, tm, dv), lambda i, j: (i, j, 0)))
    return pl.pallas_call(
        functools.partial(_kvproj_kernel, prompt=prompt),
        out_shape=out_shape,
        grid=(b, l // tm),
        in_specs=[
            pl.BlockSpec((1, tm, d), lambda i, j: (i, j, 0)),
            _const_spec((1, d)),
            _const_spec((d, dqk + dv)),
        ],
        out_specs=out_specs,
        compiler_params=_params(("parallel", "parallel")),
        name="kvproj_prompt" if prompt else "kvproj_sample",
    )(x, gain.reshape(1, d), w)


LOG2E = math.log2(math.e)


def _projb_kernel(x_ref, g_ref, wq_ref, wm_ref, q_ref, qm_ref):
    h = _rms(x_ref[...], g_ref[...]).astype(BF16)
    q_ref[...] = (_dot(h, wq_ref[...]) * (DIFF_DK ** -0.5 * LOG2E)).astype(BF16)
    qm_ref[...] = _dot(h, wm_ref[...]).astype(BF16)


def _projb(x, gain, wq, wm):
    m, d = x.shape
    tm = min(m, 512)
    assert m % tm == 0
    nq = wq.shape[1]
    return pl.pallas_call(
        _projb_kernel,
        out_shape=(jax.ShapeDtypeStruct((m, nq), BF16), jax.ShapeDtypeStruct((m, MEM_W), BF16)),
        grid=(m // tm,),
        in_specs=[
            pl.BlockSpec((tm, d), lambda i: (i, 0)),
            _const_spec((1, d)),
            _const_spec((d, nq)),
            _const_spec((d, MEM_W)),
        ],
        out_specs=(pl.BlockSpec((tm, nq), lambda i: (i, 0)), pl.BlockSpec((tm, MEM_W), lambda i: (i, 0))),
        compiler_params=_params(("parallel",)),
        name="projb",
    )(x, gain.reshape(1, d), wq, wm)


def _lambda_full(lam_ref, lam_init):
    lv = lam_ref[...]
    s1 = jnp.sum(lv[0:1] * lv[1:2], axis=-1, keepdims=True)
    s2 = jnp.sum(lv[2:3] * lv[3:4], axis=-1, keepdims=True)
    return jnp.exp(s1) - jnp.exp(s2) + lam_init


ONES_ROWS = 16


def _flash_kernel(qi_tab, ki_tab, q_ref, k_ref, vt_ref, lam_ref, sub_ref, o_ref,
                  qm_scr, m_scr, acc_scr, *, ratio, lam_init):
    step = pl.program_id(1)
    qi = qi_tab[step]
    ki = ki_tab[step]
    tq = q_ref.shape[1]
    tk = k_ref.shape[1]

    @pl.when(ki == 0)
    def _():
        for hd in range(DIFF_HEADS):
            qh = q_ref[0, :, hd * DIFF_DV:(hd + 1) * DIFF_DV].astype(F32)
            first = lax.broadcasted_iota(jnp.int32, qh.shape, 1) < DIFF_DK
            qm_scr[hd, :tq] = jnp.where(first, qh, 0.0).astype(BF16)
            qm_scr[hd, tq:] = jnp.where(first, 0.0, qh).astype(BF16)
        m_scr[...] = jnp.full(m_scr.shape, -jnp.inf, F32)
        acc_scr[...] = jnp.zeros(acc_scr.shape, F32)

    def block(masked):
        if masked:
            col = lax.broadcasted_iota(jnp.int32, (tk, 2 * tq), 1)
            kpos = ki * tk + lax.broadcasted_iota(jnp.int32, (tk, 2 * tq), 0)
            keep = kpos <= qi * tq + jnp.where(col >= tq, col - tq, col)
        ones = jnp.ones((ONES_ROWS, tk), BF16)
        for hd in range(DIFF_HEADS):
            hs = slice(hd * DIFF_DV, (hd + 1) * DIFF_DV)
            st = _dot_nt(k_ref[0, :, hs], qm_scr[hd])
            if masked:
                st = jnp.where(keep, st, -jnp.inf)
            m_old = m_scr[hd]
            m_new = jnp.maximum(m_old, jnp.max(st, axis=0, keepdims=True))
            p = jnp.exp2(st - m_new).astype(BF16)
            alpha = jnp.exp2(m_old - m_new)
            v_aug = jnp.concatenate([vt_ref[0, hs, :], ones], axis=0)
            acc_scr[hd] = alpha * acc_scr[hd] + _dot(v_aug, p)
            m_scr[hd] = m_new

    diag = ki >= qi * ratio

    @pl.when(diag)
    def _():
        block(True)

    @pl.when(jnp.logical_not(diag))
    def _():
        block(False)

    @pl.when(ki == (qi + 1) * ratio - 1)
    def _():
        lam = _lambda_full(lam_ref, lam_init)
        for hd in range(DIFF_HEADS):
            acc = acc_scr[hd]
            o = acc[:DIFF_DV] / acc[DIFF_DV:DIFF_DV + 1]
            od = o[:, :tq] - lam * o[:, tq:]
            ms = jnp.mean(od * od, axis=0, keepdims=True)
            on = (od * lax.rsqrt(ms + EPS) * sub_ref[...]) * (1.0 - lam_init)
            o_ref[0, :, hd * DIFF_DV:(hd + 1) * DIFF_DV] = on.T.astype(BF16)


def _flash_prompt(q, kb, vt, lam_vecs, subln, lam_init):
    b, l, dqk = q.shape
    dv = vt.shape[1]
    tq = tk = min(l, 512)
    assert l % tq == 0 and tq % tk == 0
    ratio = tq // tk
    pairs = [(i, j) for i in range(l // tq) for j in range((i + 1) * ratio)]
    qi_tab = jnp.asarray([p[0] for p in pairs], jnp.int32)
    ki_tab = jnp.asarray([p[1] for p in pairs], jnp.int32)
    grid_spec = pltpu.PrefetchScalarGridSpec(
        num_scalar_prefetch=2,
        grid=(b, len(pairs)),
        in_specs=[
            pl.BlockSpec((1, tq, dqk), lambda i, s, qt, kt: (i, qt[s], 0)),
            pl.BlockSpec((1, tk, dqk), lambda i, s, qt, kt: (i, kt[s], 0)),
            pl.BlockSpec((1, dv, tk), lambda i, s, qt, kt: (i, 0, kt[s])),
            pl.BlockSpec((4, DIFF_DK), lambda i, s, qt, kt: (0, 0)),
            pl.BlockSpec((DIFF_DV, 1), lambda i, s, qt, kt: (0, 0)),
        ],
        out_specs=pl.BlockSpec((1, tq, dv), lambda i, s, qt, kt: (i, qt[s], 0)),
        scratch_shapes=[
            pltpu.VMEM((DIFF_HEADS, 2 * tq, DIFF_DV), BF16),
            pltpu.VMEM((DIFF_HEADS, 1, 2 * tq), F32),
            pltpu.VMEM((DIFF_HEADS, DIFF_DV + ONES_ROWS, 2 * tq), F32),
        ],
    )
    return pl.pallas_call(
        functools.partial(_flash_kernel, ratio=ratio, lam_init=lam_init),
        out_shape=jax.ShapeDtypeStruct((b, l, dv), BF16),
        grid_spec=grid_spec,
        compiler_params=_params(("parallel", "arbitrary")),
        name="flash_prompt",
    )(qi_tab, ki_tab, q, kb, vt, lam_vecs, subln.reshape(DIFF_DV, 1))


def _paged_kernel(pt_ref, q_ref, kn_ref, vn_ref, lam_ref, sub_ref, *rest, pages_per_step, lam_init):
    kt_refs = rest[:pages_per_step]
    v_refs = rest[pages_per_step:2 * pages_per_step]
    o_ref, qm_scr, m_scr, l_scr, acc_scr = rest[2 * pages_per_step:]
    step = pl.program_id(1)
    nrow, width = qm_scr.shape
    psize = kt_refs[0].shape[2]

    @pl.when(step == 0)
    def _():
        r = lax.broadcasted_iota(jnp.int32, (nrow, width), 0)
        lane_half = lax.broadcasted_iota(jnp.int32, (nrow, width), 1) >> 6
        qrow = jnp.broadcast_to(q_ref[0].astype(F32), (nrow, width))
        qm_scr[...] = jnp.where(r == lane_half, qrow, 0.0).astype(BF16)
        m_scr[...] = jnp.full(m_scr.shape, -jnp.inf, F32)
        l_scr[...] = jnp.zeros(l_scr.shape, F32)
        acc_scr[...] = jnp.zeros(acc_scr.shape, F32)

    qm = qm_scr[...]
    s = jnp.concatenate([_dot(qm, kr[0].astype(BF16)) for kr in kt_refs], axis=1)
    m_old = m_scr[...]
    m_new = jnp.maximum(m_old, jnp.max(s, axis=-1, keepdims=True))
    pf = jnp.exp2(s - m_new)
    alpha = jnp.exp2(m_old - m_new)
    vrows = DIFF_HEADS * psize
    row_head = lax.broadcasted_iota(jnp.int32, (nrow, vrows), 0) >> 1
    col_head = lax.broadcasted_iota(jnp.int32, (nrow, vrows), 1) >> (psize.bit_length() - 1)
    own = row_head == col_head
    pv = jnp.zeros(acc_scr.shape, F32)
    for i in range(pages_per_step):
        pi = pf[:, i * psize:(i + 1) * psize]
        spread = jnp.where(own, jnp.concatenate([pi] * DIFF_HEADS, axis=1), 0.0).astype(BF16)
        pv = pv + _dot(spread, v_refs[i][0].astype(BF16))
    l_scr[...] = alpha * l_scr[...] + jnp.sum(pf, axis=-1, keepdims=True)
    acc_scr[...] = alpha * acc_scr[...] + pv
    m_scr[...] = m_new

    @pl.when(step == pl.num_programs(1) - 1)
    def _():
        s_self = jnp.sum(qm.astype(F32) * kn_ref[0], axis=-1, keepdims=True)
        m_old = m_scr[...]
        m_fin = jnp.maximum(m_old, s_self)
        p_self = jnp.exp2(s_self - m_fin)
        a_fin = jnp.exp2(m_old - m_fin)
        l_fin = a_fin * l_scr[...] + p_self
        vn = vn_ref[0]
        vn_rows = jnp.concatenate(
            [vn[:, (r // 2) * DIFF_DV:(r // 2 + 1) * DIFF_DV] for r in range(2 * DIFF_HEADS)]
            + [jnp.zeros((nrow - 2 * DIFF_HEADS, DIFF_DV), F32)], axis=0)
        acc = (a_fin * acc_scr[...] + p_self * vn_rows) / l_fin
        lam = _lambda_full(lam_ref, lam_init)
        parts = []
        for hd in range(DIFF_HEADS):
            od = acc[2 * hd:2 * hd + 1] - lam * acc[2 * hd + 1:2 * hd + 2]
            ms = jnp.mean(od * od, axis=-1, keepdims=True)
            parts.append(od * lax.rsqrt(ms + EPS))
        on = (jnp.concatenate(parts, axis=1) * sub_ref[...]) * (1.0 - lam_init)
        o_ref[0] = on.astype(BF16)


def _paged_sample(q, k_new, v_new, cache_k, cache_v, page_table, lam_vecs, subln, lam_init):
    n, dqk = q.shape
    dv = v_new.shape[1]
    n_phys, psize = cache_k.shape[:2]
    n_pages = page_table.shape[1]
    pps = 8
    assert n_pages % pps == 0 and psize & (psize - 1) == 0
    nrow = 16
    ckt = jnp.transpose(cache_k, (0, 2, 3, 4, 1)).reshape(n_phys, dqk, psize)
    cvh = jnp.transpose(cache_v, (0, 2, 1, 3)).reshape(n_phys, DIFF_HEADS * psize, DIFF_DV)

    def page_spec(i, shape):
        return pl.BlockSpec((1,) + shape, lambda bi, s, pt: (pt[bi * n_pages + s * pps + i], 0, 0))

    row_spec = pl.BlockSpec((1, 1, dqk), lambda bi, s, pt: (bi, 0, 0))
    grid_spec = pltpu.PrefetchScalarGridSpec(
        num_scalar_prefetch=1,
        grid=(n, n_pages // pps),
        in_specs=[row_spec, row_spec, row_spec,
                  pl.BlockSpec((4, DIFF_DK), lambda bi, s, pt: (0, 0)),
                  pl.BlockSpec((1, dv), lambda bi, s, pt: (0, 0))]
        + [page_spec(i, (dqk, psize)) for i in range(pps)]
        + [page_spec(i, (DIFF_HEADS * psize, DIFF_DV)) for i in range(pps)],
        out_specs=pl.BlockSpec((1, 1, dv), lambda bi, s, pt: (bi, 0, 0)),
        scratch_shapes=[
            pltpu.VMEM((nrow, dqk), BF16),
            pltpu.VMEM((nrow, 1), F32),
            pltpu.VMEM((nrow, 1), F32),
            pltpu.VMEM((nrow, DIFF_DV), F32),
        ],
    )
    out = pl.pallas_call(
        functools.partial(_paged_kernel, pages_per_step=pps, lam_init=lam_init),
        out_shape=jax.ShapeDtypeStruct((n, 1, dv), BF16),
        grid_spec=grid_spec,
        compiler_params=_params(("parallel", "arbitrary")),
        name="paged_sample",
    )(page_table.reshape(-1), q.reshape(n, 1, dqk), k_new.reshape(n, 1, dqk), v_new.reshape(n, 1, dv),
      lam_vecs, jnp.tile(subln, DIFF_HEADS).reshape(1, dv), *([ckt] * pps), *([cvh] * pps))
    return out.reshape(n, dv)


def _post_prompt_kernel(x_ref, om_ref, qm_ref, mk_ref, mv_ref, wom_ref, wome_ref, xo_ref):
    o_mem = _mem_attend_shared(qm_ref[0], mk_ref[0], mv_ref[0].astype(BF16))
    xo_ref[0] = x_ref[0] + _dot(om_ref[0], wom_ref[...]) + _dot(o_mem.astype(BF16), wome_ref[...])


def _post_prompt(x, o_main, q_mem, mem_k, mem_v, wom, wome):
    b, l, d = x.shape
    tm = 512
    assert l % tm == 0
    dm = o_main.shape[2]
    m = mem_k.shape[2]
    return pl.pallas_call(
        _post_prompt_kernel,
        out_shape=jax.ShapeDtypeStruct((b, l, d), F32),
        grid=(b, l // tm),
        in_specs=[
            pl.BlockSpec((1, tm, d), lambda i, j: (i, j, 0)),
            pl.BlockSpec((1, tm, dm), lambda i, j: (i, j, 0)),
            pl.BlockSpec((1, tm, MEM_W), lambda i, j: (i, j, 0)),
            pl.BlockSpec((1, MEM_W, m), lambda i, j: (i, 0, 0)),
            pl.BlockSpec((1, MEM_W, m), lambda i, j: (i, 0, 0)),
            _const_spec((dm, d)),
            _const_spec((MEM_W, d)),
        ],
        out_specs=pl.BlockSpec((1, tm, d), lambda i, j: (i, j, 0)),
        compiler_params=_params(("parallel", "parallel")),
        name="post_prompt",
    )(x, o_main, q_mem, mem_k, mem_v, wom, wome)


def _post_sample_kernel(x_ref, om_ref, qm_ref, mk_ref, mv_ref, wom_ref, wome_ref, xo_ref, om_scr):
    _mem_attend_rows(qm_ref[...].astype(F32), mk_ref, mv_ref, om_scr)
    xo_ref[...] = (x_ref[...] + _dot(om_ref[...], wom_ref[...])
                   + _dot(om_scr[...].astype(BF16), wome_ref[...]))


def _post_sample(x, o_main, q_mem, mem_k, mem_v, wom, wome):
    n, d = x.shape
    rows = 2 * SUBLANES
    assert n % rows == 0
    dm = o_main.shape[1]
    m = mem_k.shape[2]
    return pl.pallas_call(
        _post_sample_kernel,
        out_shape=jax.ShapeDtypeStruct((n, d), F32),
        grid=(n // rows,),
        in_specs=[
            pl.BlockSpec((rows, d), lambda i: (i, 0)),
            pl.BlockSpec((rows, dm), lambda i: (i, 0)),
            pl.BlockSpec((rows, MEM_W), lambda i: (i, 0)),
            pl.BlockSpec((rows, MEM_W, m), lambda i: (i, 0, 0)),
            pl.BlockSpec((rows, MEM_W, m), lambda i: (i, 0, 0)),
            _const_spec((dm, d)),
            _const_spec((MEM_W, d)),
        ],
        out_specs=pl.BlockSpec((rows, d), lambda i: (i, 0)),
        scratch_shapes=[pltpu.VMEM((rows, MEM_W), F32)],
        compiler_params=_params(("parallel",)),
        name="post_sample",
    )(x, o_main, q_mem, mem_k, mem_v, wom, wome)


def _pad_heads_cols(w, heads, width, padded):
    lead = w.shape[:-1]
    w = w.reshape(lead + (heads, width))
    w = jnp.pad(w, [(0, 0)] * len(lead) + [(0, 0), (0, padded - width)])
    return w.reshape(lead + (heads * padded,))


def _prep_layer_a(w_in, w_gate2, b_gate2, onorm, w_out):
    dqk = GLA_HEADS * GLA_DK
    dv = GLA_HEADS * GLA_DV
    o = 0
    wq = w_in[:, o:o + dqk]; o += dqk
    wk = w_in[:, o:o + dqk]; o += dqk
    wv = w_in[:, o:o + dv]; o += dv
    wg = w_in[:, o:o + GLA_GATE_RANK]; o += GLA_GATE_RANK
    wr = w_in[:, o:o + dv]; o += dv
    wm = w_in[:, o:o + MEM_W]
    w_all = jnp.concatenate([
        _pad_heads_cols(wq, GLA_HEADS, GLA_DK, GLA_DK_PAD),
        _pad_heads_cols(wk, GLA_HEADS, GLA_DK, GLA_DK_PAD),
        _pad_heads_cols(wv, GLA_HEADS, GLA_DV, GLA_DV_PAD),
        jnp.pad(wg, ((0, 0), (0, LANES - GLA_GATE_RANK))),
        _pad_heads_cols(wr, GLA_HEADS, GLA_DV, GLA_DV_PAD),
        wm,
    ], axis=1).astype(BF16)
    wg2 = _pad_heads_cols(w_gate2, GLA_HEADS, GLA_DK, GLA_DK_PAD)
    wg2 = jnp.pad(wg2, ((0, LANES - GLA_GATE_RANK), (0, 0))).astype(BF16)
    bg2 = _pad_heads_cols(b_gate2.reshape(1, dqk), GLA_HEADS, GLA_DK, GLA_DK_PAD)
    gain = _pad_heads_cols(jnp.tile(onorm, GLA_HEADS).reshape(1, dv), GLA_HEADS, GLA_DV, GLA_DV_PAD)
    wo_main = _pad_heads_cols(w_out[:dv].T, GLA_HEADS, GLA_DV, GLA_DV_PAD).T.astype(BF16)
    return {
        "w_in": w_all, "w_gate2": wg2, "b_gate2": bg2, "onorm": gain,
        "w_out_main": wo_main, "w_out_mem": w_out[dv:].astype(BF16),
    }


def kernel(x_prompt, x_sample, mem_prompt, state_gla, cache_k, cache_v, cache_mem_k, cache_mem_v,
           page_table, ffn1_norm, ffn1_w_gate, ffn1_w_up, ffn1_w_down, mix_norm, a_w_in, a_w_gate2,
           a_b_gate2, a_onorm, b_w_in, b_lambda_q1, b_lambda_k1, b_lambda_q2, b_lambda_k2, b_subln,
           mem_norm, w_mem_kv, w_out, ffn2_norm, ffn2_w_gate, ffn2_w_up, ffn2_w_down, kv_norm, w_kv,
           final_norm):
    bp, seq, d = x_prompt.shape
    db, dseq, _ = x_sample.shape
    assert dseq == 1
    depth = ffn1_norm.shape[0]
    n_a = a_w_in.shape[0]
    mtok = mem_prompt.shape[1]
    dqk = DIFF_HEADS * 2 * DIFF_DK
    dv = DIFF_HEADS * DIFF_DV

    bf = lambda t: t.astype(BF16)
    f1 = (bf(ffn1_w_gate), bf(ffn1_w_up), bf(ffn1_w_down))
    f2 = (bf(ffn2_w_gate), bf(ffn2_w_up), bf(ffn2_w_down))
    w_kv_b = bf(w_kv)
    w_mem_b = bf(w_mem_kv)

    mem_k_p, mem_v_p = _memkv(mem_prompt, mem_norm, w_mem_b)
    cmk = jnp.transpose(cache_mem_k, (0, 1, 3, 4, 2)).reshape(depth, db, MEM_W, mtok)
    cmv = jnp.transpose(cache_mem_v, (0, 1, 3, 4, 2)).reshape(depth, db, MEM_W, mtok)

    xp = x_prompt
    xs = x_sample.reshape(db, d)
    states_p, states_s = [], []
    kv_p = kv_s = None
    for i in range(depth):
        last = i == depth - 1
        xp = _ffn(xp.reshape(bp * seq, d), ffn1_norm[i], f1[0][i], f1[1][i], f1[2][i]).reshape(bp, seq, d)
        xs = _ffn(xs, ffn1_norm[i], f1[0][i], f1[1][i], f1[2][i])
        if i < n_a:
            wa = _prep_layer_a(a_w_in[i], a_w_gate2[i], a_b_gate2[i], a_onorm[i], w_out[i])
            xp, st_p = _mixer_a_prompt(xp, mix_norm[i], wa, mem_k_p[i], mem_v_p[i])
            xs, st_s = _mixer_a_sample(xs, mix_norm[i], wa, state_gla[i], cmk[i], cmv[i])
            states_p.append(st_p)
            states_s.append(st_s)
        else:
            j = i - n_a
            lam_init = 0.8 - 0.6 * math.exp(-0.3 * i)
            lam_vecs = jnp.stack([b_lambda_q1[j], b_lambda_k1[j], b_lambda_q2[j], b_lambda_k2[j]])
            wq = bf(b_w_in[j][:, :dqk])
            wm = bf(b_w_in[j][:, dqk:])
            wom = bf(w_out[i][:dv])
            wome = bf(w_out[i][dv:])
            k_p, v_p, kb_p, vt_p = kv_p
            q_p, qm_p = _projb(xp.reshape(bp * seq, d), mix_norm[i], wq, wm)
            o_p = _flash_prompt(q_p.reshape(bp, seq, dqk), kb_p, vt_p, lam_vecs, b_subln[j], lam_init)
            xp = _post_prompt(xp, o_p, qm_p.reshape(bp, seq, MEM_W), mem_k_p[i], mem_v_p[i], wom, wome)
            k_s, v_s = kv_s
            q_s, qm_s = _projb(xs, mix_norm[i], wq, wm)
            o_s = _paged_sample(q_s, k_s.reshape(db, dqk), v_s.reshape(db, dv), cache_k, cache_v,
                                page_table, lam_vecs, b_subln[j], lam_init)
            xs = _post_sample(xs, o_s, qm_s, cmk[i], cmv[i], wom, wome)
        fg = final_norm if last else None
        xp = _ffn(xp.reshape(bp * seq, d), ffn2_norm[i], f2[0][i], f2[1][i], f2[2][i], fg).reshape(bp, seq, d)
        xs = _ffn(xs, ffn2_norm[i], f2[0][i], f2[1][i], f2[2][i], fg)
        if i == n_a - 1:
            kv_p = _kvproj(xp, kv_norm, w_kv_b, True)
            kv_s = _kvproj(xs.reshape(1, db, d), kv_norm, w_kv_b, False)

    y_prompt = xp
    y_sample = xs.reshape(db, 1, d)
    kt_p, vh_p = kv_p[0], kv_p[1]
    k_s, v_s = kv_s

    def mem_rows(t):
        return jnp.transpose(t.reshape(depth, bp, MEM_HEADS, MEM_DH, mtok), (0, 1, 4, 2, 3))

    return (
        y_prompt,
        y_sample,
        jnp.stack(states_p),
        jnp.stack(states_s),
        jnp.transpose(kt_p.reshape(bp, DIFF_HEADS, 2, DIFF_DK, seq), (0, 4, 1, 2, 3)),
        jnp.transpose(vh_p, (0, 2, 1, 3)),
        k_s.reshape(db, 1, DIFF_HEADS, 2, DIFF_DK),
        v_s.reshape(db, 1, DIFF_HEADS, DIFF_DV),
        mem_rows(mem_k_p),
        mem_rows(mem_v_p),
    )
```

```python
import functools
import math

import jax
import jax.numpy as jnp
from jax import lax
from jax.experimental import pallas as pl
from jax.experimental.pallas import tpu as pltpu

F32 = jnp.float32
BF16 = jnp.bfloat16

EPS = 1e-6
GLA_HEADS = 4
GLA_DK = 96
GLA_DV = 192
GLA_GATE_RANK = 16
GLA_GATE_NORM = 16.0
DIFF_HEADS = 6
DIFF_DK = 64
DIFF_DV = 2 * DIFF_DK
MEM_HEADS = 4
MEM_DH = 64
MEM_W = MEM_HEADS * MEM_DH
LOG2E = math.log2(math.e)

LANES = 128
SUBLANES = 8
VMEM_LIMIT_BYTES = 56 * 1024 * 1024

GLA_DK_PAD = 128
GLA_DV_PAD = 256
GLA_CHUNK = 64

A_Q0 = 0
A_K0 = A_Q0 + GLA_HEADS * GLA_DK_PAD
A_V0 = A_K0 + GLA_HEADS * GLA_DK_PAD
A_G0 = A_V0 + GLA_HEADS * GLA_DV_PAD
A_R0 = A_G0 + LANES
A_M0 = A_R0 + GLA_HEADS * GLA_DV_PAD
A_W = A_M0 + MEM_W


def _dot(a, b):
    return jnp.dot(a, b, preferred_element_type=F32)


def _dot_nt(a, b):
    return lax.dot_general(a, b, (((1,), (1,)), ((), ())), preferred_element_type=F32)


def _rms(x, g):
    ms = jnp.mean(x * x, axis=-1, keepdims=True)
    return x * lax.rsqrt(ms + EPS) * g


def _const_spec(shape):
    nd = len(shape)
    return pl.BlockSpec(shape, lambda *_: (0,) * nd, pipeline_mode=pl.Buffered(1))


def _params(sem):
    return pltpu.CompilerParams(dimension_semantics=sem, vmem_limit_bytes=VMEM_LIMIT_BYTES)


FF_CHUNK = 256


def _layer_spec(shape, layer):
    nd = len(shape)
    return pl.BlockSpec((1,) + shape, lambda *_: (layer,) + (0,) * nd, pipeline_mode=pl.Buffered(1))


def _swiglu_half_step(x, gain, wg_ref, wu_ref, wd_ref):
    h = _rms(x, gain).astype(BF16)
    acc = jnp.zeros(x.shape, F32)
    for c in range(wg_ref.shape[2] // FF_CHUNK):
        sl = slice(c * FF_CHUNK, (c + 1) * FF_CHUNK)
        g = _dot(h, wg_ref[0, :, sl])
        u = _dot(h, wu_ref[0, :, sl])
        a = (g * jax.nn.sigmoid(g) * u).astype(BF16)
        acc = acc + _dot(a, wd_ref[0, sl, :])
    return x + 0.5 * acc


def _ffn_kernel(*refs, pre, post):
    refs = list(refs)
    x_ref = refs.pop(0)
    x = x_ref[0]
    if pre == "mix_out":
        om_ref, qm_ref, mk_ref, mv_ref, wo_ref = refs[:5]
        del refs[:5]
        dm = om_ref.shape[2]
        o_mem = _mem_attend_shared(qm_ref[0], mk_ref[0], mv_ref[0].astype(BF16))
        x = x + _dot(om_ref[0], wo_ref[0, :dm, :]) + _dot(o_mem.astype(BF16), wo_ref[0, dm:, :])
    g_ref, wg_ref, wu_ref, wd_ref = refs[:4]
    del refs[:4]
    y = _swiglu_half_step(x, g_ref[0], wg_ref, wu_ref, wd_ref)
    if post == "final_norm":
        fg_ref, o_ref = refs
        o_ref[0] = _rms(y, fg_ref[...])
        return
    if post is None:
        (o_ref,) = refs
        o_ref[0] = y
        return
    pg_ref, pw_ref, o_ref = refs[:3]
    o_ref[0] = y
    h = _rms(y, pg_ref[...]).astype(BF16)
    if post == "kv":
        _store_prompt_kv(h, pw_ref, *refs[3:])
    else:
        q_ref, qm_out_ref = refs[3:]
        dqk = q_ref.shape[2]
        q_ref[0] = (_dot(h, pw_ref[0, :, :dqk]) * (DIFF_DK ** -0.5 * LOG2E)).astype(BF16)
        qm_out_ref[0] = _dot(h, pw_ref[0, :, dqk:]).astype(BF16)


def _ffn(x, gains, wg, wu, wd, layer, pre=None, post=None, pre_args=(), post_args=()):
    b, l, d = x.shape
    d_ff = wg.shape[2]
    tm = min(l, 512)
    assert l % tm == 0 and d_ff % FF_CHUNK == 0
    depth = gains.shape[0]
    row = lambda width: pl.BlockSpec((1, tm, width), lambda i, j: (i, j, 0))
    in_specs = [row(d)]
    args = [x]
    if pre == "mix_out":
        o_main, q_mem, mem_kt, mem_vt, w_out_b = pre_args
        m = mem_kt.shape[2]
        per_seq = pl.BlockSpec((1, MEM_W, m), lambda i, j: (i, 0, 0))
        in_specs += [row(o_main.shape[2]), row(MEM_W), per_seq, per_seq, _layer_spec(w_out_b.shape[1:], layer)]
        args += [o_main, q_mem, mem_kt, mem_vt, w_out_b]
    in_specs += [_layer_spec((1, d), layer), _layer_spec((d, d_ff), layer),
                 _layer_spec((d, d_ff), layer), _layer_spec((d_ff, d), layer)]
    args += [gains.reshape(depth, 1, d), wg, wu, wd]
    out_shape = [jax.ShapeDtypeStruct((b, l, d), F32)]
    out_specs = [row(d)]
    if post == "final_norm":
        (final_gain,) = post_args
        in_specs.append(_const_spec((1, d)))
        args.append(final_gain.reshape(1, d))
    elif post == "kv":
        kv_gain, w_kv_b = post_args
        in_specs += [_const_spec((1, d)), _const_spec(w_kv_b.shape)]
        args += [kv_gain.reshape(1, d), w_kv_b]
        kv_shapes, kv_specs = _prompt_kv_outputs(b, l, tm)
        out_shape += kv_shapes
        out_specs += kv_specs
    elif post == "q":
        q_gain, w_in_b, q_layer = post_args
        dqk = DIFF_HEADS * 2 * DIFF_DK
        in_specs += [_const_spec((1, d)), _layer_spec(w_in_b.shape[1:], q_layer)]
        args += [q_gain.reshape(1, d), w_in_b]
        out_shape += [jax.ShapeDtypeStruct((b, l, dqk), BF16), jax.ShapeDtypeStruct((b, l, MEM_W), BF16)]
        out_specs += [row(dqk), row(MEM_W)]
    out = pl.pallas_call(
        functools.partial(_ffn_kernel, pre=pre, post=post),
        out_shape=tuple(out_shape),
        grid=(b, l // tm),
        in_specs=in_specs,
        out_specs=tuple(out_specs),
        compiler_params=_params(("parallel", "parallel")),
        name="ffn" + ("_" + pre if pre else "") + ("_" + post if post else ""),
    )(*args)
    return out[0] if len(out) == 1 else out


def _memkv_kernel(m_ref, g_ref, w_ref, kt_ref, vt_ref):
    h = _rms(m_ref[0], g_ref[0]).astype(BF16)
    kv = _dot(h, w_ref[0])
    kt_ref[0, 0] = kv[:, :MEM_W].T
    vt_ref[0, 0] = kv[:, MEM_W:].T


def _memkv(mem, gains, w):
    depth = w.shape[0]
    b, m, d = mem.shape
    out = jax.ShapeDtypeStruct((depth, b, MEM_W, m), F32)
    return pl.pallas_call(
        _memkv_kernel,
        out_shape=(out, out),
        grid=(depth, b),
        in_specs=[
            pl.BlockSpec((1, m, d), lambda i, j: (j, 0, 0)),
            pl.BlockSpec((1, 1, d), lambda i, j: (i, 0, 0)),
            pl.BlockSpec((1, d, 2 * MEM_W), lambda i, j: (i, 0, 0)),
        ],
        out_specs=(
            pl.BlockSpec((1, 1, MEM_W, m), lambda i, j: (i, j, 0, 0)),
            pl.BlockSpec((1, 1, MEM_W, m), lambda i, j: (i, j, 0, 0)),
        ),
        compiler_params=_params(("arbitrary", "arbitrary")),
        name="memkv",
    )(mem, gains.reshape(depth, 1, d), w)


def _log_sigmoid(z):
    return jnp.minimum(z, 0.0) - jnp.log1p(jnp.exp(-jnp.abs(z)))


def _gla_recurrence_inputs(h, win_ref, wg2_ref, bg2_ref):
    q = _dot(h, win_ref[:, A_Q0:A_K0])
    k = _dot(h, win_ref[:, A_K0:A_V0])
    v = _dot(h, win_ref[:, A_V0:A_G0])
    g_lr = _dot(h, win_ref[:, A_G0:A_R0])
    z = _dot(g_lr.astype(BF16), wg2_ref[...]) + bg2_ref[...]
    logg = _log_sigmoid(z) / GLA_GATE_NORM
    return q, k, v, logg


def _gla_inputs(h, win_ref, wg2_ref, bg2_ref):
    q, k, v, logg = _gla_recurrence_inputs(h, win_ref, wg2_ref, bg2_ref)
    r = _dot(h, win_ref[:, A_R0:A_M0])
    q_mem = _dot(h, win_ref[:, A_M0:A_W])
    return q, k, v, logg, r, q_mem


def _gla_output_gate(o, r, gain):
    parts = []
    for h in range(GLA_HEADS):
        sl = slice(h * GLA_DV_PAD, (h + 1) * GLA_DV_PAD)
        oh = o[:, sl]
        ms = jnp.sum(oh * oh, axis=-1, keepdims=True) * (1.0 / GLA_DV)
        parts.append(oh * lax.rsqrt(ms + EPS) * gain[:, sl])
    on = jnp.concatenate(parts, axis=1)
    return on * (r * jax.nn.sigmoid(r))


def _mem_attend_shared(qm, mkt, mvt):
    row_head = lax.broadcasted_iota(jnp.int32, mkt.shape, 0) >> 6
    out_head = lax.broadcasted_iota(jnp.int32, (qm.shape[0], MEM_W), 1) >> 6
    scores = [_dot(qm, jnp.where(row_head == h, mkt, 0.0).astype(BF16)) * (MEM_DH ** -0.5)
              for h in range(MEM_HEADS)]
    o = jnp.zeros((qm.shape[0], MEM_W), F32)
    for h in range(MEM_HEADS):
        e = jnp.exp(scores[h] - jnp.max(scores[h], axis=-1, keepdims=True))
        l = jnp.sum(e, axis=-1, keepdims=True)
        oh = _dot_nt(e.astype(BF16), mvt) / l
        o = o + jnp.where(out_head == h, oh, 0.0)
    return o


def _mem_attend_rows(qmem, mkt_ref, mvt_ref, om_scr):
    rows = qmem.shape[0]
    r8 = lax.broadcasted_iota(jnp.int32, (SUBLANES, MEM_W), 0)
    lh = lax.broadcasted_iota(jnp.int32, (SUBLANES, MEM_W), 1) >> 6
    for j in range(rows):
        qrow = jnp.broadcast_to(qmem[j:j + 1], (SUBLANES, MEM_W))
        qm = jnp.where(r8 == lh, qrow, 0.0).astype(BF16)
        s = _dot(qm, mkt_ref[j].astype(BF16)) * (MEM_DH ** -0.5)
        e = jnp.exp(s - jnp.max(s, axis=-1, keepdims=True))
        l = jnp.sum(e, axis=-1, keepdims=True)
        oh = _dot_nt(e.astype(BF16), mvt_ref[j].astype(BF16)) / l
        om_scr[j:j + 1, :] = jnp.sum(jnp.where(r8 == lh, oh, 0.0), axis=0, keepdims=True)


def _mixer_a_prompt_kernel(x_ref, g_ref, win_ref, wg2_ref, bg2_ref, on_ref, mk_ref, mv_ref,
                           wom_ref, wome_ref, xo_ref, st_ref, s_scr):
    tl = x_ref.shape[1]
    li = pl.program_id(1)

    @pl.when(li == 0)
    def _():
        s_scr[...] = jnp.zeros(s_scr.shape, F32)

    x = x_ref[0]
    h = _rms(x, g_ref[...]).astype(BF16)
    q, k, v, logg = _gla_recurrence_inputs(h, win_ref, wg2_ref, bg2_ref)
    r = _dot(h, win_ref[:, A_R0:A_M0])
    q_mem = _dot(h, win_ref[:, A_M0:A_W])

    row = lax.broadcasted_iota(jnp.int32, (tl, tl), 0)
    col = lax.broadcasted_iota(jnp.int32, (tl, tl), 1)
    tril = ((row >> 6) == (col >> 6)) & (col <= row)
    trilb = jnp.where(tril, 1.0, 0.0).astype(BF16)
    hi = logg.astype(BF16)
    lo = (logg - hi.astype(F32)).astype(BF16)
    b = _dot(trilb, hi) + _dot(trilb, lo)

    o_mem = _mem_attend_shared(q_mem.astype(BF16), mk_ref[0], mv_ref[0].astype(BF16))

    lane_chunk = lax.broadcasted_iota(jnp.int32, (GLA_DK_PAD, tl), 1) >> 6
    scale = GLA_DK ** -0.5
    nchunk = tl // GLA_CHUNK
    qdbs, intras, decs, incs = [], [], [], []
    for hd in range(GLA_HEADS):
        ks = slice(hd * GLA_DK_PAD, (hd + 1) * GLA_DK_PAD)
        vs = slice(hd * GLA_DV_PAD, (hd + 1) * GLA_DV_PAD)
        bh = b[:, ks]
        qdb = ((q[:, ks] * scale) * jnp.exp(bh)).astype(BF16)
        ki = k[:, ks] * jnp.exp(-bh)
        att = jnp.where(tril, _dot_nt(qdb, ki.astype(BF16)), 0.0)
        vh = v[:, vs].astype(BF16)
        intras.append(_dot(att.astype(BF16), vh))
        ki_t = ki.T
        b_t = bh.T
        dec_h, inc_h = [], []
        for c in range(nchunk):
            last = c * GLA_CHUNK + GLA_CHUNK - 1
            dec = jnp.exp(b_t[:, last:last + 1])
            kend_t = jnp.where(lane_chunk == c, ki_t * dec, 0.0).astype(BF16)
            dec_h.append(dec)
            inc_h.append(_dot(kend_t, vh))
        qdbs.append(qdb)
        decs.append(dec_h)
        incs.append(inc_h)

    o_parts = []
    for hd in range(GLA_HEADS):
        state = s_scr[hd]
        starts = []
        for c in range(nchunk):
            starts.append(state.astype(BF16))
            state = decs[hd][c] * state + incs[hd][c]
        s_scr[hd] = state
        inter = [_dot(qdbs[hd][c * GLA_CHUNK:(c + 1) * GLA_CHUNK], starts[c]) for c in range(nchunk)]
        o_parts.append(intras[hd] + jnp.concatenate(inter, axis=0))
    o = jnp.concatenate(o_parts, axis=1)

    o_main = _gla_output_gate(o, r, on_ref[...])
    xo_ref[0] = x + _dot(o_main.astype(BF16), wom_ref[...]) + _dot(o_mem.astype(BF16), wome_ref[...])

    @pl.when(li == pl.num_programs(1) - 1)
    def _():
        for hd in range(GLA_HEADS):
            st_ref[0, hd] = s_scr[hd][:GLA_DK, :GLA_DV]


def _mixer_a_prompt(x, gain, wa, mem_k, mem_v):
    b, l, d = x.shape
    tl = min(l, 256)
    assert l % tl == 0 and tl % GLA_CHUNK == 0
    m = mem_k.shape[2]
    return pl.pallas_call(
        _mixer_a_prompt_kernel,
        out_shape=(
            jax.ShapeDtypeStruct((b, l, d), F32),
            jax.ShapeDtypeStruct((b, GLA_HEADS, GLA_DK, GLA_DV), F32),
        ),
        grid=(b, l // tl),
        in_specs=[
            pl.BlockSpec((1, tl, d), lambda i, j: (i, j, 0)),
            _const_spec((1, d)),
            _const_spec((d, A_W)),
            _const_spec((LANES, GLA_HEADS * GLA_DK_PAD)),
            _const_spec((1, GLA_HEADS * GLA_DK_PAD)),
            _const_spec((1, GLA_HEADS * GLA_DV_PAD)),
            pl.BlockSpec((1, MEM_W, m), lambda i, j: (i, 0, 0)),
            pl.BlockSpec((1, MEM_W, m), lambda i, j: (i, 0, 0)),
            _const_spec((GLA_HEADS * GLA_DV_PAD, d)),
            _const_spec((MEM_W, d)),
        ],
        out_specs=(
            pl.BlockSpec((1, tl, d), lambda i, j: (i, j, 0)),
            pl.BlockSpec((1, GLA_HEADS, GLA_DK, GLA_DV), lambda i, j: (i, 0, 0, 0)),
        ),
        scratch_shapes=[pltpu.VMEM((GLA_HEADS, GLA_DK_PAD, GLA_DV_PAD), F32)],
        compiler_params=_params(("parallel", "arbitrary")),
        name="mixer_a_prompt",
    )(x, gain.reshape(1, d), wa["w_in"], wa["w_gate2"], wa["b_gate2"], wa["onorm"], mem_k, mem_v,
      wa["w_out_main"], wa["w_out_mem"])


def _columns(t):
    pad = jnp.zeros((LANES - t.shape[0], t.shape[1]), F32)
    return jnp.concatenate([t, pad], axis=0).T


def _mixer_a_sample_kernel(x_ref, g_ref, win_ref, wg2_ref, bg2_ref, on_ref, st_ref, mk_ref, mv_ref,
                           wom_ref, wome_ref, xo_ref, sto_ref, o_scr, om_scr):
    rows = x_ref.shape[0]
    x = x_ref[...]
    h = _rms(x, g_ref[...]).astype(BF16)
    q, k, v, logg, r, q_mem = _gla_inputs(h, win_ref, wg2_ref, bg2_ref)
    a_t = _columns(jnp.exp(logg))
    k_t = _columns(k)
    q_t = _columns(q * (GLA_DK ** -0.5))
    o_scr[...] = jnp.zeros(o_scr.shape, F32)
    for j in range(rows):
        for hd in range(GLA_HEADS):
            rs = slice(hd * GLA_DK_PAD, hd * GLA_DK_PAD + GLA_DK)
            vs = slice(hd * GLA_DV_PAD, hd * GLA_DV_PAD + GLA_DV)
            s_new = a_t[rs, j:j + 1] * st_ref[j, hd] + k_t[rs, j:j + 1] * v[j:j + 1, vs]
            sto_ref[j, hd] = s_new
            o_scr[j:j + 1, vs] = jnp.sum(q_t[rs, j:j + 1] * s_new, axis=0, keepdims=True)
    o_main = _gla_output_gate(o_scr[...], r, on_ref[...])
    _mem_attend_rows(q_mem, mk_ref, mv_ref, om_scr)
    xo_ref[...] = (x + _dot(o_main.astype(BF16), wom_ref[...])
                   + _dot(om_scr[...].astype(BF16), wome_ref[...]))


def _mixer_a_sample(x, gain, wa, state, mem_k, mem_v):
    n, d = x.shape
    rows = SUBLANES
    assert n % rows == 0
    m = mem_k.shape[2]
    return pl.pallas_call(
        _mixer_a_sample_kernel,
        out_shape=(
            jax.ShapeDtypeStruct((n, d), F32),
            jax.ShapeDtypeStruct(state.shape, F32),
        ),
        grid=(n // rows,),
        in_specs=[
            pl.BlockSpec((rows, d), lambda i: (i, 0)),
            _const_spec((1, d)),
            _const_spec((d, A_W)),
            _const_spec((LANES, GLA_HEADS * GLA_DK_PAD)),
            _const_spec((1, GLA_HEADS * GLA_DK_PAD)),
            _const_spec((1, GLA_HEADS * GLA_DV_PAD)),
            pl.BlockSpec((rows, GLA_HEADS, GLA_DK, GLA_DV), lambda i: (i, 0, 0, 0)),
            pl.BlockSpec((rows, MEM_W, m), lambda i: (i, 0, 0)),
            pl.BlockSpec((rows, MEM_W, m), lambda i: (i, 0, 0)),
            _const_spec((GLA_HEADS * GLA_DV_PAD, d)),
            _const_spec((MEM_W, d)),
        ],
        out_specs=(
            pl.BlockSpec((rows, d), lambda i: (i, 0)),
            pl.BlockSpec((rows, GLA_HEADS, GLA_DK, GLA_DV), lambda i: (i, 0, 0, 0)),
        ),
        scratch_shapes=[
            pltpu.VMEM((rows, GLA_HEADS * GLA_DV_PAD), F32),
            pltpu.VMEM((rows, MEM_W), F32),
        ],
        compiler_params=_params(("parallel",)),
        name="mixer_a_sample",
    )(x, gain.reshape(1, d), wa["w_in"], wa["w_gate2"], wa["b_gate2"], wa["onorm"], state, mem_k, mem_v,
      wa["w_out_main"], wa["w_out_mem"])


def _store_prompt_kv(h, w_ref, kt_ref, vh_ref, kb_ref, vt_ref):
    dqk = DIFF_HEADS * 2 * DIFF_DK
    k = _dot(h, w_ref[:, :dqk])
    v = _dot(h, w_ref[:, dqk:])
    kt_ref[0] = k.T
    for hd in range(DIFF_HEADS):
        vh_ref[0, hd] = v[:, hd * DIFF_DV:(hd + 1) * DIFF_DV]
    kb_ref[0] = k.astype(BF16)
    vt_ref[0] = v.T.astype(BF16)


def _prompt_kv_outputs(b, l, tm):
    dqk = DIFF_HEADS * 2 * DIFF_DK
    dv = DIFF_HEADS * DIFF_DV
    shapes = [jax.ShapeDtypeStruct((b, dqk, l), F32),
              jax.ShapeDtypeStruct((b, DIFF_HEADS, l, DIFF_DV), F32),
              jax.ShapeDtypeStruct((b, l, dqk), BF16),
              jax.ShapeDtypeStruct((b, dv, l), BF16)]
    specs = [pl.BlockSpec((1, dqk, tm), lambda i, j: (i, 0, j)),
             pl.BlockSpec((1, DIFF_HEADS, tm, DIFF_DV), lambda i, j: (i, 0, j, 0)),
             pl.BlockSpec((1, tm, dqk), lambda i, j: (i, j, 0)),
             pl.BlockSpec((1, dv, tm), lambda i, j: (i, 0, j))]
    return shapes, specs


def _kvproj_sample_kernel(x_ref, g_ref, w_ref, k_ref, v_ref):
    h = _rms(x_ref[...], g_ref[...]).astype(BF16)
    dqk = k_ref.shape[1]
    k_ref[...] = _dot(h, w_ref[:, :dqk])
    v_ref[...] = _dot(h, w_ref[:, dqk:])


def _kvproj_sample(x, gain, w):
    n, d = x.shape
    dqk = DIFF_HEADS * 2 * DIFF_DK
    dv = DIFF_HEADS * DIFF_DV
    return pl.pallas_call(
        _kvproj_sample_kernel,
        out_shape=(jax.ShapeDtypeStruct((n, dqk), F32), jax.ShapeDtypeStruct((n, dv), F32)),
        grid=(1,),
        in_specs=[pl.BlockSpec((n, d), lambda i: (0, 0)), _const_spec((1, d)), _const_spec((d, dqk + dv))],
        out_specs=(pl.BlockSpec((n, dqk), lambda i: (0, 0)), pl.BlockSpec((n, dv), lambda i: (0, 0))),
        compiler_params=_params(("arbitrary",)),
        name="kvproj_sample",
    )(x, gain.reshape(1, d), w)


def _projb_sample_kernel(x_ref, g_ref, w_ref, q_ref, qm_ref):
    h = _rms(x_ref[...], g_ref[...]).astype(BF16)
    dqk = q_ref.shape[1]
    q_ref[...] = (_dot(h, w_ref[0, :, :dqk]) * (DIFF_DK ** -0.5 * LOG2E)).astype(BF16)
    qm_ref[...] = _dot(h, w_ref[0, :, dqk:]).astype(BF16)


def _projb_sample(x, gain, w_in_b, layer):
    n, d = x.shape
    dqk = DIFF_HEADS * 2 * DIFF_DK
    return pl.pallas_call(
        _projb_sample_kernel,
        out_shape=(jax.ShapeDtypeStruct((n, dqk), BF16), jax.ShapeDtypeStruct((n, MEM_W), BF16)),
        grid=(1,),
        in_specs=[pl.BlockSpec((n, d), lambda i: (0, 0)), _const_spec((1, d)),
                  _layer_spec(w_in_b.shape[1:], layer)],
        out_specs=(pl.BlockSpec((n, dqk), lambda i: (0, 0)), pl.BlockSpec((n, MEM_W), lambda i: (0, 0))),
        compiler_params=_params(("arbitrary",)),
        name="projb_sample",
    )(x, gain.reshape(1, d), w_in_b)


def _lambda_full(lam_ref, lam_init):
    lv = lam_ref[...]
    s1 = jnp.sum(lv[0:1] * lv[1:2], axis=-1, keepdims=True)
    s2 = jnp.sum(lv[2:3] * lv[3:4], axis=-1, keepdims=True)
    return jnp.exp(s1) - jnp.exp(s2) + lam_init


ONES_ROWS = 16


def _flash_kernel(qi_tab, ki_tab, q_ref, k_ref, vt_ref, lam_ref, sub_ref, o_ref,
                  qm_scr, m_scr, acc_scr, *, ratio, lam_init):
    step = pl.program_id(1)
    qi = qi_tab[step]
    ki = ki_tab[step]
    tq = q_ref.shape[1]
    tk = k_ref.shape[1]

    @pl.when(ki == 0)
    def _():
        for hd in range(DIFF_HEADS):
            qh = q_ref[0, :, hd * DIFF_DV:(hd + 1) * DIFF_DV].astype(F32)
            first = lax.broadcasted_iota(jnp.int32, qh.shape, 1) < DIFF_DK
            qm_scr[hd, :tq] = jnp.where(first, qh, 0.0).astype(BF16)
            qm_scr[hd, tq:] = jnp.where(first, 0.0, qh).astype(BF16)
        m_scr[...] = jnp.full(m_scr.shape, -jnp.inf, F32)
        acc_scr[...] = jnp.zeros(acc_scr.shape, F32)

    def block(masked):
        if masked:
            col = lax.broadcasted_iota(jnp.int32, (tk, 2 * tq), 1)
            kpos = ki * tk + lax.broadcasted_iota(jnp.int32, (tk, 2 * tq), 0)
            keep = kpos <= qi * tq + jnp.where(col >= tq, col - tq, col)
        ones = jnp.ones((ONES_ROWS, tk), BF16)

        def scores(hd):
            st = _dot_nt(k_ref[0, :, hd * DIFF_DV:(hd + 1) * DIFF_DV], qm_scr[hd])
            return jnp.where(keep, st, -jnp.inf) if masked else st

        st_next = scores(0)
        for hd in range(DIFF_HEADS):
            hs = slice(hd * DIFF_DV, (hd + 1) * DIFF_DV)
            st = st_next
            if hd + 1 < DIFF_HEADS:
                st_next = scores(hd + 1)
            m_old = m_scr[hd]
            m_new = jnp.maximum(m_old, jnp.max(st, axis=0, keepdims=True))
            p = jnp.exp2(st - m_new).astype(BF16)
            alpha = jnp.exp2(m_old - m_new)
            v_aug = jnp.concatenate([vt_ref[0, hs, :], ones], axis=0)
            acc_scr[hd] = alpha * acc_scr[hd] + _dot(v_aug, p)
            m_scr[hd] = m_new

    diag = ki >= qi * ratio

    @pl.when(diag)
    def _():
        block(True)

    @pl.when(jnp.logical_not(diag))
    def _():
        block(False)

    @pl.when(ki == (qi + 1) * ratio - 1)
    def _():
        lam = _lambda_full(lam_ref, lam_init)
        for hd in range(DIFF_HEADS):
            acc = acc_scr[hd]
            o = acc[:DIFF_DV] / acc[DIFF_DV:DIFF_DV + 1]
            od = o[:, :tq] - lam * o[:, tq:]
            ms = jnp.mean(od * od, axis=0, keepdims=True)
            on = (od * lax.rsqrt(ms + EPS) * sub_ref[...]) * (1.0 - lam_init)
            o_ref[0, :, hd * DIFF_DV:(hd + 1) * DIFF_DV] = on.T.astype(BF16)


def _flash_prompt(q, kb, vt, lam_vecs, subln, lam_init):
    b, l, dqk = q.shape
    dv = vt.shape[1]
    tq = min(l, 512)
    tk = min(l, 512)
    assert l % tq == 0 and tq % tk == 0
    ratio = tq // tk
    pairs = [(i, j) for i in range(l // tq) for j in range((i + 1) * ratio)]
    qi_tab = jnp.asarray([p[0] for p in pairs], jnp.int32)
    ki_tab = jnp.asarray([p[1] for p in pairs], jnp.int32)
    grid_spec = pltpu.PrefetchScalarGridSpec(
        num_scalar_prefetch=2,
        grid=(b, len(pairs)),
        in_specs=[
            pl.BlockSpec((1, tq, dqk), lambda i, s, qt, kt: (i, qt[s], 0)),
            pl.BlockSpec((1, tk, dqk), lambda i, s, qt, kt: (i, kt[s], 0)),
            pl.BlockSpec((1, dv, tk), lambda i, s, qt, kt: (i, 0, kt[s])),
            pl.BlockSpec((4, DIFF_DK), lambda i, s, qt, kt: (0, 0)),
            pl.BlockSpec((DIFF_DV, 1), lambda i, s, qt, kt: (0, 0)),
        ],
        out_specs=pl.BlockSpec((1, tq, dv), lambda i, s, qt, kt: (i, qt[s], 0)),
        scratch_shapes=[
            pltpu.VMEM((DIFF_HEADS, 2 * tq, DIFF_DV), BF16),
            pltpu.VMEM((DIFF_HEADS, 1, 2 * tq), F32),
            pltpu.VMEM((DIFF_HEADS, DIFF_DV + ONES_ROWS, 2 * tq), F32),
        ],
    )
    return pl.pallas_call(
        functools.partial(_flash_kernel, ratio=ratio, lam_init=lam_init),
        out_shape=jax.ShapeDtypeStruct((b, l, dv), BF16),
        grid_spec=grid_spec,
        compiler_params=_params(("parallel", "arbitrary")),
        name="flash_prompt",
    )(qi_tab, ki_tab, q, kb, vt, lam_vecs, subln.reshape(DIFF_DV, 1))


def _paged_kernel(pt_ref, q_ref, kn_ref, vn_ref, lam_ref, sub_ref, *rest, pages_per_step, lam_init):
    kt_refs = rest[:pages_per_step]
    v_refs = rest[pages_per_step:2 * pages_per_step]
    o_ref, qm_scr, m_scr, l_scr, acc_scr = rest[2 * pages_per_step:]
    step = pl.program_id(1)
    nrow, width = qm_scr.shape
    psize = kt_refs[0].shape[2]

    @pl.when(step == 0)
    def _():
        r = lax.broadcasted_iota(jnp.int32, (nrow, width), 0)
        lane_half = lax.broadcasted_iota(jnp.int32, (nrow, width), 1) >> 6
        qrow = jnp.broadcast_to(q_ref[0].astype(F32), (nrow, width))
        qm_scr[...] = jnp.where(r == lane_half, qrow, 0.0).astype(BF16)
        m_scr[...] = jnp.full(m_scr.shape, -jnp.inf, F32)
        l_scr[...] = jnp.zeros(l_scr.shape, F32)
        acc_scr[...] = jnp.zeros(acc_scr.shape, F32)

    qm = qm_scr[...]
    s = jnp.concatenate([_dot(qm, kr[0].astype(BF16)) for kr in kt_refs], axis=1)
    m_old = m_scr[...]
    m_new = jnp.maximum(m_old, jnp.max(s, axis=-1, keepdims=True))
    pf = jnp.exp2(s - m_new)
    alpha = jnp.exp2(m_old - m_new)
    vrows = DIFF_HEADS * psize
    row_head = lax.broadcasted_iota(jnp.int32, (nrow, vrows), 0) >> 1
    col_head = lax.broadcasted_iota(jnp.int32, (nrow, vrows), 1) >> (psize.bit_length() - 1)
    own = row_head == col_head
    pv = jnp.zeros(acc_scr.shape, F32)
    for i in range(pages_per_step):
        pi = pf[:, i * psize:(i + 1) * psize]
        spread = jnp.where(own, jnp.concatenate([pi] * DIFF_HEADS, axis=1), 0.0).astype(BF16)
        pv = pv + _dot(spread, v_refs[i][0].astype(BF16))
    l_scr[...] = alpha * l_scr[...] + jnp.sum(pf, axis=-1, keepdims=True)
    acc_scr[...] = alpha * acc_scr[...] + pv
    m_scr[...] = m_new

    @pl.when(step == pl.num_programs(1) - 1)
    def _():
        s_self = jnp.sum(qm.astype(F32) * kn_ref[0], axis=-1, keepdims=True)
        m_old = m_scr[...]
        m_fin = jnp.maximum(m_old, s_self)
        p_self = jnp.exp2(s_self - m_fin)
        a_fin = jnp.exp2(m_old - m_fin)
        l_fin = a_fin * l_scr[...] + p_self
        vn = vn_ref[0]
        vn_rows = jnp.concatenate(
            [vn[:, (r // 2) * DIFF_DV:(r // 2 + 1) * DIFF_DV] for r in range(2 * DIFF_HEADS)]
            + [jnp.zeros((nrow - 2 * DIFF_HEADS, DIFF_DV), F32)], axis=0)
        acc = (a_fin * acc_scr[...] + p_self * vn_rows) / l_fin
        lam = _lambda_full(lam_ref, lam_init)
        parts = []
        for hd in range(DIFF_HEADS):
            od = acc[2 * hd:2 * hd + 1] - lam * acc[2 * hd + 1:2 * hd + 2]
            ms = jnp.mean(od * od, axis=-1, keepdims=True)
            parts.append(od * lax.rsqrt(ms + EPS))
        on = (jnp.concatenate(parts, axis=1) * sub_ref[...]) * (1.0 - lam_init)
        o_ref[0] = on.astype(BF16)


def _paged_sample(q, k_new, v_new, cache_k, cache_v, page_table, lam_vecs, subln, lam_init):
    n, dqk = q.shape
    dv = v_new.shape[1]
    n_phys, psize = cache_k.shape[:2]
    n_pages = page_table.shape[1]
    pps = 8
    assert n_pages % pps == 0 and psize & (psize - 1) == 0
    nrow = 16
    ckt = jnp.transpose(cache_k, (0, 2, 3, 4, 1)).reshape(n_phys, dqk, psize)
    cvh = jnp.transpose(cache_v, (0, 2, 1, 3)).reshape(n_phys, DIFF_HEADS * psize, DIFF_DV)

    def page_spec(i, shape):
        return pl.BlockSpec((1,) + shape, lambda bi, s, pt: (pt[bi * n_pages + s * pps + i], 0, 0))

    row_spec = pl.BlockSpec((1, 1, dqk), lambda bi, s, pt: (bi, 0, 0))
    grid_spec = pltpu.PrefetchScalarGridSpec(
        num_scalar_prefetch=1,
        grid=(n, n_pages // pps),
        in_specs=[row_spec, row_spec, row_spec,
                  pl.BlockSpec((4, DIFF_DK), lambda bi, s, pt: (0, 0)),
                  pl.BlockSpec((1, dv), lambda bi, s, pt: (0, 0))]
        + [page_spec(i, (dqk, psize)) for i in range(pps)]
        + [page_spec(i, (DIFF_HEADS * psize, DIFF_DV)) for i in range(pps)],
        out_specs=pl.BlockSpec((1, 1, dv), lambda bi, s, pt: (bi, 0, 0)),
        scratch_shapes=[
            pltpu.VMEM((nrow, dqk), BF16),
            pltpu.VMEM((nrow, 1), F32),
            pltpu.VMEM((nrow, 1), F32),
            pltpu.VMEM((nrow, DIFF_DV), F32),
        ],
    )
    out = pl.pallas_call(
        functools.partial(_paged_kernel, pages_per_step=pps, lam_init=lam_init),
        out_shape=jax.ShapeDtypeStruct((n, 1, dv), BF16),
        grid_spec=grid_spec,
        compiler_params=_params(("parallel", "arbitrary")),
        name="paged_sample",
    )(page_table.reshape(-1), q.reshape(n, 1, dqk), k_new.reshape(n, 1, dqk), v_new.reshape(n, 1, dv),
      lam_vecs, jnp.tile(subln, DIFF_HEADS).reshape(1, dv), *([ckt] * pps), *([cvh] * pps))
    return out.reshape(n, dv)


def _post_sample_kernel(x_ref, om_ref, qm_ref, mk_ref, mv_ref, wo_ref, xo_ref, om_scr):
    _mem_attend_rows(qm_ref[...].astype(F32), mk_ref, mv_ref, om_scr)
    dm = om_ref.shape[1]
    xo_ref[...] = (x_ref[...] + _dot(om_ref[...], wo_ref[0, :dm, :])
                   + _dot(om_scr[...].astype(BF16), wo_ref[0, dm:, :]))


def _post_sample(x, o_main, q_mem, mem_k, mem_v, w_out_b, layer):
    n, d = x.shape
    rows = 2 * SUBLANES
    assert n % rows == 0
    dm = o_main.shape[1]
    m = mem_k.shape[2]
    return pl.pallas_call(
        _post_sample_kernel,
        out_shape=jax.ShapeDtypeStruct((n, d), F32),
        grid=(n // rows,),
        in_specs=[
            pl.BlockSpec((rows, d), lambda i: (i, 0)),
            pl.BlockSpec((rows, dm), lambda i: (i, 0)),
            pl.BlockSpec((rows, MEM_W), lambda i: (i, 0)),
            pl.BlockSpec((rows, MEM_W, m), lambda i: (i, 0, 0)),
            pl.BlockSpec((rows, MEM_W, m), lambda i: (i, 0, 0)),
            _layer_spec(w_out_b.shape[1:], layer),
        ],
        out_specs=pl.BlockSpec((rows, d), lambda i: (i, 0)),
        scratch_shapes=[pltpu.VMEM((rows, MEM_W), F32)],
        compiler_params=_params(("parallel",)),
        name="post_sample",
    )(x, o_main, q_mem, mem_k, mem_v, w_out_b)


def _pad_heads_cols(w, heads, width, padded):
    lead = w.shape[:-1]
    w = w.reshape(lead + (heads, width))
    w = jnp.pad(w, [(0, 0)] * len(lead) + [(0, 0), (0, padded - width)])
    return w.reshape(lead + (heads * padded,))


def _prep_layer_a(w_in, w_gate2, b_gate2, onorm, w_out):
    dqk = GLA_HEADS * GLA_DK
    dv = GLA_HEADS * GLA_DV
    o = 0
    wq = w_in[:, o:o + dqk]; o += dqk
    wk = w_in[:, o:o + dqk]; o += dqk
    wv = w_in[:, o:o + dv]; o += dv
    wg = w_in[:, o:o + GLA_GATE_RANK]; o += GLA_GATE_RANK
    wr = w_in[:, o:o + dv]; o += dv
    wm = w_in[:, o:o + MEM_W]
    w_all = jnp.concatenate([
        _pad_heads_cols(wq, GLA_HEADS, GLA_DK, GLA_DK_PAD),
        _pad_heads_cols(wk, GLA_HEADS, GLA_DK, GLA_DK_PAD),
        _pad_heads_cols(wv, GLA_HEADS, GLA_DV, GLA_DV_PAD),
        jnp.pad(wg, ((0, 0), (0, LANES - GLA_GATE_RANK))),
        _pad_heads_cols(wr, GLA_HEADS, GLA_DV, GLA_DV_PAD),
        wm,
    ], axis=1).astype(BF16)
    wg2 = _pad_heads_cols(w_gate2, GLA_HEADS, GLA_DK, GLA_DK_PAD)
    wg2 = jnp.pad(wg2, ((0, LANES - GLA_GATE_RANK), (0, 0))).astype(BF16)
    bg2 = _pad_heads_cols(b_gate2.reshape(1, dqk), GLA_HEADS, GLA_DK, GLA_DK_PAD)
    gain = _pad_heads_cols(jnp.tile(onorm, GLA_HEADS).reshape(1, dv), GLA_HEADS, GLA_DV, GLA_DV_PAD)
    wo_main = _pad_heads_cols(w_out[:dv].T, GLA_HEADS, GLA_DV, GLA_DV_PAD).T.astype(BF16)
    return {
        "w_in": w_all, "w_gate2": wg2, "b_gate2": bg2, "onorm": gain,
        "w_out_main": wo_main, "w_out_mem": w_out[dv:].astype(BF16),
    }


def kernel(x_prompt, x_sample, mem_prompt, state_gla, cache_k, cache_v, cache_mem_k, cache_mem_v,
           page_table, ffn1_norm, ffn1_w_gate, ffn1_w_up, ffn1_w_down, mix_norm, a_w_in, a_w_gate2,
           a_b_gate2, a_onorm, b_w_in, b_lambda_q1, b_lambda_k1, b_lambda_q2, b_lambda_k2, b_subln,
           mem_norm, w_mem_kv, w_out, ffn2_norm, ffn2_w_gate, ffn2_w_up, ffn2_w_down, kv_norm, w_kv,
           final_norm):
    bp, seq, d = x_prompt.shape
    db, dseq, _ = x_sample.shape
    assert dseq == 1
    depth = ffn1_norm.shape[0]
    n_a = a_w_in.shape[0]
    mtok = mem_prompt.shape[1]
    assert depth > n_a
    bf = lambda t: t.astype(BF16)
    f1 = (bf(ffn1_w_gate), bf(ffn1_w_up), bf(ffn1_w_down))
    f2 = (bf(ffn2_w_gate), bf(ffn2_w_up), bf(ffn2_w_down))
    w_kv_b = bf(w_kv)
    w_mem_b = bf(w_mem_kv)
    w_out_b = bf(w_out)
    b_w_in_b = bf(b_w_in)

    mem_k_p, mem_v_p = _memkv(mem_prompt, mem_norm, w_mem_b)
    cmk = jnp.transpose(cache_mem_k, (0, 1, 3, 4, 2)).reshape(depth, db, MEM_W, mtok)
    cmv = jnp.transpose(cache_mem_v, (0, 1, 3, 4, 2)).reshape(depth, db, MEM_W, mtok)

    xp = x_prompt
    xs = x_sample.reshape(1, db, d)
    states_p, states_s = [], []
    kv_p = kv_s = None
    for i in range(depth):
        last = i == depth - 1
        is_a = i < n_a
        j = i - n_a
        if is_a:
            xp = _ffn(xp, ffn1_norm, *f1, i)
        else:
            xp, q_p, qm_p = _ffn(xp, ffn1_norm, *f1, i, post="q", post_args=(mix_norm[i], b_w_in_b, j))
        xs = _ffn(xs, ffn1_norm, *f1, i)
        pre, pre_args = None, ()
        if is_a:
            wa = _prep_layer_a(a_w_in[i], a_w_gate2[i], a_b_gate2[i], a_onorm[i], w_out[i])
            xp, st_p = _mixer_a_prompt(xp, mix_norm[i], wa, mem_k_p[i], mem_v_p[i])
            xs2, st_s = _mixer_a_sample(xs[0], mix_norm[i], wa, state_gla[i], cmk[i], cmv[i])
            xs = xs2[None]
            states_p.append(st_p)
            states_s.append(st_s)
        else:
            lam_init = 0.8 - 0.6 * math.exp(-0.3 * i)
            lam_vecs = jnp.stack([b_lambda_q1[j], b_lambda_k1[j], b_lambda_q2[j], b_lambda_k2[j]])
            _, _, kb_p, vt_p = kv_p
            o_p = _flash_prompt(q_p, kb_p, vt_p, lam_vecs, b_subln[j], lam_init)
            pre, pre_args = "mix_out", (o_p, qm_p, mem_k_p[i], mem_v_p[i], w_out_b)
            k_s, v_s = kv_s
            q_s, qm_s = _projb_sample(xs[0], mix_norm[i], b_w_in_b, j)
            o_s = _paged_sample(q_s, k_s, v_s, cache_k, cache_v, page_table, lam_vecs, b_subln[j], lam_init)
            xs = _post_sample(xs[0], o_s, qm_s, cmk[i], cmv[i], w_out_b, i)[None]
        if last:
            xp = _ffn(xp, ffn2_norm, *f2, i, pre=pre, pre_args=pre_args, post="final_norm", post_args=(final_norm,))
            xs = _ffn(xs, ffn2_norm, *f2, i, post="final_norm", post_args=(final_norm,))
        elif i == n_a - 1:
            xp, *kv_p = _ffn(xp, ffn2_norm, *f2, i, pre=pre, pre_args=pre_args, post="kv",
                             post_args=(kv_norm, w_kv_b))
            xs = _ffn(xs, ffn2_norm, *f2, i)
            kv_s = _kvproj_sample(xs[0], kv_norm, w_kv_b)
        else:
            xp = _ffn(xp, ffn2_norm, *f2, i, pre=pre, pre_args=pre_args)
            xs = _ffn(xs, ffn2_norm, *f2, i)

    y_prompt = xp
    y_sample = xs.reshape(db, 1, d)
    kt_p, vh_p = kv_p[0], kv_p[1]
    k_s, v_s = kv_s

    def mem_rows(t):
        return jnp.transpose(t.reshape(depth, bp, MEM_HEADS, MEM_DH, mtok), (0, 1, 4, 2, 3))

    return (
        y_prompt,
        y_sample,
        jnp.stack(states_p),
        jnp.stack(states_s),
        jnp.transpose(kt_p.reshape(bp, DIFF_HEADS, 2, DIFF_DK, seq), (0, 4, 1, 2, 3)),
        jnp.transpose(vh_p, (0, 2, 1, 3)),
        k_s.reshape(db, 1, DIFF_HEADS, 2, DIFF_DK),
        v_s.reshape(db, 1, DIFF_HEADS, DIFF_DV),
        mem_rows(mem_k_p),
        mem_rows(mem_v_p),
    )
```

```python
import functools
import math

import jax
import jax.numpy as jnp
from jax import lax
from jax.experimental import pallas as pl
from jax.experimental.pallas import tpu as pltpu

F32 = jnp.float32
BF16 = jnp.bfloat16

EPS = 1e-6
GLA_HEADS = 4
GLA_DK = 96
GLA_DV = 192
GLA_GATE_RANK = 16
GLA_GATE_NORM = 16.0
DIFF_HEADS = 6
DIFF_DK = 64
DIFF_DV = 2 * DIFF_DK
MEM_HEADS = 4
MEM_DH = 64
MEM_W = MEM_HEADS * MEM_DH
LOG2E = math.log2(math.e)

LANES = 128
SUBLANES = 8
VMEM_LIMIT_BYTES = 56 * 1024 * 1024

GLA_DK_PAD = 128
GLA_DV_PAD = 256
GLA_CHUNK = 64

A_Q0 = 0
A_K0 = A_Q0 + GLA_HEADS * GLA_DK_PAD
A_V0 = A_K0 + GLA_HEADS * GLA_DK_PAD
A_G0 = A_V0 + GLA_HEADS * GLA_DV_PAD
A_R0 = A_G0 + LANES
A_M0 = A_R0 + GLA_HEADS * GLA_DV_PAD
A_W = A_M0 + MEM_W


def _dot(a, b):
    return jnp.dot(a, b, preferred_element_type=F32)


def _dot_nt(a, b):
    return lax.dot_general(a, b, (((1,), (1,)), ((), ())), preferred_element_type=F32)


def _rms(x, g):
    ms = jnp.mean(x * x, axis=-1, keepdims=True)
    return x * lax.rsqrt(ms + EPS) * g


def _const_spec(shape):
    nd = len(shape)
    return pl.BlockSpec(shape, lambda *_: (0,) * nd, pipeline_mode=pl.Buffered(1))


def _params(sem):
    return pltpu.CompilerParams(dimension_semantics=sem, vmem_limit_bytes=VMEM_LIMIT_BYTES)


FF_CHUNK = 256


def _layer_spec(shape, layer):
    nd = len(shape)
    return pl.BlockSpec((1,) + shape, lambda *_: (layer,) + (0,) * nd, pipeline_mode=pl.Buffered(1))


def _swiglu_half_step(x, gain, wg_ref, wu_ref, wd_ref):
    h = _rms(x, gain).astype(BF16)
    acc = jnp.zeros(x.shape, F32)
    for c in range(wg_ref.shape[2] // FF_CHUNK):
        sl = slice(c * FF_CHUNK, (c + 1) * FF_CHUNK)
        g = _dot(h, wg_ref[0, :, sl])
        u = _dot(h, wu_ref[0, :, sl])
        a = (g * jax.nn.sigmoid(g) * u).astype(BF16)
        acc = acc + _dot(a, wd_ref[0, sl, :])
    return x + 0.5 * acc


def _ffn_kernel(*refs, pre, post):
    refs = list(refs)
    x_ref = refs.pop(0)
    x = x_ref[0]
    if pre == "mix_out":
        om_ref, qm_ref, mk_ref, mv_ref, wo_ref = refs[:5]
        del refs[:5]
        dm = om_ref.shape[2]
        o_mem = _mem_attend_shared(qm_ref[0], mk_ref[0], mv_ref[0].astype(BF16))
        x = x + _dot(om_ref[0], wo_ref[0, :dm, :]) + _dot(o_mem.astype(BF16), wo_ref[0, dm:, :])
    g_ref, wg_ref, wu_ref, wd_ref = refs[:4]
    del refs[:4]
    y = _swiglu_half_step(x, g_ref[0], wg_ref, wu_ref, wd_ref)
    if post == "final_norm":
        fg_ref, o_ref = refs
        o_ref[0] = _rms(y, fg_ref[...])
        return
    if post is None:
        (o_ref,) = refs
        o_ref[0] = y
        return
    pg_ref, pw_ref, o_ref = refs[:3]
    o_ref[0] = y
    h = _rms(y, pg_ref[...]).astype(BF16)
    if post == "kv":
        _store_prompt_kv(h, pw_ref, *refs[3:])
    else:
        q_ref, qm_out_ref = refs[3:]
        dqk = q_ref.shape[2]
        q_ref[0] = (_dot(h, pw_ref[0, :, :dqk]) * (DIFF_DK ** -0.5 * LOG2E)).astype(BF16)
        qm_out_ref[0] = _dot(h, pw_ref[0, :, dqk:]).astype(BF16)


def _ffn(x, gains, wg, wu, wd, layer, pre=None, post=None, pre_args=(), post_args=()):
    b, l, d = x.shape
    d_ff = wg.shape[2]
    tm = min(l, 512)
    assert l % tm == 0 and d_ff % FF_CHUNK == 0
    depth = gains.shape[0]
    row = lambda width: pl.BlockSpec((1, tm, width), lambda i, j: (i, j, 0))
    in_specs = [row(d)]
    args = [x]
    if pre == "mix_out":
        o_main, q_mem, mem_kt, mem_vt, w_out_b = pre_args
        m = mem_kt.shape[2]
        per_seq = pl.BlockSpec((1, MEM_W, m), lambda i, j: (i, 0, 0))
        in_specs += [row(o_main.shape[2]), row(MEM_W), per_seq, per_seq, _layer_spec(w_out_b.shape[1:], layer)]
        args += [o_main, q_mem, mem_kt, mem_vt, w_out_b]
    in_specs += [_layer_spec((1, d), layer), _layer_spec((d, d_ff), layer),
                 _layer_spec((d, d_ff), layer), _layer_spec((d_ff, d), layer)]
    args += [gains.reshape(depth, 1, d), wg, wu, wd]
    out_shape = [jax.ShapeDtypeStruct((b, l, d), F32)]
    out_specs = [row(d)]
    if post == "final_norm":
        (final_gain,) = post_args
        in_specs.append(_const_spec((1, d)))
        args.append(final_gain.reshape(1, d))
    elif post == "kv":
        kv_gain, w_kv_b = post_args
        in_specs += [_const_spec((1, d)), _const_spec(w_kv_b.shape)]
        args += [kv_gain.reshape(1, d), w_kv_b]
        kv_shapes, kv_specs = _prompt_kv_outputs(b, l, tm)
        out_shape += kv_shapes
        out_specs += kv_specs
    elif post == "q":
        q_gain, w_in_b, q_layer = post_args
        dqk = DIFF_HEADS * 2 * DIFF_DK
        in_specs += [_const_spec((1, d)), _layer_spec(w_in_b.shape[1:], q_layer)]
        args += [q_gain.reshape(1, d), w_in_b]
        out_shape += [jax.ShapeDtypeStruct((b, l, dqk), BF16), jax.ShapeDtypeStruct((b, l, MEM_W), BF16)]
        out_specs += [row(dqk), row(MEM_W)]
    out = pl.pallas_call(
        functools.partial(_ffn_kernel, pre=pre, post=post),
        out_shape=tuple(out_shape),
        grid=(b, l // tm),
        in_specs=in_specs,
        out_specs=tuple(out_specs),
        compiler_params=_params(("parallel", "parallel")),
        name="ffn" + ("_" + pre if pre else "") + ("_" + post if post else ""),
    )(*args)
    return out[0] if len(out) == 1 else out


def _memkv_kernel(m_ref, g_ref, w_ref, kt_ref, vt_ref):
    h = _rms(m_ref[0], g_ref[0]).astype(BF16)
    kv = _dot(h, w_ref[0])
    kt_ref[0, 0] = kv[:, :MEM_W].T
    vt_ref[0, 0] = kv[:, MEM_W:].T


def _memkv(mem, gains, w):
    depth = w.shape[0]
    b, m, d = mem.shape
    out = jax.ShapeDtypeStruct((depth, b, MEM_W, m), F32)
    return pl.pallas_call(
        _memkv_kernel,
        out_shape=(out, out),
        grid=(depth, b),
        in_specs=[
            pl.BlockSpec((1, m, d), lambda i, j: (j, 0, 0)),
            pl.BlockSpec((1, 1, d), lambda i, j: (i, 0, 0)),
            pl.BlockSpec((1, d, 2 * MEM_W), lambda i, j: (i, 0, 0)),
        ],
        out_specs=(
            pl.BlockSpec((1, 1, MEM_W, m), lambda i, j: (i, j, 0, 0)),
            pl.BlockSpec((1, 1, MEM_W, m), lambda i, j: (i, j, 0, 0)),
        ),
        compiler_params=_params(("arbitrary", "arbitrary")),
        name="memkv",
    )(mem, gains.reshape(depth, 1, d), w)


def _log_sigmoid(z):
    return jnp.minimum(z, 0.0) - jnp.log1p(jnp.exp(-jnp.abs(z)))


def _gla_recurrence_inputs(h, win_ref, wg2_ref, bg2_ref):
    q = _dot(h, win_ref[:, A_Q0:A_K0])
    k = _dot(h, win_ref[:, A_K0:A_V0])
    v = _dot(h, win_ref[:, A_V0:A_G0])
    g_lr = _dot(h, win_ref[:, A_G0:A_R0])
    z = _dot(g_lr.astype(BF16), wg2_ref[...]) + bg2_ref[...]
    logg = _log_sigmoid(z) / GLA_GATE_NORM
    return q, k, v, logg


def _gla_inputs(h, win_ref, wg2_ref, bg2_ref):
    q, k, v, logg = _gla_recurrence_inputs(h, win_ref, wg2_ref, bg2_ref)
    r = _dot(h, win_ref[:, A_R0:A_M0])
    q_mem = _dot(h, win_ref[:, A_M0:A_W])
    return q, k, v, logg, r, q_mem


def _gla_output_gate(o, r, gain):
    parts = []
    for h in range(GLA_HEADS):
        sl = slice(h * GLA_DV_PAD, (h + 1) * GLA_DV_PAD)
        oh = o[:, sl]
        ms = jnp.sum(oh * oh, axis=-1, keepdims=True) * (1.0 / GLA_DV)
        parts.append(oh * lax.rsqrt(ms + EPS) * gain[:, sl])
    on = jnp.concatenate(parts, axis=1)
    return on * (r * jax.nn.sigmoid(r))


def _mem_attend_shared(qm, mkt, mvt):
    row_head = lax.broadcasted_iota(jnp.int32, mkt.shape, 0) >> 6
    out_head = lax.broadcasted_iota(jnp.int32, (qm.shape[0], MEM_W), 1) >> 6
    scores = [_dot(qm, jnp.where(row_head == h, mkt, 0.0).astype(BF16)) * (MEM_DH ** -0.5)
              for h in range(MEM_HEADS)]
    o = jnp.zeros((qm.shape[0], MEM_W), F32)
    for h in range(MEM_HEADS):
        e = jnp.exp(scores[h] - jnp.max(scores[h], axis=-1, keepdims=True))
        l = jnp.sum(e, axis=-1, keepdims=True)
        oh = _dot_nt(e.astype(BF16), mvt) / l
        o = o + jnp.where(out_head == h, oh, 0.0)
    return o


def _mem_attend_rows(qmem, mkt_ref, mvt_ref, om_scr):
    rows = qmem.shape[0]
    r8 = lax.broadcasted_iota(jnp.int32, (SUBLANES, MEM_W), 0)
    lh = lax.broadcasted_iota(jnp.int32, (SUBLANES, MEM_W), 1) >> 6
    for j in range(rows):
        qrow = jnp.broadcast_to(qmem[j:j + 1], (SUBLANES, MEM_W))
        qm = jnp.where(r8 == lh, qrow, 0.0).astype(BF16)
        s = _dot(qm, mkt_ref[j].astype(BF16)) * (MEM_DH ** -0.5)
        e = jnp.exp(s - jnp.max(s, axis=-1, keepdims=True))
        l = jnp.sum(e, axis=-1, keepdims=True)
        oh = _dot_nt(e.astype(BF16), mvt_ref[j].astype(BF16)) / l
        om_scr[j:j + 1, :] = jnp.sum(jnp.where(r8 == lh, oh, 0.0), axis=0, keepdims=True)


def _mixer_a_prompt_kernel(x_ref, g_ref, win_ref, wg2_ref, bg2_ref, on_ref, mk_ref, mv_ref,
                           wom_ref, wome_ref, xo_ref, st_ref, s_scr):
    tl = x_ref.shape[1]
    li = pl.program_id(1)

    @pl.when(li == 0)
    def _():
        s_scr[...] = jnp.zeros(s_scr.shape, F32)

    x = x_ref[0]
    h = _rms(x, g_ref[...]).astype(BF16)
    q, k, v, logg = _gla_recurrence_inputs(h, win_ref, wg2_ref, bg2_ref)
    r = _dot(h, win_ref[:, A_R0:A_M0])
    q_mem = _dot(h, win_ref[:, A_M0:A_W])

    row = lax.broadcasted_iota(jnp.int32, (tl, tl), 0)
    col = lax.broadcasted_iota(jnp.int32, (tl, tl), 1)
    tril = ((row >> 6) == (col >> 6)) & (col <= row)
    trilb = jnp.where(tril, 1.0, 0.0).astype(BF16)
    hi = logg.astype(BF16)
    lo = (logg - hi.astype(F32)).astype(BF16)
    b = _dot(trilb, hi) + _dot(trilb, lo)

    o_mem = _mem_attend_shared(q_mem.astype(BF16), mk_ref[0], mv_ref[0].astype(BF16))

    lane_chunk = lax.broadcasted_iota(jnp.int32, (GLA_DK_PAD, tl), 1) >> 6
    scale = GLA_DK ** -0.5
    nchunk = tl // GLA_CHUNK
    qdbs, intras, decs, incs = [], [], [], []
    for hd in range(GLA_HEADS):
        ks = slice(hd * GLA_DK_PAD, (hd + 1) * GLA_DK_PAD)
        vs = slice(hd * GLA_DV_PAD, (hd + 1) * GLA_DV_PAD)
        bh = b[:, ks]
        qdb = ((q[:, ks] * scale) * jnp.exp(bh)).astype(BF16)
        ki = k[:, ks] * jnp.exp(-bh)
        att = jnp.where(tril, _dot_nt(qdb, ki.astype(BF16)), 0.0)
        vh = v[:, vs].astype(BF16)
        intras.append(_dot(att.astype(BF16), vh))
        ki_t = ki.T
        b_t = bh.T
        dec_h, inc_h = [], []
        for c in range(nchunk):
            last = c * GLA_CHUNK + GLA_CHUNK - 1
            dec = jnp.exp(b_t[:, last:last + 1])
            kend_t = jnp.where(lane_chunk == c, ki_t * dec, 0.0).astype(BF16)
            dec_h.append(dec)
            inc_h.append(_dot(kend_t, vh))
        qdbs.append(qdb)
        decs.append(dec_h)
        incs.append(inc_h)

    o_parts = []
    for hd in range(GLA_HEADS):
        state = s_scr[hd]
        starts = []
        for c in range(nchunk):
            starts.append(state.astype(BF16))
            state = decs[hd][c] * state + incs[hd][c]
        s_scr[hd] = state
        inter = [_dot(qdbs[hd][c * GLA_CHUNK:(c + 1) * GLA_CHUNK], starts[c]) for c in range(nchunk)]
        o_parts.append(intras[hd] + jnp.concatenate(inter, axis=0))
    o = jnp.concatenate(o_parts, axis=1)

    o_main = _gla_output_gate(o, r, on_ref[...])
    xo_ref[0] = x + _dot(o_main.astype(BF16), wom_ref[...]) + _dot(o_mem.astype(BF16), wome_ref[...])

    @pl.when(li == pl.num_programs(1) - 1)
    def _():
        for hd in range(GLA_HEADS):
            st_ref[0, hd] = s_scr[hd][:GLA_DK, :GLA_DV]


def _mixer_a_prompt(x, gain, wa, mem_k, mem_v):
    b, l, d = x.shape
    tl = min(l, 256)
    assert l % tl == 0 and tl % GLA_CHUNK == 0
    m = mem_k.shape[2]
    return pl.pallas_call(
        _mixer_a_prompt_kernel,
        out_shape=(
            jax.ShapeDtypeStruct((b, l, d), F32),
            jax.ShapeDtypeStruct((b, GLA_HEADS, GLA_DK, GLA_DV), F32),
        ),
        grid=(b, l // tl),
        in_specs=[
            pl.BlockSpec((1, tl, d), lambda i, j: (i, j, 0)),
            _const_spec((1, d)),
            _const_spec((d, A_W)),
            _const_spec((LANES, GLA_HEADS * GLA_DK_PAD)),
            _const_spec((1, GLA_HEADS * GLA_DK_PAD)),
            _const_spec((1, GLA_HEADS * GLA_DV_PAD)),
            pl.BlockSpec((1, MEM_W, m), lambda i, j: (i, 0, 0)),
            pl.BlockSpec((1, MEM_W, m), lambda i, j: (i, 0, 0)),
            _const_spec((GLA_HEADS * GLA_DV_PAD, d)),
            _const_spec((MEM_W, d)),
        ],
        out_specs=(
            pl.BlockSpec((1, tl, d), lambda i, j: (i, j, 0)),
            pl.BlockSpec((1, GLA_HEADS, GLA_DK, GLA_DV), lambda i, j: (i, 0, 0, 0)),
        ),
        scratch_shapes=[pltpu.VMEM((GLA_HEADS, GLA_DK_PAD, GLA_DV_PAD), F32)],
        compiler_params=_params(("parallel", "arbitrary")),
        name="mixer_a_prompt",
    )(x, gain.reshape(1, d), wa["w_in"], wa["w_gate2"], wa["b_gate2"], wa["onorm"], mem_k, mem_v,
      wa["w_out_main"], wa["w_out_mem"])


def _columns(t):
    pad = jnp.zeros((LANES - t.shape[0], t.shape[1]), F32)
    return jnp.concatenate([t, pad], axis=0).T


def _mixer_a_sample_kernel(x_ref, g_ref, win_ref, wg2_ref, bg2_ref, on_ref, st_ref, mk_ref, mv_ref,
                           wom_ref, wome_ref, xo_ref, sto_ref, o_scr, om_scr):
    rows = x_ref.shape[0]
    x = x_ref[...]
    h = _rms(x, g_ref[...]).astype(BF16)
    q, k, v, logg, r, q_mem = _gla_inputs(h, win_ref, wg2_ref, bg2_ref)
    a_t = _columns(jnp.exp(logg))
    k_t = _columns(k)
    q_t = _columns(q * (GLA_DK ** -0.5))
    o_scr[...] = jnp.zeros(o_scr.shape, F32)
    for j in range(rows):
        for hd in range(GLA_HEADS):
            rs = slice(hd * GLA_DK_PAD, hd * GLA_DK_PAD + GLA_DK)
            vs = slice(hd * GLA_DV_PAD, hd * GLA_DV_PAD + GLA_DV)
            s_new = a_t[rs, j:j + 1] * st_ref[j, hd] + k_t[rs, j:j + 1] * v[j:j + 1, vs]
            sto_ref[j, hd] = s_new
            o_scr[j:j + 1, vs] = jnp.sum(q_t[rs, j:j + 1] * s_new, axis=0, keepdims=True)
    o_main = _gla_output_gate(o_scr[...], r, on_ref[...])
    _mem_attend_rows(q_mem, mk_ref, mv_ref, om_scr)
    xo_ref[...] = (x + _dot(o_main.astype(BF16), wom_ref[...])
                   + _dot(om_scr[...].astype(BF16), wome_ref[...]))


def _mixer_a_sample(x, gain, wa, state, mem_k, mem_v):
    n, d = x.shape
    rows = SUBLANES
    assert n % rows == 0
    m = mem_k.shape[2]
    return pl.pallas_call(
        _mixer_a_sample_kernel,
        out_shape=(
            jax.ShapeDtypeStruct((n, d), F32),
            jax.ShapeDtypeStruct(state.shape, F32),
        ),
        grid=(n // rows,),
        in_specs=[
            pl.BlockSpec((rows, d), lambda i: (i, 0)),
            _const_spec((1, d)),
            _const_spec((d, A_W)),
            _const_spec((LANES, GLA_HEADS * GLA_DK_PAD)),
            _const_spec((1, GLA_HEADS * GLA_DK_PAD)),
            _const_spec((1, GLA_HEADS * GLA_DV_PAD)),
            pl.BlockSpec((rows, GLA_HEADS, GLA_DK, GLA_DV), lambda i: (i, 0, 0, 0)),
            pl.BlockSpec((rows, MEM_W, m), lambda i: (i, 0, 0)),
            pl.BlockSpec((rows, MEM_W, m), lambda i: (i, 0, 0)),
            _const_spec((GLA_HEADS * GLA_DV_PAD, d)),
            _const_spec((MEM_W, d)),
        ],
        out_specs=(
            pl.BlockSpec((rows, d), lambda i: (i, 0)),
            pl.BlockSpec((rows, GLA_HEADS, GLA_DK, GLA_DV), lambda i: (i, 0, 0, 0)),
        ),
        scratch_shapes=[
            pltpu.VMEM((rows, GLA_HEADS * GLA_DV_PAD), F32),
            pltpu.VMEM((rows, MEM_W), F32),
        ],
        compiler_params=_params(("parallel",)),
        name="mixer_a_sample",
    )(x, gain.reshape(1, d), wa["w_in"], wa["w_gate2"], wa["b_gate2"], wa["onorm"], state, mem_k, mem_v,
      wa["w_out_main"], wa["w_out_mem"])


def _store_prompt_kv(h, w_ref, kt_ref, vh_ref, kb_ref, vt_ref):
    dqk = DIFF_HEADS * 2 * DIFF_DK
    k = _dot(h, w_ref[:, :dqk])
    v = _dot(h, w_ref[:, dqk:])
    kt_ref[0] = k.T
    for hd in range(DIFF_HEADS):
        vh_ref[0, hd] = v[:, hd * DIFF_DV:(hd + 1) * DIFF_DV]
    kb_ref[0] = k.astype(BF16)
    vt_ref[0] = v.T.astype(BF16)


def _prompt_kv_outputs(b, l, tm):
    dqk = DIFF_HEADS * 2 * DIFF_DK
    dv = DIFF_HEADS * DIFF_DV
    shapes = [jax.ShapeDtypeStruct((b, dqk, l), F32),
              jax.ShapeDtypeStruct((b, DIFF_HEADS, l, DIFF_DV), F32),
              jax.ShapeDtypeStruct((b, l, dqk), BF16),
              jax.ShapeDtypeStruct((b, dv, l), BF16)]
    specs = [pl.BlockSpec((1, dqk, tm), lambda i, j: (i, 0, j)),
             pl.BlockSpec((1, DIFF_HEADS, tm, DIFF_DV), lambda i, j: (i, 0, j, 0)),
             pl.BlockSpec((1, tm, dqk), lambda i, j: (i, j, 0)),
             pl.BlockSpec((1, dv, tm), lambda i, j: (i, 0, j))]
    return shapes, specs


def _kvproj_sample_kernel(x_ref, g_ref, w_ref, k_ref, v_ref):
    h = _rms(x_ref[...], g_ref[...]).astype(BF16)
    dqk = k_ref.shape[1]
    k_ref[...] = _dot(h, w_ref[:, :dqk])
    v_ref[...] = _dot(h, w_ref[:, dqk:])


def _kvproj_sample(x, gain, w):
    n, d = x.shape
    dqk = DIFF_HEADS * 2 * DIFF_DK
    dv = DIFF_HEADS * DIFF_DV
    return pl.pallas_call(
        _kvproj_sample_kernel,
        out_shape=(jax.ShapeDtypeStruct((n, dqk), F32), jax.ShapeDtypeStruct((n, dv), F32)),
        grid=(1,),
        in_specs=[pl.BlockSpec((n, d), lambda i: (0, 0)), _const_spec((1, d)), _const_spec((d, dqk + dv))],
        out_specs=(pl.BlockSpec((n, dqk), lambda i: (0, 0)), pl.BlockSpec((n, dv), lambda i: (0, 0))),
        compiler_params=_params(("arbitrary",)),
        name="kvproj_sample",
    )(x, gain.reshape(1, d), w)


def _projb_sample_kernel(x_ref, g_ref, w_ref, q_ref, qm_ref):
    h = _rms(x_ref[...], g_ref[...]).astype(BF16)
    dqk = q_ref.shape[1]
    q_ref[...] = (_dot(h, w_ref[0, :, :dqk]) * (DIFF_DK ** -0.5 * LOG2E)).astype(BF16)
    qm_ref[...] = _dot(h, w_ref[0, :, dqk:]).astype(BF16)


def _projb_sample(x, gain, w_in_b, layer):
    n, d = x.shape
    dqk = DIFF_HEADS * 2 * DIFF_DK
    return pl.pallas_call(
        _projb_sample_kernel,
        out_shape=(jax.ShapeDtypeStruct((n, dqk), BF16), jax.ShapeDtypeStruct((n, MEM_W), BF16)),
        grid=(1,),
        in_specs=[pl.BlockSpec((n, d), lambda i: (0, 0)), _const_spec((1, d)),
                  _layer_spec(w_in_b.shape[1:], layer)],
        out_specs=(pl.BlockSpec((n, dqk), lambda i: (0, 0)), pl.BlockSpec((n, MEM_W), lambda i: (0, 0))),
        compiler_params=_params(("arbitrary",)),
        name="projb_sample",
    )(x, gain.reshape(1, d), w_in_b)


def _lambda_full(lam_ref, lam_init):
    lv = lam_ref[...]
    s1 = jnp.sum(lv[0:1] * lv[1:2], axis=-1, keepdims=True)
    s2 = jnp.sum(lv[2:3] * lv[3:4], axis=-1, keepdims=True)
    return jnp.exp(s1) - jnp.exp(s2) + lam_init


ONES_ROWS = 16
DECODE_ROWS = 16


def _decode_init(qs_ref, sqm_scr, sm_scr, sl_scr, sacc_scr):
    nrow, width = sqm_scr.shape
    r = lax.broadcasted_iota(jnp.int32, (nrow, width), 0)
    lane_half = lax.broadcasted_iota(jnp.int32, (nrow, width), 1) >> 6
    qrow = jnp.broadcast_to(qs_ref[0].astype(F32), (nrow, width))
    sqm_scr[...] = jnp.where(r == lane_half, qrow, 0.0).astype(BF16)
    sm_scr[...] = jnp.full(sm_scr.shape, -jnp.inf, F32)
    sl_scr[...] = jnp.zeros(sl_scr.shape, F32)
    sacc_scr[...] = jnp.zeros(sacc_scr.shape, F32)


def _decode_pages(kt_refs, v_refs, sqm_scr, sm_scr, sl_scr, sacc_scr):
    nrow = sqm_scr.shape[0]
    psize = kt_refs[0].shape[2]
    qm = sqm_scr[...]
    s = jnp.concatenate([_dot(qm, kr[0].astype(BF16)) for kr in kt_refs], axis=1)
    m_old = sm_scr[...]
    m_new = jnp.maximum(m_old, jnp.max(s, axis=-1, keepdims=True))
    pf = jnp.exp2(s - m_new)
    alpha = jnp.exp2(m_old - m_new)
    vrows = DIFF_HEADS * psize
    row_head = lax.broadcasted_iota(jnp.int32, (nrow, vrows), 0) >> 1
    col_head = lax.broadcasted_iota(jnp.int32, (nrow, vrows), 1) >> (psize.bit_length() - 1)
    own = row_head == col_head
    pv = jnp.zeros(sacc_scr.shape, F32)
    for i in range(len(v_refs)):
        pi = pf[:, i * psize:(i + 1) * psize]
        spread = jnp.where(own, jnp.concatenate([pi] * DIFF_HEADS, axis=1), 0.0).astype(BF16)
        pv = pv + _dot(spread, v_refs[i][0].astype(BF16))
    sl_scr[...] = alpha * sl_scr[...] + jnp.sum(pf, axis=-1, keepdims=True)
    sacc_scr[...] = alpha * sacc_scr[...] + pv
    sm_scr[...] = m_new


def _decode_finish(kn_ref, vn_ref, lam_ref, subrow_ref, os_ref, sqm_scr, sm_scr, sl_scr, sacc_scr, lam_init):
    nrow = sqm_scr.shape[0]
    s_self = jnp.sum(sqm_scr[...].astype(F32) * kn_ref[0], axis=-1, keepdims=True)
    m_old = sm_scr[...]
    m_fin = jnp.maximum(m_old, s_self)
    p_self = jnp.exp2(s_self - m_fin)
    a_fin = jnp.exp2(m_old - m_fin)
    l_fin = a_fin * sl_scr[...] + p_self
    vn = vn_ref[0]
    vn_rows = jnp.concatenate(
        [vn[:, (r // 2) * DIFF_DV:(r // 2 + 1) * DIFF_DV] for r in range(2 * DIFF_HEADS)]
        + [jnp.zeros((nrow - 2 * DIFF_HEADS, DIFF_DV), F32)], axis=0)
    acc = (a_fin * sacc_scr[...] + p_self * vn_rows) / l_fin
    lam = _lambda_full(lam_ref, lam_init)
    parts = []
    for hd in range(DIFF_HEADS):
        od = acc[2 * hd:2 * hd + 1] - lam * acc[2 * hd + 1:2 * hd + 2]
        ms = jnp.mean(od * od, axis=-1, keepdims=True)
        parts.append(od * lax.rsqrt(ms + EPS))
    on = (jnp.concatenate(parts, axis=1) * subrow_ref[...]) * (1.0 - lam_init)
    os_ref[0] = on.astype(BF16)


def _attn_kernel(qi_tab, ki_tab, pt_ref, q_ref, k_ref, vt_ref, lam_ref, sub_ref,
                 qs_ref, kn_ref, vn_ref, subrow_ref, *rest,
                 ratio, lam_init, pages_per_step, steps_per_seq, decode_steps):
    kt_refs = rest[:pages_per_step]
    v_refs = rest[pages_per_step:2 * pages_per_step]
    o_ref, os_ref, qm_scr, m_scr, acc_scr, sqm_scr, sm_scr, sl_scr, sacc_scr = rest[2 * pages_per_step:]
    decode_state = (sqm_scr, sm_scr, sl_scr, sacc_scr)
    step = pl.program_id(1)
    qi = qi_tab[step]
    ki = ki_tab[step]
    tq = q_ref.shape[1]
    tk = k_ref.shape[1]

    decoding = step < decode_steps
    group = lax.rem(step, steps_per_seq)

    @pl.when(jnp.logical_and(decoding, group == 0))
    def _():
        _decode_init(qs_ref, *decode_state)

    @pl.when(decoding)
    def _():
        _decode_pages(kt_refs, v_refs, *decode_state)

    @pl.when(jnp.logical_and(decoding, group == steps_per_seq - 1))
    def _():
        _decode_finish(kn_ref, vn_ref, lam_ref, subrow_ref, os_ref, *decode_state, lam_init)

    @pl.when(ki == 0)
    def _():
        for hd in range(DIFF_HEADS):
            qh = q_ref[0, :, hd * DIFF_DV:(hd + 1) * DIFF_DV].astype(F32)
            first = lax.broadcasted_iota(jnp.int32, qh.shape, 1) < DIFF_DK
            qm_scr[hd, :tq] = jnp.where(first, qh, 0.0).astype(BF16)
            qm_scr[hd, tq:] = jnp.where(first, 0.0, qh).astype(BF16)
        m_scr[...] = jnp.full(m_scr.shape, -jnp.inf, F32)
        acc_scr[...] = jnp.zeros(acc_scr.shape, F32)

    def block(masked):
        if masked:
            col = lax.broadcasted_iota(jnp.int32, (tk, 2 * tq), 1)
            kpos = ki * tk + lax.broadcasted_iota(jnp.int32, (tk, 2 * tq), 0)
            keep = kpos <= qi * tq + jnp.where(col >= tq, col - tq, col)
        ones = jnp.ones((ONES_ROWS, tk), BF16)

        def scores(hd):
            st = _dot_nt(k_ref[0, :, hd * DIFF_DV:(hd + 1) * DIFF_DV], qm_scr[hd])
            return jnp.where(keep, st, -jnp.inf) if masked else st

        st_next = scores(0)
        for hd in range(DIFF_HEADS):
            hs = slice(hd * DIFF_DV, (hd + 1) * DIFF_DV)
            st = st_next
            if hd + 1 < DIFF_HEADS:
                st_next = scores(hd + 1)
            m_old = m_scr[hd]
            m_new = jnp.maximum(m_old, jnp.max(st, axis=0, keepdims=True))
            p = jnp.exp2(st - m_new).astype(BF16)
            alpha = jnp.exp2(m_old - m_new)
            v_aug = jnp.concatenate([vt_ref[0, hs, :], ones], axis=0)
            acc_scr[hd] = alpha * acc_scr[hd] + _dot(v_aug, p)
            m_scr[hd] = m_new

    diag = ki >= qi * ratio

    @pl.when(diag)
    def _():
        block(True)

    @pl.when(jnp.logical_not(diag))
    def _():
        block(False)

    @pl.when(ki == (qi + 1) * ratio - 1)
    def _():
        lam = _lambda_full(lam_ref, lam_init)
        for hd in range(DIFF_HEADS):
            acc = acc_scr[hd]
            o = acc[:DIFF_DV] / acc[DIFF_DV:DIFF_DV + 1]
            od = o[:, :tq] - lam * o[:, tq:]
            ms = jnp.mean(od * od, axis=0, keepdims=True)
            on = (od * lax.rsqrt(ms + EPS) * sub_ref[...]) * (1.0 - lam_init)
            o_ref[0, :, hd * DIFF_DV:(hd + 1) * DIFF_DV] = on.T.astype(BF16)


def _attention(q, kb, vt, q_s, k_new, v_new, cache_k, cache_v, page_table, lam_vecs, subln, lam_init):
    b, l, dqk = q.shape
    dv = vt.shape[1]
    n = q_s.shape[0]
    n_phys, psize = cache_k.shape[:2]
    n_pages = page_table.shape[1]
    tq = min(l, 512)
    tk = min(l, 256)
    assert l % tq == 0 and tq % tk == 0 and n % b == 0 and psize & (psize - 1) == 0
    ratio = tq // tk
    pairs = [(i, j) for i in range(l // tq) for j in range((i + 1) * ratio)]
    qi_tab = jnp.asarray([p[0] for p in pairs], jnp.int32)
    ki_tab = jnp.asarray([p[1] for p in pairs], jnp.int32)
    nsteps = len(pairs)
    spb = n // b
    fits = [c for c in (8, 16, 32) if n_pages % c == 0 and spb * (n_pages // c) <= nsteps]
    assert fits, "prompt sweep too short to carry the sample group's page sweep"
    pps = fits[0]
    sps = n_pages // pps
    decode_steps = spb * sps

    ckt = jnp.transpose(cache_k, (0, 2, 3, 4, 1)).reshape(n_phys, dqk, psize)
    cvh = jnp.transpose(cache_v, (0, 2, 1, 3)).reshape(n_phys, DIFF_HEADS * psize, DIFF_DV)

    def seq_of(i, s):
        return i * spb + jnp.minimum(s, decode_steps - 1) // sps

    def page_spec(j, shape):
        def index(i, s, qt, kt, pt):
            group = jnp.minimum(s, decode_steps - 1) % sps
            return (pt[seq_of(i, s) * n_pages + group * pps + j], 0, 0)
        return pl.BlockSpec((1,) + shape, index)

    seq_spec = pl.BlockSpec((1, 1, dqk), lambda i, s, qt, kt, pt: (seq_of(i, s), 0, 0))
    const2 = lambda shape: pl.BlockSpec(shape, lambda i, s, qt, kt, pt: (0, 0))
    grid_spec = pltpu.PrefetchScalarGridSpec(
        num_scalar_prefetch=3,
        grid=(b, nsteps),
        in_specs=[
            pl.BlockSpec((1, tq, dqk), lambda i, s, qt, kt, pt: (i, qt[s], 0)),
            pl.BlockSpec((1, tk, dqk), lambda i, s, qt, kt, pt: (i, kt[s], 0)),
            pl.BlockSpec((1, dv, tk), lambda i, s, qt, kt, pt: (i, 0, kt[s])),
            const2((4, DIFF_DK)),
            const2((DIFF_DV, 1)),
            seq_spec, seq_spec, seq_spec,
            const2((1, dv)),
        ]
        + [page_spec(j, (dqk, psize)) for j in range(pps)]
        + [page_spec(j, (DIFF_HEADS * psize, DIFF_DV)) for j in range(pps)],
        out_specs=(
            pl.BlockSpec((1, tq, dv), lambda i, s, qt, kt, pt: (i, qt[s], 0)),
            pl.BlockSpec((1, 1, dv), lambda i, s, qt, kt, pt: (seq_of(i, s), 0, 0)),
        ),
        scratch_shapes=[
            pltpu.VMEM((DIFF_HEADS, 2 * tq, DIFF_DV), BF16),
            pltpu.VMEM((DIFF_HEADS, 1, 2 * tq), F32),
            pltpu.VMEM((DIFF_HEADS, DIFF_DV + ONES_ROWS, 2 * tq), F32),
            pltpu.VMEM((DECODE_ROWS, dqk), BF16),
            pltpu.VMEM((DECODE_ROWS, 1), F32),
            pltpu.VMEM((DECODE_ROWS, 1), F32),
            pltpu.VMEM((DECODE_ROWS, DIFF_DV), F32),
        ],
    )
    o_p, o_s = pl.pallas_call(
        functools.partial(_attn_kernel, ratio=ratio, lam_init=lam_init, pages_per_step=pps,
                          steps_per_seq=sps, decode_steps=decode_steps),
        out_shape=(jax.ShapeDtypeStruct((b, l, dv), BF16), jax.ShapeDtypeStruct((n, 1, dv), BF16)),
        grid_spec=grid_spec,
        compiler_params=_params(("arbitrary", "arbitrary")),
        name="attention",
    )(qi_tab, ki_tab, page_table.reshape(-1), q, kb, vt, lam_vecs, subln.reshape(DIFF_DV, 1),
      q_s.reshape(n, 1, dqk), k_new.reshape(n, 1, dqk), v_new.reshape(n, 1, dv),
      jnp.tile(subln, DIFF_HEADS).reshape(1, dv), *([ckt] * pps), *([cvh] * pps))
    return o_p, o_s.reshape(n, dv)


def _post_sample_kernel(x_ref, om_ref, qm_ref, mk_ref, mv_ref, wo_ref, xo_ref, om_scr):
    _mem_attend_rows(qm_ref[...].astype(F32), mk_ref, mv_ref, om_scr)
    dm = om_ref.shape[1]
    xo_ref[...] = (x_ref[...] + _dot(om_ref[...], wo_ref[0, :dm, :])
                   + _dot(om_scr[...].astype(BF16), wo_ref[0, dm:, :]))


def _post_sample(x, o_main, q_mem, mem_k, mem_v, w_out_b, layer):
    n, d = x.shape
    rows = 2 * SUBLANES
    assert n % rows == 0
    dm = o_main.shape[1]
    m = mem_k.shape[2]
    return pl.pallas_call(
        _post_sample_kernel,
        out_shape=jax.ShapeDtypeStruct((n, d), F32),
        grid=(n // rows,),
        in_specs=[
            pl.BlockSpec((rows, d), lambda i: (i, 0)),
            pl.BlockSpec((rows, dm), lambda i: (i, 0)),
            pl.BlockSpec((rows, MEM_W), lambda i: (i, 0)),
            pl.BlockSpec((rows, MEM_W, m), lambda i: (i, 0, 0)),
            pl.BlockSpec((rows, MEM_W, m), lambda i: (i, 0, 0)),
            _layer_spec(w_out_b.shape[1:], layer),
        ],
        out_specs=pl.BlockSpec((rows, d), lambda i: (i, 0)),
        scratch_shapes=[pltpu.VMEM((rows, MEM_W), F32)],
        compiler_params=_params(("parallel",)),
        name="post_sample",
    )(x, o_main, q_mem, mem_k, mem_v, w_out_b)


def _pad_heads_cols(w, heads, width, padded):
    lead = w.shape[:-1]
    w = w.reshape(lead + (heads, width))
    w = jnp.pad(w, [(0, 0)] * len(lead) + [(0, 0), (0, padded - width)])
    return w.reshape(lead + (heads * padded,))


def _prep_layer_a(w_in, w_gate2, b_gate2, onorm, w_out):
    dqk = GLA_HEADS * GLA_DK
    dv = GLA_HEADS * GLA_DV
    o = 0
    wq = w_in[:, o:o + dqk]; o += dqk
    wk = w_in[:, o:o + dqk]; o += dqk
    wv = w_in[:, o:o + dv]; o += dv
    wg = w_in[:, o:o + GLA_GATE_RANK]; o += GLA_GATE_RANK
    wr = w_in[:, o:o + dv]; o += dv
    wm = w_in[:, o:o + MEM_W]
    w_all = jnp.concatenate([
        _pad_heads_cols(wq, GLA_HEADS, GLA_DK, GLA_DK_PAD),
        _pad_heads_cols(wk, GLA_HEADS, GLA_DK, GLA_DK_PAD),
        _pad_heads_cols(wv, GLA_HEADS, GLA_DV, GLA_DV_PAD),
        jnp.pad(wg, ((0, 0), (0, LANES - GLA_GATE_RANK))),
        _pad_heads_cols(wr, GLA_HEADS, GLA_DV, GLA_DV_PAD),
        wm,
    ], axis=1).astype(BF16)
    wg2 = _pad_heads_cols(w_gate2, GLA_HEADS, GLA_DK, GLA_DK_PAD)
    wg2 = jnp.pad(wg2, ((0, LANES - GLA_GATE_RANK), (0, 0))).astype(BF16)
    bg2 = _pad_heads_cols(b_gate2.reshape(1, dqk), GLA_HEADS, GLA_DK, GLA_DK_PAD)
    gain = _pad_heads_cols(jnp.tile(onorm, GLA_HEADS).reshape(1, dv), GLA_HEADS, GLA_DV, GLA_DV_PAD)
    wo_main = _pad_heads_cols(w_out[:dv].T, GLA_HEADS, GLA_DV, GLA_DV_PAD).T.astype(BF16)
    return {
        "w_in": w_all, "w_gate2": wg2, "b_gate2": bg2, "onorm": gain,
        "w_out_main": wo_main, "w_out_mem": w_out[dv:].astype(BF16),
    }


def kernel(x_prompt, x_sample, mem_prompt, state_gla, cache_k, cache_v, cache_mem_k, cache_mem_v,
           page_table, ffn1_norm, ffn1_w_gate, ffn1_w_up, ffn1_w_down, mix_norm, a_w_in, a_w_gate2,
           a_b_gate2, a_onorm, b_w_in, b_lambda_q1, b_lambda_k1, b_lambda_q2, b_lambda_k2, b_subln,
           mem_norm, w_mem_kv, w_out, ffn2_norm, ffn2_w_gate, ffn2_w_up, ffn2_w_down, kv_norm, w_kv,
           final_norm):
    bp, seq, d = x_prompt.shape
    db, dseq, _ = x_sample.shape
    assert dseq == 1
    depth = ffn1_norm.shape[0]
    n_a = a_w_in.shape[0]
    mtok = mem_prompt.shape[1]
    assert depth > n_a
    bf = lambda t: t.astype(BF16)
    f1 = (bf(ffn1_w_gate), bf(ffn1_w_up), bf(ffn1_w_down))
    f2 = (bf(ffn2_w_gate), bf(ffn2_w_up), bf(ffn2_w_down))
    w_kv_b = bf(w_kv)
    w_mem_b = bf(w_mem_kv)
    w_out_b = bf(w_out)
    b_w_in_b = bf(b_w_in)

    mem_k_p, mem_v_p = _memkv(mem_prompt, mem_norm, w_mem_b)
    cmk = jnp.transpose(cache_mem_k, (0, 1, 3, 4, 2)).reshape(depth, db, MEM_W, mtok)
    cmv = jnp.transpose(cache_mem_v, (0, 1, 3, 4, 2)).reshape(depth, db, MEM_W, mtok)

    xp = x_prompt
    xs = x_sample.reshape(1, db, d)
    states_p, states_s = [], []
    kv_p = kv_s = None
    for i in range(depth):
        last = i == depth - 1
        is_a = i < n_a
        j = i - n_a
        if is_a:
            xp = _ffn(xp, ffn1_norm, *f1, i)
        else:
            xp, q_p, qm_p = _ffn(xp, ffn1_norm, *f1, i, post="q", post_args=(mix_norm[i], b_w_in_b, j))
        xs = _ffn(xs, ffn1_norm, *f1, i)
        pre, pre_args = None, ()
        if is_a:
            wa = _prep_layer_a(a_w_in[i], a_w_gate2[i], a_b_gate2[i], a_onorm[i], w_out[i])
            xp, st_p = _mixer_a_prompt(xp, mix_norm[i], wa, mem_k_p[i], mem_v_p[i])
            xs2, st_s = _mixer_a_sample(xs[0], mix_norm[i], wa, state_gla[i], cmk[i], cmv[i])
            xs = xs2[None]
            states_p.append(st_p)
            states_s.append(st_s)
        else:
            lam_init = 0.8 - 0.6 * math.exp(-0.3 * i)
            lam_vecs = jnp.stack([b_lambda_q1[j], b_lambda_k1[j], b_lambda_q2[j], b_lambda_k2[j]])
            _, _, kb_p, vt_p = kv_p
            k_s, v_s = kv_s
            q_s, qm_s = _projb_sample(xs[0], mix_norm[i], b_w_in_b, j)
            o_p, o_s = _attention(q_p, kb_p, vt_p, q_s, k_s, v_s, cache_k, cache_v, page_table,
                                  lam_vecs, b_subln[j], lam_init)
            pre, pre_args = "mix_out", (o_p, qm_p, mem_k_p[i], mem_v_p[i], w_out_b)
            xs = _post_sample(xs[0], o_s, qm_s, cmk[i], cmv[i], w_out_b, i)[None]
        if last:
            xp = _ffn(xp, ffn2_norm, *f2, i, pre=pre, pre_args=pre_args, post="final_norm", post_args=(final_norm,))
            xs = _ffn(xs, ffn2_norm, *f2, i, post="final_norm", post_args=(final_norm,))
        elif i == n_a - 1:
            xp, *kv_p = _ffn(xp, ffn2_norm, *f2, i, pre=pre, pre_args=pre_args, post="kv",
                             post_args=(kv_norm, w_kv_b))
            xs = _ffn(xs, ffn2_norm, *f2, i)
            kv_s = _kvproj_sample(xs[0], kv_norm, w_kv_b)
        else:
            xp = _ffn(xp, ffn2_norm, *f2, i, pre=pre, pre_args=pre_args)
            xs = _ffn(xs, ffn2_norm, *f2, i)

    y_prompt = xp
    y_sample = xs.reshape(db, 1, d)
    kt_p, vh_p = kv_p[0], kv_p[1]
    k_s, v_s = kv_s

    def mem_rows(t):
        return jnp.transpose(t.reshape(depth, bp, MEM_HEADS, MEM_DH, mtok), (0, 1, 4, 2, 3))

    return (
        y_prompt,
        y_sample,
        jnp.stack(states_p),
        jnp.stack(states_s),
        jnp.transpose(kt_p.reshape(bp, DIFF_HEADS, 2, DIFF_DK, seq), (0, 4, 1, 2, 3)),
        jnp.transpose(vh_p, (0, 2, 1, 3)),
        k_s.reshape(db, 1, DIFF_HEADS, 2, DIFF_DK),
        v_s.reshape(db, 1, DIFF_HEADS, DIFF_DV),
        mem_rows(mem_k_p),
        mem_rows(mem_v_p),
    )
```

```python
import functools
import math

import jax
import jax.numpy as jnp
from jax import lax
from jax.experimental import pallas as pl
from jax.experimental.pallas import tpu as pltpu

F32 = jnp.float32
BF16 = jnp.bfloat16

EPS = 1e-6
GLA_HEADS = 4
GLA_DK = 96
GLA_DV = 192
GLA_GATE_RANK = 16
GLA_GATE_NORM = 16.0
DIFF_HEADS = 6
DIFF_DK = 64
DIFF_DV = 2 * DIFF_DK
MEM_HEADS = 4
MEM_DH = 64
MEM_W = MEM_HEADS * MEM_DH
LOG2E = math.log2(math.e)

LANES = 128
SUBLANES = 8
VMEM_LIMIT_BYTES = 56 * 1024 * 1024

GLA_DK_PAD = 128
GLA_DV_PAD = 256
GLA_CHUNK = 64

A_Q0 = 0
A_K0 = A_Q0 + GLA_HEADS * GLA_DK_PAD
A_V0 = A_K0 + GLA_HEADS * GLA_DK_PAD
A_G0 = A_V0 + GLA_HEADS * GLA_DV_PAD
A_R0 = A_G0 + LANES
A_M0 = A_R0 + GLA_HEADS * GLA_DV_PAD
A_W = A_M0 + MEM_W


def _dot(a, b):
    return jnp.dot(a, b, preferred_element_type=F32)


def _dot_nt(a, b):
    return lax.dot_general(a, b, (((1,), (1,)), ((), ())), preferred_element_type=F32)


def _rms(x, g):
    ms = jnp.mean(x * x, axis=-1, keepdims=True)
    return x * lax.rsqrt(ms + EPS) * g


def _const_spec(shape):
    nd = len(shape)
    return pl.BlockSpec(shape, lambda *_: (0,) * nd, pipeline_mode=pl.Buffered(1))


def _params(sem):
    return pltpu.CompilerParams(dimension_semantics=sem, vmem_limit_bytes=VMEM_LIMIT_BYTES)


FF_CHUNK = 256


def _layer_spec(shape, layer):
    nd = len(shape)
    return pl.BlockSpec((1,) + shape, lambda *_: (layer,) + (0,) * nd, pipeline_mode=pl.Buffered(1))


def _swiglu_half_step(x, gain, wg_ref, wu_ref, wd_ref, side_work=None):
    h = _rms(x, gain).astype(BF16)
    nchunk = wg_ref.shape[2] // FF_CHUNK

    def up(c):
        sl = slice(c * FF_CHUNK, (c + 1) * FF_CHUNK)
        return _dot(h, wg_ref[0, :, sl]), _dot(h, wu_ref[0, :, sl])

    acc = jnp.zeros(x.shape, F32)
    g, u = up(0)
    for c in range(nchunk):
        a = (g * jax.nn.sigmoid(g) * u).astype(BF16)
        if c + 1 < nchunk:
            g, u = up(c + 1)
        if side_work is not None:
            side_work[c]()
        acc = acc + _dot(a, wd_ref[0, c * FF_CHUNK:(c + 1) * FF_CHUNK, :])
    return x + 0.5 * acc


_N_POST_IN = {None: 0, "final_norm": 1, "kv": 2, "q": 2}
_N_POST_OUT = {None: 0, "final_norm": 0, "kv": 4, "q": 2}
_N_DECODE_IN = 5


def _ffn_kernel(*refs, pre, post, decode):
    refs = list(refs)
    if decode:
        refs.pop(0)
    n_in = 1 + (5 if pre else 0) + 4 + _N_POST_IN[post]
    ins, refs = refs[:n_in], refs[n_in:]
    if decode:
        pps, sps, lam_init = decode
        n_dec = _N_DECODE_IN + 2 * pps
        (qs_ref, kn_ref, vn_ref, lam_ref, subrow_ref), pages = refs[:_N_DECODE_IN], refs[_N_DECODE_IN:n_dec]
        refs = refs[n_dec:]
    n_out = 1 + _N_POST_OUT[post]
    outs, refs = refs[:n_out], refs[n_out:]
    if decode:
        os_ref, *decode_state = refs
        step = pl.program_id(0) * pl.num_programs(1) + pl.program_id(1)
        group = lax.rem(step, sps)

        @pl.when(group == 0)
        def _():
            _decode_init(qs_ref, *decode_state)

    x = ins[0][0]
    ins = ins[1:]
    if pre == "mix_out":
        om_ref, qm_ref, mk_ref, mv_ref, wo_ref = ins[:5]
        ins = ins[5:]
        dm = om_ref.shape[2]
        o_mem = _mem_attend_shared(qm_ref[0], mk_ref[0], mv_ref[0].astype(BF16))
        x = x + _dot(om_ref[0], wo_ref[0, :dm, :]) + _dot(o_mem.astype(BF16), wo_ref[0, dm:, :])
    g_ref, wg_ref, wu_ref, wd_ref = ins[:4]
    ins = ins[4:]
    side_work = None
    if decode:
        side_work = _decode_side_work(pages[:pps], pages[pps:], *decode_state, wg_ref.shape[2] // FF_CHUNK)
    y = _swiglu_half_step(x, g_ref[0], wg_ref, wu_ref, wd_ref, side_work)
    o_ref = outs[0]
    if post == "final_norm":
        o_ref[0] = _rms(y, ins[0][...])
    else:
        o_ref[0] = y
    if post in ("kv", "q"):
        pg_ref, pw_ref = ins
        h = _rms(y, pg_ref[...]).astype(BF16)
        if post == "kv":
            _store_prompt_kv(h, pw_ref, *outs[1:])
        else:
            q_ref, qm_out_ref = outs[1:]
            dqk = q_ref.shape[2]
            q_ref[0] = (_dot(h, pw_ref[0, :, :dqk]) * (DIFF_DK ** -0.5 * LOG2E)).astype(BF16)
            qm_out_ref[0] = _dot(h, pw_ref[0, :, dqk:]).astype(BF16)

    if decode:
        @pl.when(group == sps - 1)
        def _():
            _decode_finish(kn_ref, vn_ref, lam_ref, subrow_ref, os_ref, *decode_state, lam_init)


def _ffn(x, gains, wg, wu, wd, layer, pre=None, post=None, pre_args=(), post_args=(), decode=None):
    b, l, d = x.shape
    d_ff = wg.shape[2]
    tm = min(l, 256 if decode else 512)
    assert l % tm == 0 and d_ff % FF_CHUNK == 0
    depth = gains.shape[0]
    nlt = l // tm
    row = lambda width: pl.BlockSpec((1, tm, width), lambda i, j, *_: (i, j, 0))
    in_specs = [row(d)]
    args = [x]
    if pre == "mix_out":
        o_main, q_mem, mem_kt, mem_vt, w_out_b = pre_args
        m = mem_kt.shape[2]
        per_seq = pl.BlockSpec((1, MEM_W, m), lambda i, j, *_: (i, 0, 0))
        in_specs += [row(o_main.shape[2]), row(MEM_W), per_seq, per_seq, _layer_spec(w_out_b.shape[1:], layer)]
        args += [o_main, q_mem, mem_kt, mem_vt, w_out_b]
    in_specs += [_layer_spec((1, d), layer), _layer_spec((d, d_ff), layer),
                 _layer_spec((d, d_ff), layer), _layer_spec((d_ff, d), layer)]
    args += [gains.reshape(depth, 1, d), wg, wu, wd]
    out_shape = [jax.ShapeDtypeStruct((b, l, d), F32)]
    out_specs = [row(d)]
    if post == "final_norm":
        (final_gain,) = post_args
        in_specs.append(_const_spec((1, d)))
        args.append(final_gain.reshape(1, d))
    elif post == "kv":
        kv_gain, w_kv_b = post_args
        in_specs += [_const_spec((1, d)), _const_spec(w_kv_b.shape)]
        args += [kv_gain.reshape(1, d), w_kv_b]
        kv_shapes, kv_specs = _prompt_kv_outputs(b, l, tm)
        out_shape += kv_shapes
        out_specs += kv_specs
    elif post == "q":
        q_gain, w_in_b, q_layer = post_args
        dqk = DIFF_HEADS * 2 * DIFF_DK
        in_specs += [_const_spec((1, d)), _layer_spec(w_in_b.shape[1:], q_layer)]
        args += [q_gain.reshape(1, d), w_in_b]
        out_shape += [jax.ShapeDtypeStruct((b, l, dqk), BF16), jax.ShapeDtypeStruct((b, l, MEM_W), BF16)]
        out_specs += [row(dqk), row(MEM_W)]
    name = "ffn" + ("_" + pre if pre else "") + ("_" + post if post else "")
    kernel_kwargs = dict(pre=pre, post=post, decode=None)
    scratch, prefetch = [], []
    if decode:
        job = decode
        dqk, dv, n_pages, psize = job["dqk"], job["dv"], job["n_pages"], job["psize"]
        seq0, nseq = job["seq0"], job["nseq"]
        nsteps = b * nlt
        assert nsteps % nseq == 0, "each sequence's page sweep must take a whole number of grid steps"
        sps = nsteps // nseq
        assert n_pages % sps == 0
        pps = n_pages // sps

        def seq_local(i, j):
            return (i * nlt + j) // sps

        def page_spec(k, shape):
            def index(i, j, pt):
                group = (i * nlt + j) % sps
                return (pt[(seq0 + seq_local(i, j)) * n_pages + group * pps + k], 0, 0)
            return pl.BlockSpec((1,) + shape, index)

        seq_spec = pl.BlockSpec((1, 1, dqk), lambda i, j, pt: (seq0 + seq_local(i, j), 0, 0))
        in_specs += [seq_spec, seq_spec, seq_spec,
                     pl.BlockSpec((4, DIFF_DK), lambda i, j, pt: (0, 0)),
                     pl.BlockSpec((1, dv), lambda i, j, pt: (0, 0))]
        in_specs += [page_spec(k, (dqk, psize)) for k in range(pps)]
        in_specs += [page_spec(k, (DIFF_HEADS * psize, DIFF_DV)) for k in range(pps)]
        args += [job["q"], job["k_new"], job["v_new"], job["lam_vecs"], job["subrow"]]
        args += [job["ckt"]] * pps + [job["cvh"]] * pps
        out_shape.append(jax.ShapeDtypeStruct((nseq, 1, dv), BF16))
        out_specs.append(pl.BlockSpec((1, 1, dv), lambda i, j, pt: (seq_local(i, j), 0, 0)))
        scratch = [pltpu.VMEM((DECODE_ROWS, dqk), BF16), pltpu.VMEM((DECODE_ROWS, 1), F32),
                   pltpu.VMEM((DECODE_ROWS, 1), F32), pltpu.VMEM((DECODE_ROWS, DIFF_DV), F32)]
        prefetch = [job["page_table"]]
        kernel_kwargs["decode"] = (pps, sps, job["lam_init"])
        name += "_decode"
    out = pl.pallas_call(
        functools.partial(_ffn_kernel, **kernel_kwargs),
        out_shape=tuple(out_shape),
        grid_spec=pltpu.PrefetchScalarGridSpec(
            num_scalar_prefetch=len(prefetch), grid=(b, nlt), in_specs=in_specs,
            out_specs=tuple(out_specs), scratch_shapes=scratch),
        compiler_params=_params(("arbitrary", "arbitrary") if decode else ("parallel", "parallel")),
        name=name,
    )(*prefetch, *args)
    return out[0] if len(out) == 1 else out


def _memkv_kernel(m_ref, g_ref, w_ref, kt_ref, vt_ref):
    h = _rms(m_ref[0], g_ref[0]).astype(BF16)
    kv = _dot(h, w_ref[0])
    kt_ref[0, 0] = kv[:, :MEM_W].T
    vt_ref[0, 0] = kv[:, MEM_W:].T


def _memkv(mem, gains, w):
    depth = w.shape[0]
    b, m, d = mem.shape
    out = jax.ShapeDtypeStruct((depth, b, MEM_W, m), F32)
    return pl.pallas_call(
        _memkv_kernel,
        out_shape=(out, out),
        grid=(depth, b),
        in_specs=[
            pl.BlockSpec((1, m, d), lambda i, j: (j, 0, 0)),
            pl.BlockSpec((1, 1, d), lambda i, j: (i, 0, 0)),
            pl.BlockSpec((1, d, 2 * MEM_W), lambda i, j: (i, 0, 0)),
        ],
        out_specs=(
            pl.BlockSpec((1, 1, MEM_W, m), lambda i, j: (i, j, 0, 0)),
            pl.BlockSpec((1, 1, MEM_W, m), lambda i, j: (i, j, 0, 0)),
        ),
        compiler_params=_params(("arbitrary", "arbitrary")),
        name="memkv",
    )(mem, gains.reshape(depth, 1, d), w)


def _log_sigmoid(z):
    return jnp.minimum(z, 0.0) - jnp.log1p(jnp.exp(-jnp.abs(z)))


def _gla_recurrence_inputs(h, win_ref, wg2_ref, bg2_ref):
    q = _dot(h, win_ref[:, A_Q0:A_K0])
    k = _dot(h, win_ref[:, A_K0:A_V0])
    v = _dot(h, win_ref[:, A_V0:A_G0])
    g_lr = _dot(h, win_ref[:, A_G0:A_R0])
    z = _dot(g_lr.astype(BF16), wg2_ref[...]) + bg2_ref[...]
    logg = _log_sigmoid(z) / GLA_GATE_NORM
    return q, k, v, logg


def _gla_inputs(h, win_ref, wg2_ref, bg2_ref):
    q, k, v, logg = _gla_recurrence_inputs(h, win_ref, wg2_ref, bg2_ref)
    r = _dot(h, win_ref[:, A_R0:A_M0])
    q_mem = _dot(h, win_ref[:, A_M0:A_W])
    return q, k, v, logg, r, q_mem


def _gla_output_gate(o, r, gain):
    parts = []
    for h in range(GLA_HEADS):
        sl = slice(h * GLA_DV_PAD, (h + 1) * GLA_DV_PAD)
        oh = o[:, sl]
        ms = jnp.sum(oh * oh, axis=-1, keepdims=True) * (1.0 / GLA_DV)
        parts.append(oh * lax.rsqrt(ms + EPS) * gain[:, sl])
    on = jnp.concatenate(parts, axis=1)
    return on * (r * jax.nn.sigmoid(r))


def _mem_attend_shared(qm, mkt, mvt):
    row_head = lax.broadcasted_iota(jnp.int32, mkt.shape, 0) >> 6
    out_head = lax.broadcasted_iota(jnp.int32, (qm.shape[0], MEM_W), 1) >> 6
    scores = [_dot(qm, jnp.where(row_head == h, mkt, 0.0).astype(BF16)) * (MEM_DH ** -0.5)
              for h in range(MEM_HEADS)]
    o = jnp.zeros((qm.shape[0], MEM_W), F32)
    for h in range(MEM_HEADS):
        e = jnp.exp(scores[h] - jnp.max(scores[h], axis=-1, keepdims=True))
        l = jnp.sum(e, axis=-1, keepdims=True)
        oh = _dot_nt(e.astype(BF16), mvt) / l
        o = o + jnp.where(out_head == h, oh, 0.0)
    return o


def _mem_attend_rows(qmem, mkt_ref, mvt_ref, om_scr):
    rows = qmem.shape[0]
    r8 = lax.broadcasted_iota(jnp.int32, (SUBLANES, MEM_W), 0)
    lh = lax.broadcasted_iota(jnp.int32, (SUBLANES, MEM_W), 1) >> 6
    for j in range(rows):
        qrow = jnp.broadcast_to(qmem[j:j + 1], (SUBLANES, MEM_W))
        qm = jnp.where(r8 == lh, qrow, 0.0).astype(BF16)
        s = _dot(qm, mkt_ref[j].astype(BF16)) * (MEM_DH ** -0.5)
        e = jnp.exp(s - jnp.max(s, axis=-1, keepdims=True))
        l = jnp.sum(e, axis=-1, keepdims=True)
        oh = _dot_nt(e.astype(BF16), mvt_ref[j].astype(BF16)) / l
        om_scr[j:j + 1, :] = jnp.sum(jnp.where(r8 == lh, oh, 0.0), axis=0, keepdims=True)


def _mixer_a_prompt_kernel(x_ref, g_ref, win_ref, wg2_ref, bg2_ref, on_ref, mk_ref, mv_ref,
                           wom_ref, wome_ref, xo_ref, st_ref, s_scr):
    tl = x_ref.shape[1]
    li = pl.program_id(1)

    @pl.when(li == 0)
    def _():
        s_scr[...] = jnp.zeros(s_scr.shape, F32)

    x = x_ref[0]
    h = _rms(x, g_ref[...]).astype(BF16)
    q, k, v, logg = _gla_recurrence_inputs(h, win_ref, wg2_ref, bg2_ref)
    r = _dot(h, win_ref[:, A_R0:A_M0])
    q_mem = _dot(h, win_ref[:, A_M0:A_W])

    row = lax.broadcasted_iota(jnp.int32, (tl, tl), 0)
    col = lax.broadcasted_iota(jnp.int32, (tl, tl), 1)
    tril = ((row >> 6) == (col >> 6)) & (col <= row)
    trilb = jnp.where(tril, 1.0, 0.0).astype(BF16)
    hi = logg.astype(BF16)
    lo = (logg - hi.astype(F32)).astype(BF16)
    b = _dot(trilb, hi) + _dot(trilb, lo)

    o_mem = _mem_attend_shared(q_mem.astype(BF16), mk_ref[0], mv_ref[0].astype(BF16))

    lane_chunk = lax.broadcasted_iota(jnp.int32, (GLA_DK_PAD, tl), 1) >> 6
    scale = GLA_DK ** -0.5
    nchunk = tl // GLA_CHUNK
    qdbs, intras, decs, incs = [], [], [], []
    for hd in range(GLA_HEADS):
        ks = slice(hd * GLA_DK_PAD, (hd + 1) * GLA_DK_PAD)
        vs = slice(hd * GLA_DV_PAD, (hd + 1) * GLA_DV_PAD)
        bh = b[:, ks]
        qdb = ((q[:, ks] * scale) * jnp.exp(bh)).astype(BF16)
        ki = k[:, ks] * jnp.exp(-bh)
        att = jnp.where(tril, _dot_nt(qdb, ki.astype(BF16)), 0.0)
        vh = v[:, vs].astype(BF16)
        intras.append(_dot(att.astype(BF16), vh))
        ki_t = ki.T
        b_t = bh.T
        dec_h, inc_h = [], []
        for c in range(nchunk):
            last = c * GLA_CHUNK + GLA_CHUNK - 1
            dec = jnp.exp(b_t[:, last:last + 1])
            kend_t = jnp.where(lane_chunk == c, ki_t * dec, 0.0).astype(BF16)
            dec_h.append(dec)
            inc_h.append(_dot(kend_t, vh))
        qdbs.append(qdb)
        decs.append(dec_h)
        incs.append(inc_h)

    o_parts = []
    for hd in range(GLA_HEADS):
        state = s_scr[hd]
        starts = []
        for c in range(nchunk):
            starts.append(state.astype(BF16))
            state = decs[hd][c] * state + incs[hd][c]
        s_scr[hd] = state
        inter = [_dot(qdbs[hd][c * GLA_CHUNK:(c + 1) * GLA_CHUNK], starts[c]) for c in range(nchunk)]
        o_parts.append(intras[hd] + jnp.concatenate(inter, axis=0))
    o = jnp.concatenate(o_parts, axis=1)

    o_main = _gla_output_gate(o, r, on_ref[...])
    xo_ref[0] = x + _dot(o_main.astype(BF16), wom_ref[...]) + _dot(o_mem.astype(BF16), wome_ref[...])

    @pl.when(li == pl.num_programs(1) - 1)
    def _():
        for hd in range(GLA_HEADS):
            st_ref[0, hd] = s_scr[hd][:GLA_DK, :GLA_DV]


def _mixer_a_prompt(x, gain, wa, mem_k, mem_v):
    b, l, d = x.shape
    tl = min(l, 256)
    assert l % tl == 0 and tl % GLA_CHUNK == 0
    m = mem_k.shape[2]
    return pl.pallas_call(
        _mixer_a_prompt_kernel,
        out_shape=(
            jax.ShapeDtypeStruct((b, l, d), F32),
            jax.ShapeDtypeStruct((b, GLA_HEADS, GLA_DK, GLA_DV), F32),
        ),
        grid=(b, l // tl),
        in_specs=[
            pl.BlockSpec((1, tl, d), lambda i, j: (i, j, 0)),
            _const_spec((1, d)),
            _const_spec((d, A_W)),
            _const_spec((LANES, GLA_HEADS * GLA_DK_PAD)),
            _const_spec((1, GLA_HEADS * GLA_DK_PAD)),
            _const_spec((1, GLA_HEADS * GLA_DV_PAD)),
            pl.BlockSpec((1, MEM_W, m), lambda i, j: (i, 0, 0)),
            pl.BlockSpec((1, MEM_W, m), lambda i, j: (i, 0, 0)),
            _const_spec((GLA_HEADS * GLA_DV_PAD, d)),
            _const_spec((MEM_W, d)),
        ],
        out_specs=(
            pl.BlockSpec((1, tl, d), lambda i, j: (i, j, 0)),
            pl.BlockSpec((1, GLA_HEADS, GLA_DK, GLA_DV), lambda i, j: (i, 0, 0, 0)),
        ),
        scratch_shapes=[pltpu.VMEM((GLA_HEADS, GLA_DK_PAD, GLA_DV_PAD), F32)],
        compiler_params=_params(("parallel", "arbitrary")),
        name="mixer_a_prompt",
    )(x, gain.reshape(1, d), wa["w_in"], wa["w_gate2"], wa["b_gate2"], wa["onorm"], mem_k, mem_v,
      wa["w_out_main"], wa["w_out_mem"])


def _columns(t):
    pad = jnp.zeros((LANES - t.shape[0], t.shape[1]), F32)
    return jnp.concatenate([t, pad], axis=0).T


def _mixer_a_sample_kernel(x_ref, g_ref, win_ref, wg2_ref, bg2_ref, on_ref, st_ref, mk_ref, mv_ref,
                           wom_ref, wome_ref, xo_ref, sto_ref, o_scr, om_scr):
    rows = x_ref.shape[0]
    x = x_ref[...]
    h = _rms(x, g_ref[...]).astype(BF16)
    q, k, v, logg, r, q_mem = _gla_inputs(h, win_ref, wg2_ref, bg2_ref)
    a_t = _columns(jnp.exp(logg))
    k_t = _columns(k)
    q_t = _columns(q * (GLA_DK ** -0.5))
    o_scr[...] = jnp.zeros(o_scr.shape, F32)
    for j in range(rows):
        for hd in range(GLA_HEADS):
            rs = slice(hd * GLA_DK_PAD, hd * GLA_DK_PAD + GLA_DK)
            vs = slice(hd * GLA_DV_PAD, hd * GLA_DV_PAD + GLA_DV)
            s_new = a_t[rs, j:j + 1] * st_ref[j, hd] + k_t[rs, j:j + 1] * v[j:j + 1, vs]
            sto_ref[j, hd] = s_new
            o_scr[j:j + 1, vs] = jnp.sum(q_t[rs, j:j + 1] * s_new, axis=0, keepdims=True)
    o_main = _gla_output_gate(o_scr[...], r, on_ref[...])
    _mem_attend_rows(q_mem, mk_ref, mv_ref, om_scr)
    xo_ref[...] = (x + _dot(o_main.astype(BF16), wom_ref[...])
                   + _dot(om_scr[...].astype(BF16), wome_ref[...]))


def _mixer_a_sample(x, gain, wa, state, mem_k, mem_v):
    n, d = x.shape
    rows = SUBLANES
    assert n % rows == 0
    m = mem_k.shape[2]
    return pl.pallas_call(
        _mixer_a_sample_kernel,
        out_shape=(
            jax.ShapeDtypeStruct((n, d), F32),
            jax.ShapeDtypeStruct(state.shape, F32),
        ),
        grid=(n // rows,),
        in_specs=[
            pl.BlockSpec((rows, d), lambda i: (i, 0)),
            _const_spec((1, d)),
            _const_spec((d, A_W)),
            _const_spec((LANES, GLA_HEADS * GLA_DK_PAD)),
            _const_spec((1, GLA_HEADS * GLA_DK_PAD)),
            _const_spec((1, GLA_HEADS * GLA_DV_PAD)),
            pl.BlockSpec((rows, GLA_HEADS, GLA_DK, GLA_DV), lambda i: (i, 0, 0, 0)),
            pl.BlockSpec((rows, MEM_W, m), lambda i: (i, 0, 0)),
            pl.BlockSpec((rows, MEM_W, m), lambda i: (i, 0, 0)),
            _const_spec((GLA_HEADS * GLA_DV_PAD, d)),
            _const_spec((MEM_W, d)),
        ],
        out_specs=(
            pl.BlockSpec((rows, d), lambda i: (i, 0)),
            pl.BlockSpec((rows, GLA_HEADS, GLA_DK, GLA_DV), lambda i: (i, 0, 0, 0)),
        ),
        scratch_shapes=[
            pltpu.VMEM((rows, GLA_HEADS * GLA_DV_PAD), F32),
            pltpu.VMEM((rows, MEM_W), F32),
        ],
        compiler_params=_params(("parallel",)),
        name="mixer_a_sample",
    )(x, gain.reshape(1, d), wa["w_in"], wa["w_gate2"], wa["b_gate2"], wa["onorm"], state, mem_k, mem_v,
      wa["w_out_main"], wa["w_out_mem"])


def _store_prompt_kv(h, w_ref, kt_ref, vh_ref, kb_ref, vt_ref):
    dqk = DIFF_HEADS * 2 * DIFF_DK
    k = _dot(h, w_ref[:, :dqk])
    v = _dot(h, w_ref[:, dqk:])
    kt_ref[0] = k.T
    for hd in range(DIFF_HEADS):
        vh_ref[0, hd] = v[:, hd * DIFF_DV:(hd + 1) * DIFF_DV]
    kb_ref[0] = k.astype(BF16)
    vt_ref[0] = v.T.astype(BF16)


def _prompt_kv_outputs(b, l, tm):
    dqk = DIFF_HEADS * 2 * DIFF_DK
    dv = DIFF_HEADS * DIFF_DV
    shapes = [jax.ShapeDtypeStruct((b, dqk, l), F32),
              jax.ShapeDtypeStruct((b, DIFF_HEADS, l, DIFF_DV), F32),
              jax.ShapeDtypeStruct((b, l, dqk), BF16),
              jax.ShapeDtypeStruct((b, dv, l), BF16)]
    specs = [pl.BlockSpec((1, dqk, tm), lambda i, j, *_: (i, 0, j)),
             pl.BlockSpec((1, DIFF_HEADS, tm, DIFF_DV), lambda i, j, *_: (i, 0, j, 0)),
             pl.BlockSpec((1, tm, dqk), lambda i, j, *_: (i, j, 0)),
             pl.BlockSpec((1, dv, tm), lambda i, j, *_: (i, 0, j))]
    return shapes, specs


def _kvproj_sample_kernel(x_ref, g_ref, w_ref, k_ref, v_ref):
    h = _rms(x_ref[...], g_ref[...]).astype(BF16)
    dqk = k_ref.shape[1]
    k_ref[...] = _dot(h, w_ref[:, :dqk])
    v_ref[...] = _dot(h, w_ref[:, dqk:])


def _kvproj_sample(x, gain, w):
    n, d = x.shape
    dqk = DIFF_HEADS * 2 * DIFF_DK
    dv = DIFF_HEADS * DIFF_DV
    return pl.pallas_call(
        _kvproj_sample_kernel,
        out_shape=(jax.ShapeDtypeStruct((n, dqk), F32), jax.ShapeDtypeStruct((n, dv), F32)),
        grid=(1,),
        in_specs=[pl.BlockSpec((n, d), lambda i: (0, 0)), _const_spec((1, d)), _const_spec((d, dqk + dv))],
        out_specs=(pl.BlockSpec((n, dqk), lambda i: (0, 0)), pl.BlockSpec((n, dv), lambda i: (0, 0))),
        compiler_params=_params(("arbitrary",)),
        name="kvproj_sample",
    )(x, gain.reshape(1, d), w)


def _projb_sample_kernel(x_ref, g_ref, w_ref, q_ref, qm_ref):
    h = _rms(x_ref[...], g_ref[...]).astype(BF16)
    dqk = q_ref.shape[1]
    q_ref[...] = (_dot(h, w_ref[0, :, :dqk]) * (DIFF_DK ** -0.5 * LOG2E)).astype(BF16)
    qm_ref[...] = _dot(h, w_ref[0, :, dqk:]).astype(BF16)


def _projb_sample(x, gain, w_in_b, layer):
    n, d = x.shape
    dqk = DIFF_HEADS * 2 * DIFF_DK
    return pl.pallas_call(
        _projb_sample_kernel,
        out_shape=(jax.ShapeDtypeStruct((n, dqk), BF16), jax.ShapeDtypeStruct((n, MEM_W), BF16)),
        grid=(1,),
        in_specs=[pl.BlockSpec((n, d), lambda i: (0, 0)), _const_spec((1, d)),
                  _layer_spec(w_in_b.shape[1:], layer)],
        out_specs=(pl.BlockSpec((n, dqk), lambda i: (0, 0)), pl.BlockSpec((n, MEM_W), lambda i: (0, 0))),
        compiler_params=_params(("arbitrary",)),
        name="projb_sample",
    )(x, gain.reshape(1, d), w_in_b)


def _lambda_full(lam_ref, lam_init):
    lv = lam_ref[...]
    s1 = jnp.sum(lv[0:1] * lv[1:2], axis=-1, keepdims=True)
    s2 = jnp.sum(lv[2:3] * lv[3:4], axis=-1, keepdims=True)
    return jnp.exp(s1) - jnp.exp(s2) + lam_init


ONES_ROWS = 16
DECODE_ROWS = 16


def _decode_init(qs_ref, sqm_scr, sm_scr, sl_scr, sacc_scr):
    nrow, width = sqm_scr.shape
    r = lax.broadcasted_iota(jnp.int32, (nrow, width), 0)
    lane_half = lax.broadcasted_iota(jnp.int32, (nrow, width), 1) >> 6
    qrow = jnp.broadcast_to(qs_ref[0].astype(F32), (nrow, width))
    sqm_scr[...] = jnp.where(r == lane_half, qrow, 0.0).astype(BF16)
    sm_scr[...] = jnp.full(sm_scr.shape, -jnp.inf, F32)
    sl_scr[...] = jnp.zeros(sl_scr.shape, F32)
    sacc_scr[...] = jnp.zeros(sacc_scr.shape, F32)


def _decode_side_work(kt_refs, v_refs, sqm_scr, sm_scr, sl_scr, sacc_scr, nslab):
    nrow = sacc_scr.shape[0]
    psize = kt_refs[0].shape[2]
    assert nslab >= 3
    box = {}

    def score_job():
        qm = sqm_scr[...]
        box["s"] = jnp.concatenate([_dot(qm, kr[0].astype(BF16)) for kr in kt_refs], axis=1)

    def softmax_job():
        m_old = sm_scr[...]
        m_new = jnp.maximum(m_old, jnp.max(box["s"], axis=-1, keepdims=True))
        box["pf"] = jnp.exp2(box["s"] - m_new)
        box["alpha"] = jnp.exp2(m_old - m_new)
        sl_scr[...] = box["alpha"] * sl_scr[...] + jnp.sum(box["pf"], axis=-1, keepdims=True)
        sm_scr[...] = m_new

    def value_job():
        vrows = DIFF_HEADS * psize
        row_head = lax.broadcasted_iota(jnp.int32, (nrow, vrows), 0) >> 1
        col_head = lax.broadcasted_iota(jnp.int32, (nrow, vrows), 1) >> (psize.bit_length() - 1)
        own = row_head == col_head
        pv = jnp.zeros(sacc_scr.shape, F32)
        for i, vr in enumerate(v_refs):
            pi = box["pf"][:, i * psize:(i + 1) * psize]
            spread = jnp.where(own, jnp.concatenate([pi] * DIFF_HEADS, axis=1), 0.0).astype(BF16)
            pv = pv + _dot(spread, vr[0].astype(BF16))
        sacc_scr[...] = box["alpha"] * sacc_scr[...] + pv

    idle = lambda: None
    return [score_job, softmax_job] + [idle] * (nslab - 3) + [value_job]


def _decode_finish(kn_ref, vn_ref, lam_ref, subrow_ref, os_ref, sqm_scr, sm_scr, sl_scr, sacc_scr, lam_init):
    nrow = sqm_scr.shape[0]
    s_self = jnp.sum(sqm_scr[...].astype(F32) * kn_ref[0], axis=-1, keepdims=True)
    m_old = sm_scr[...]
    m_fin = jnp.maximum(m_old, s_self)
    p_self = jnp.exp2(s_self - m_fin)
    a_fin = jnp.exp2(m_old - m_fin)
    l_fin = a_fin * sl_scr[...] + p_self
    vn = vn_ref[0]
    vn_rows = jnp.concatenate(
        [vn[:, (r // 2) * DIFF_DV:(r // 2 + 1) * DIFF_DV] for r in range(2 * DIFF_HEADS)]
        + [jnp.zeros((nrow - 2 * DIFF_HEADS, DIFF_DV), F32)], axis=0)
    acc = (a_fin * sacc_scr[...] + p_self * vn_rows) / l_fin
    lam = _lambda_full(lam_ref, lam_init)
    parts = []
    for hd in range(DIFF_HEADS):
        od = acc[2 * hd:2 * hd + 1] - lam * acc[2 * hd + 1:2 * hd + 2]
        ms = jnp.mean(od * od, axis=-1, keepdims=True)
        parts.append(od * lax.rsqrt(ms + EPS))
    on = (jnp.concatenate(parts, axis=1) * subrow_ref[...]) * (1.0 - lam_init)
    os_ref[0] = on.astype(BF16)


def _flash_kernel(qi_tab, ki_tab, q_ref, k_ref, vt_ref, lam_ref, sub_ref, o_ref,
                  qm_scr, m_scr, acc_scr, *, ratio, lam_init):
    step = pl.program_id(1)
    qi = qi_tab[step]
    ki = ki_tab[step]
    tq = q_ref.shape[1]
    tk = k_ref.shape[1]

    @pl.when(ki == 0)
    def _():
        for hd in range(DIFF_HEADS):
            qh = q_ref[0, :, hd * DIFF_DV:(hd + 1) * DIFF_DV].astype(F32)
            first = lax.broadcasted_iota(jnp.int32, qh.shape, 1) < DIFF_DK
            qm_scr[hd, :tq] = jnp.where(first, qh, 0.0).astype(BF16)
            qm_scr[hd, tq:] = jnp.where(first, 0.0, qh).astype(BF16)
        m_scr[...] = jnp.full(m_scr.shape, -jnp.inf, F32)
        acc_scr[...] = jnp.zeros(acc_scr.shape, F32)

    def block(masked):
        if masked:
            col = lax.broadcasted_iota(jnp.int32, (tk, 2 * tq), 1)
            kpos = ki * tk + lax.broadcasted_iota(jnp.int32, (tk, 2 * tq), 0)
            keep = kpos <= qi * tq + jnp.where(col >= tq, col - tq, col)
        ones = jnp.ones((ONES_ROWS, tk), BF16)

        def scores(hd):
            st = _dot_nt(k_ref[0, :, hd * DIFF_DV:(hd + 1) * DIFF_DV], qm_scr[hd])
            return jnp.where(keep, st, -jnp.inf) if masked else st

        st_next = scores(0)
        for hd in range(DIFF_HEADS):
            hs = slice(hd * DIFF_DV, (hd + 1) * DIFF_DV)
            st = st_next
            if hd + 1 < DIFF_HEADS:
                st_next = scores(hd + 1)
            m_old = m_scr[hd]
            m_new = jnp.maximum(m_old, jnp.max(st, axis=0, keepdims=True))
            p = jnp.exp2(st - m_new).astype(BF16)
            alpha = jnp.exp2(m_old - m_new)
            v_aug = jnp.concatenate([vt_ref[0, hs, :], ones], axis=0)
            acc_scr[hd] = alpha * acc_scr[hd] + _dot(v_aug, p)
            m_scr[hd] = m_new

    diag = ki >= qi * ratio

    @pl.when(diag)
    def _():
        block(True)

    @pl.when(jnp.logical_not(diag))
    def _():
        block(False)

    @pl.when(ki == (qi + 1) * ratio - 1)
    def _():
        lam = _lambda_full(lam_ref, lam_init)
        for hd in range(DIFF_HEADS):
            acc = acc_scr[hd]
            o = acc[:DIFF_DV] / acc[DIFF_DV:DIFF_DV + 1]
            od = o[:, :tq] - lam * o[:, tq:]
            ms = jnp.mean(od * od, axis=0, keepdims=True)
            on = (od * lax.rsqrt(ms + EPS) * sub_ref[...]) * (1.0 - lam_init)
            o_ref[0, :, hd * DIFF_DV:(hd + 1) * DIFF_DV] = on.T.astype(BF16)


def _flash_prompt(q, kb, vt, lam_vecs, subln, lam_init):
    b, l, dqk = q.shape
    dv = vt.shape[1]
    tq = min(l, 512)
    tk = min(l, 512)
    assert l % tq == 0 and tq % tk == 0
    ratio = tq // tk
    pairs = [(i, j) for i in range(l // tq) for j in range((i + 1) * ratio)]
    qi_tab = jnp.asarray([p[0] for p in pairs], jnp.int32)
    ki_tab = jnp.asarray([p[1] for p in pairs], jnp.int32)
    grid_spec = pltpu.PrefetchScalarGridSpec(
        num_scalar_prefetch=2,
        grid=(b, len(pairs)),
        in_specs=[
            pl.BlockSpec((1, tq, dqk), lambda i, s, qt, kt: (i, qt[s], 0)),
            pl.BlockSpec((1, tk, dqk), lambda i, s, qt, kt: (i, kt[s], 0)),
            pl.BlockSpec((1, dv, tk), lambda i, s, qt, kt: (i, 0, kt[s])),
            pl.BlockSpec((4, DIFF_DK), lambda i, s, qt, kt: (0, 0)),
            pl.BlockSpec((DIFF_DV, 1), lambda i, s, qt, kt: (0, 0)),
        ],
        out_specs=pl.BlockSpec((1, tq, dv), lambda i, s, qt, kt: (i, qt[s], 0)),
        scratch_shapes=[
            pltpu.VMEM((DIFF_HEADS, 2 * tq, DIFF_DV), BF16),
            pltpu.VMEM((DIFF_HEADS, 1, 2 * tq), F32),
            pltpu.VMEM((DIFF_HEADS, DIFF_DV + ONES_ROWS, 2 * tq), F32),
        ],
    )
    return pl.pallas_call(
        functools.partial(_flash_kernel, ratio=ratio, lam_init=lam_init),
        out_shape=jax.ShapeDtypeStruct((b, l, dv), BF16),
        grid_spec=grid_spec,
        compiler_params=_params(("parallel", "arbitrary")),
        name="flash_prompt",
    )(qi_tab, ki_tab, q, kb, vt, lam_vecs, subln.reshape(DIFF_DV, 1))


def _decode_jobs(q_s, k_new, v_new, cache_k, cache_v, page_table, lam_vecs, subln, lam_init, ncalls):
    n, dqk = q_s.shape
    dv = v_new.shape[1]
    n_phys, psize = cache_k.shape[:2]
    n_pages = page_table.shape[1]
    assert n % ncalls == 0 and psize & (psize - 1) == 0
    nseq = n // ncalls
    common = dict(
        q=q_s.reshape(n, 1, dqk), k_new=k_new.reshape(n, 1, dqk), v_new=v_new.reshape(n, 1, dv),
        ckt=jnp.transpose(cache_k, (0, 2, 3, 4, 1)).reshape(n_phys, dqk, psize),
        cvh=jnp.transpose(cache_v, (0, 2, 1, 3)).reshape(n_phys, DIFF_HEADS * psize, DIFF_DV),
        page_table=page_table.reshape(-1), lam_vecs=lam_vecs,
        subrow=jnp.tile(subln, DIFF_HEADS).reshape(1, dv), lam_init=lam_init,
        dqk=dqk, dv=dv, n_pages=n_pages, psize=psize, nseq=nseq)
    return [dict(common, seq0=c * nseq) for c in range(ncalls)]


def _post_sample_kernel(x_ref, om_ref, qm_ref, mk_ref, mv_ref, wo_ref, xo_ref, om_scr):
    _mem_attend_rows(qm_ref[...].astype(F32), mk_ref, mv_ref, om_scr)
    dm = om_ref.shape[1]
    xo_ref[...] = (x_ref[...] + _dot(om_ref[...], wo_ref[0, :dm, :])
                   + _dot(om_scr[...].astype(BF16), wo_ref[0, dm:, :]))


def _post_sample(x, o_main, q_mem, mem_k, mem_v, w_out_b, layer):
    n, d = x.shape
    rows = 2 * SUBLANES
    assert n % rows == 0
    dm = o_main.shape[1]
    m = mem_k.shape[2]
    return pl.pallas_call(
        _post_sample_kernel,
        out_shape=jax.ShapeDtypeStruct((n, d), F32),
        grid=(n // rows,),
        in_specs=[
            pl.BlockSpec((rows, d), lambda i: (i, 0)),
            pl.BlockSpec((rows, dm), lambda i: (i, 0)),
            pl.BlockSpec((rows, MEM_W), lambda i: (i, 0)),
            pl.BlockSpec((rows, MEM_W, m), lambda i: (i, 0, 0)),
            pl.BlockSpec((rows, MEM_W, m), lambda i: (i, 0, 0)),
            _layer_spec(w_out_b.shape[1:], layer),
        ],
        out_specs=pl.BlockSpec((rows, d), lambda i: (i, 0)),
        scratch_shapes=[pltpu.VMEM((rows, MEM_W), F32)],
        compiler_params=_params(("parallel",)),
        name="post_sample",
    )(x, o_main, q_mem, mem_k, mem_v, w_out_b)


def _pad_heads_cols(w, heads, width, padded):
    lead = w.shape[:-1]
    w = w.reshape(lead + (heads, width))
    w = jnp.pad(w, [(0, 0)] * len(lead) + [(0, 0), (0, padded - width)])
    return w.reshape(lead + (heads * padded,))


def _prep_layer_a(w_in, w_gate2, b_gate2, onorm, w_out):
    dqk = GLA_HEADS * GLA_DK
    dv = GLA_HEADS * GLA_DV
    o = 0
    wq = w_in[:, o:o + dqk]; o += dqk
    wk = w_in[:, o:o + dqk]; o += dqk
    wv = w_in[:, o:o + dv]; o += dv
    wg = w_in[:, o:o + GLA_GATE_RANK]; o += GLA_GATE_RANK
    wr = w_in[:, o:o + dv]; o += dv
    wm = w_in[:, o:o + MEM_W]
    w_all = jnp.concatenate([
        _pad_heads_cols(wq, GLA_HEADS, GLA_DK, GLA_DK_PAD),
        _pad_heads_cols(wk, GLA_HEADS, GLA_DK, GLA_DK_PAD),
        _pad_heads_cols(wv, GLA_HEADS, GLA_DV, GLA_DV_PAD),
        jnp.pad(wg, ((0, 0), (0, LANES - GLA_GATE_RANK))),
        _pad_heads_cols(wr, GLA_HEADS, GLA_DV, GLA_DV_PAD),
        wm,
    ], axis=1).astype(BF16)
    wg2 = _pad_heads_cols(w_gate2, GLA_HEADS, GLA_DK, GLA_DK_PAD)
    wg2 = jnp.pad(wg2, ((0, LANES - GLA_GATE_RANK), (0, 0))).astype(BF16)
    bg2 = _pad_heads_cols(b_gate2.reshape(1, dqk), GLA_HEADS, GLA_DK, GLA_DK_PAD)
    gain = _pad_heads_cols(jnp.tile(onorm, GLA_HEADS).reshape(1, dv), GLA_HEADS, GLA_DV, GLA_DV_PAD)
    wo_main = _pad_heads_cols(w_out[:dv].T, GLA_HEADS, GLA_DV, GLA_DV_PAD).T.astype(BF16)
    return {
        "w_in": w_all, "w_gate2": wg2, "b_gate2": bg2, "onorm": gain,
        "w_out_main": wo_main, "w_out_mem": w_out[dv:].astype(BF16),
    }


def kernel(x_prompt, x_sample, mem_prompt, state_gla, cache_k, cache_v, cache_mem_k, cache_mem_v,
           page_table, ffn1_norm, ffn1_w_gate, ffn1_w_up, ffn1_w_down, mix_norm, a_w_in, a_w_gate2,
           a_b_gate2, a_onorm, b_w_in, b_lambda_q1, b_lambda_k1, b_lambda_q2, b_lambda_k2, b_subln,
           mem_norm, w_mem_kv, w_out, ffn2_norm, ffn2_w_gate, ffn2_w_up, ffn2_w_down, kv_norm, w_kv,
           final_norm):
    bp, seq, d = x_prompt.shape
    db, dseq, _ = x_sample.shape
    assert dseq == 1
    depth = ffn1_norm.shape[0]
    n_a = a_w_in.shape[0]
    mtok = mem_prompt.shape[1]
    assert depth == n_a + 1
    lb = depth - 1
    bf = lambda t: t.astype(BF16)
    f1 = (bf(ffn1_w_gate), bf(ffn1_w_up), bf(ffn1_w_down))
    f2 = (bf(ffn2_w_gate), bf(ffn2_w_up), bf(ffn2_w_down))
    w_kv_b = bf(w_kv)
    w_mem_b = bf(w_mem_kv)
    w_out_b = bf(w_out)
    b_w_in_b = bf(b_w_in)

    mem_k_p, mem_v_p = _memkv(mem_prompt, mem_norm, w_mem_b)
    cmk = jnp.transpose(cache_mem_k, (0, 1, 3, 4, 2)).reshape(depth, db, MEM_W, mtok)
    cmv = jnp.transpose(cache_mem_v, (0, 1, 3, 4, 2)).reshape(depth, db, MEM_W, mtok)

    layer_a = [_prep_layer_a(a_w_in[i], a_w_gate2[i], a_b_gate2[i], a_onorm[i], w_out[i]) for i in range(n_a)]
    lam_init = 0.8 - 0.6 * math.exp(-0.3 * lb)
    lam_vecs = jnp.stack([b_lambda_q1[0], b_lambda_k1[0], b_lambda_q2[0], b_lambda_k2[0]])

    xs = x_sample.reshape(1, db, d)
    states_s = []
    for i in range(n_a):
        xs = _ffn(xs, ffn1_norm, *f1, i)
        xs2, st_s = _mixer_a_sample(xs[0], mix_norm[i], layer_a[i], state_gla[i], cmk[i], cmv[i])
        states_s.append(st_s)
        xs = _ffn(xs2[None], ffn2_norm, *f2, i)
    k_s, v_s = _kvproj_sample(xs[0], kv_norm, w_kv_b)
    xs = _ffn(xs, ffn1_norm, *f1, lb)
    q_s, qm_s = _projb_sample(xs[0], mix_norm[lb], b_w_in_b, 0)

    jobs = iter(_decode_jobs(q_s, k_s, v_s, cache_k, cache_v, page_table, lam_vecs, b_subln[0], lam_init,
                             ncalls=2 * depth))
    o_s_parts = []
    xp = x_prompt
    states_p = []
    for i in range(n_a):
        xp, part = _ffn(xp, ffn1_norm, *f1, i, decode=next(jobs))
        o_s_parts.append(part)
        xp, st_p = _mixer_a_prompt(xp, mix_norm[i], layer_a[i], mem_k_p[i], mem_v_p[i])
        states_p.append(st_p)
        if i == n_a - 1:
            xp, *kv_p, part = _ffn(xp, ffn2_norm, *f2, i, post="kv", post_args=(kv_norm, w_kv_b),
                                   decode=next(jobs))
        else:
            xp, part = _ffn(xp, ffn2_norm, *f2, i, decode=next(jobs))
        o_s_parts.append(part)
    xp, q_p, qm_p, part = _ffn(xp, ffn1_norm, *f1, lb, post="q", post_args=(mix_norm[lb], b_w_in_b, 0),
                               decode=next(jobs))
    o_s_parts.append(part)
    kt_p, vh_p, kb_p, vt_p = kv_p
    o_p = _flash_prompt(q_p, kb_p, vt_p, lam_vecs, b_subln[0], lam_init)
    xp, part = _ffn(xp, ffn2_norm, *f2, lb, pre="mix_out", pre_args=(o_p, qm_p, mem_k_p[lb], mem_v_p[lb], w_out_b),
                    post="final_norm", post_args=(final_norm,), decode=next(jobs))
    o_s_parts.append(part)

    o_s = jnp.concatenate(o_s_parts, axis=0).reshape(db, -1)
    xs = _post_sample(xs[0], o_s, qm_s, cmk[lb], cmv[lb], w_out_b, lb)[None]
    xs = _ffn(xs, ffn2_norm, *f2, lb, post="final_norm", post_args=(final_norm,))

    y_prompt = xp
    y_sample = xs.reshape(db, 1, d)

    def mem_rows(t):
        return jnp.transpose(t.reshape(depth, bp, MEM_HEADS, MEM_DH, mtok), (0, 1, 4, 2, 3))

    return (
        y_prompt,
        y_sample,
        jnp.stack(states_p),
        jnp.stack(states_s),
        jnp.transpose(kt_p.reshape(bp, DIFF_HEADS, 2, DIFF_DK, seq), (0, 4, 1, 2, 3)),
        jnp.transpose(vh_p, (0, 2, 1, 3)),
        k_s.reshape(db, 1, DIFF_HEADS, 2, DIFF_DK),
        v_s.reshape(db, 1, DIFF_HEADS, DIFF_DV),
        mem_rows(mem_k_p),
        mem_rows(mem_v_p),
    )
```

```python
import functools
import math

import jax
import jax.numpy as jnp
from jax import lax
from jax.experimental import pallas as pl
from jax.experimental.pallas import tpu as pltpu

F32 = jnp.float32
BF16 = jnp.bfloat16

EPS = 1e-6
GLA_HEADS = 4
GLA_DK = 96
GLA_DV = 192
GLA_GATE_RANK = 16
GLA_GATE_NORM = 16.0
DIFF_HEADS = 6
DIFF_DK = 64
DIFF_DV = 2 * DIFF_DK
MEM_HEADS = 4
MEM_DH = 64
MEM_W = MEM_HEADS * MEM_DH
LOG2E = math.log2(math.e)

LANES = 128
SUBLANES = 8
VMEM_LIMIT_BYTES = 56 * 1024 * 1024

GLA_DK_PAD = 128
GLA_DV_PAD = 256
GLA_CHUNK = 64

A_Q0 = 0
A_K0 = A_Q0 + GLA_HEADS * GLA_DK_PAD
A_V0 = A_K0 + GLA_HEADS * GLA_DK_PAD
A_G0 = A_V0 + GLA_HEADS * GLA_DV_PAD
A_R0 = A_G0 + LANES
A_M0 = A_R0 + GLA_HEADS * GLA_DV_PAD
A_W = A_M0 + MEM_W


def _dot(a, b):
    return jnp.dot(a, b, preferred_element_type=F32)


def _dot_nt(a, b):
    return lax.dot_general(a, b, (((1,), (1,)), ((), ())), preferred_element_type=F32)


def _rms(x, g):
    ms = jnp.mean(x * x, axis=-1, keepdims=True)
    return x * lax.rsqrt(ms + EPS) * g


def _const_spec(shape):
    nd = len(shape)
    return pl.BlockSpec(shape, lambda *_: (0,) * nd, pipeline_mode=pl.Buffered(1))


def _params(sem):
    return pltpu.CompilerParams(dimension_semantics=sem, vmem_limit_bytes=VMEM_LIMIT_BYTES)


FF_CHUNK = 256


def _layer_spec(shape, layer):
    nd = len(shape)
    return pl.BlockSpec((1,) + shape, lambda *_: (layer,) + (0,) * nd, pipeline_mode=pl.Buffered(1))


def _swiglu_half_step(x, gain, wg_ref, wu_ref, wd_ref, side_work=None):
    h = _rms(x, gain).astype(BF16)
    nchunk = wg_ref.shape[2] // FF_CHUNK

    def up(c):
        sl = slice(c * FF_CHUNK, (c + 1) * FF_CHUNK)
        return _dot(h, wg_ref[0, :, sl]), _dot(h, wu_ref[0, :, sl])

    acc = jnp.zeros(x.shape, F32)
    g, u = up(0)
    for c in range(nchunk):
        a = (g * jax.nn.sigmoid(g) * u).astype(BF16)
        if c + 1 < nchunk:
            g, u = up(c + 1)
        if side_work is not None:
            side_work[c]()
        acc = acc + _dot(a, wd_ref[0, c * FF_CHUNK:(c + 1) * FF_CHUNK, :])
    return x + 0.5 * acc


_N_POST_IN = {None: 0, "final_norm": 1, "kv": 2, "q": 2}
_N_POST_OUT = {None: 0, "final_norm": 0, "kv": 4, "q": 2}
_N_DECODE_IN = 5


def _ffn_kernel(*refs, pre, post, decode):
    refs = list(refs)
    if decode:
        refs.pop(0)
    n_in = 1 + (5 if pre else 0) + 4 + _N_POST_IN[post]
    ins, refs = refs[:n_in], refs[n_in:]
    if decode:
        pps, sps, lam_init = decode
        n_dec = _N_DECODE_IN + 2 * pps
        (qs_ref, kn_ref, vn_ref, lam_ref, subrow_ref), pages = refs[:_N_DECODE_IN], refs[_N_DECODE_IN:n_dec]
        refs = refs[n_dec:]
    n_out = 1 + _N_POST_OUT[post]
    outs, refs = refs[:n_out], refs[n_out:]
    if decode:
        os_ref, *decode_state = refs
        step = pl.program_id(0) * pl.num_programs(1) + pl.program_id(1)
        group = lax.rem(step, sps)

        @pl.when(group == 0)
        def _():
            _decode_init(qs_ref, *decode_state)

    x = ins[0][0]
    ins = ins[1:]
    if pre == "mix_out":
        om_ref, qm_ref, mk_ref, mv_ref, wo_ref = ins[:5]
        ins = ins[5:]
        dm = om_ref.shape[2]
        o_mem = _mem_attend_shared(qm_ref[0], mk_ref[0], mv_ref[0].astype(BF16))
        x = x + _dot(om_ref[0], wo_ref[0, :dm, :]) + _dot(o_mem.astype(BF16), wo_ref[0, dm:, :])
    g_ref, wg_ref, wu_ref, wd_ref = ins[:4]
    ins = ins[4:]
    side_work = None
    if decode:
        side_work = _decode_side_work(pages[:pps], pages[pps:], *decode_state, wg_ref.shape[2] // FF_CHUNK)
    y = _swiglu_half_step(x, g_ref[0], wg_ref, wu_ref, wd_ref, side_work)
    o_ref = outs[0]
    if post == "final_norm":
        o_ref[0] = _rms(y, ins[0][...])
    else:
        o_ref[0] = y
    if post in ("kv", "q"):
        pg_ref, pw_ref = ins
        h = _rms(y, pg_ref[...]).astype(BF16)
        if post == "kv":
            _store_prompt_kv(h, pw_ref, *outs[1:])
        else:
            q_ref, qm_out_ref = outs[1:]
            dqk = q_ref.shape[2]
            q_ref[0] = (_dot(h, pw_ref[0, :, :dqk]) * (DIFF_DK ** -0.5 * LOG2E)).astype(BF16)
            qm_out_ref[0] = _dot(h, pw_ref[0, :, dqk:]).astype(BF16)

    if decode:
        @pl.when(group == sps - 1)
        def _():
            _decode_finish(kn_ref, vn_ref, lam_ref, subrow_ref, os_ref, *decode_state, lam_init)


def _ffn(x, gains, wg, wu, wd, layer, pre=None, post=None, pre_args=(), post_args=(), decode=None):
    b, l, d = x.shape
    d_ff = wg.shape[2]
    tm = min(l, 256 if decode else 512)
    assert l % tm == 0 and d_ff % FF_CHUNK == 0
    depth = gains.shape[0]
    nlt = l // tm
    row = lambda width: pl.BlockSpec((1, tm, width), lambda i, j, *_: (i, j, 0))
    in_specs = [row(d)]
    args = [x]
    if pre == "mix_out":
        o_main, q_mem, mem_kt, mem_vt, w_out_b = pre_args
        m = mem_kt.shape[2]
        per_seq = pl.BlockSpec((1, MEM_W, m), lambda i, j, *_: (i, 0, 0))
        in_specs += [row(o_main.shape[2]), row(MEM_W), per_seq, per_seq, _layer_spec(w_out_b.shape[1:], layer)]
        args += [o_main, q_mem, mem_kt, mem_vt, w_out_b]
    in_specs += [_layer_spec((1, d), layer), _layer_spec((d, d_ff), layer),
                 _layer_spec((d, d_ff), layer), _layer_spec((d_ff, d), layer)]
    args += [gains.reshape(depth, 1, d), wg, wu, wd]
    out_shape = [jax.ShapeDtypeStruct((b, l, d), F32)]
    out_specs = [row(d)]
    if post == "final_norm":
        (final_gain,) = post_args
        in_specs.append(_const_spec((1, d)))
        args.append(final_gain.reshape(1, d))
    elif post == "kv":
        kv_gain, w_kv_b = post_args
        in_specs += [_const_spec((1, d)), _const_spec(w_kv_b.shape)]
        args += [kv_gain.reshape(1, d), w_kv_b]
        kv_shapes, kv_specs = _prompt_kv_outputs(b, l, tm)
        out_shape += kv_shapes
        out_specs += kv_specs
    elif post == "q":
        q_gain, w_in_b, q_layer = post_args
        dqk = DIFF_HEADS * 2 * DIFF_DK
        in_specs += [_const_spec((1, d)), _layer_spec(w_in_b.shape[1:], q_layer)]
        args += [q_gain.reshape(1, d), w_in_b]
        out_shape += [jax.ShapeDtypeStruct((b, l, dqk), BF16), jax.ShapeDtypeStruct((b, l, MEM_W), BF16)]
        out_specs += [row(dqk), row(MEM_W)]
    name = "ffn" + ("_" + pre if pre else "") + ("_" + post if post else "")
    kernel_kwargs = dict(pre=pre, post=post, decode=None)
    scratch, prefetch = [], []
    if decode:
        job = decode
        dqk, dv, n_pages, psize = job["dqk"], job["dv"], job["n_pages"], job["psize"]
        seq0, nseq = job["seq0"], job["nseq"]
        nsteps = b * nlt
        assert nsteps % nseq == 0, "each sequence's page sweep must take a whole number of grid steps"
        sps = nsteps // nseq
        assert n_pages % sps == 0
        pps = n_pages // sps

        def seq_local(i, j):
            return (i * nlt + j) // sps

        def page_spec(k, shape):
            def index(i, j, pt):
                group = (i * nlt + j) % sps
                return (pt[(seq0 + seq_local(i, j)) * n_pages + group * pps + k], 0, 0)
            return pl.BlockSpec((1,) + shape, index)

        seq_spec = pl.BlockSpec((1, 1, dqk), lambda i, j, pt: (seq0 + seq_local(i, j), 0, 0))
        in_specs += [seq_spec, seq_spec, seq_spec,
                     pl.BlockSpec((4, DIFF_DK), lambda i, j, pt: (0, 0)),
                     pl.BlockSpec((1, dv), lambda i, j, pt: (0, 0))]
        in_specs += [page_spec(k, (dqk, psize)) for k in range(pps)]
        in_specs += [page_spec(k, (DIFF_HEADS * psize, DIFF_DV)) for k in range(pps)]
        args += [job["q"], job["k_new"], job["v_new"], job["lam_vecs"], job["subrow"]]
        args += [job["ckt"]] * pps + [job["cvh"]] * pps
        out_shape.append(jax.ShapeDtypeStruct((nseq, 1, dv), BF16))
        out_specs.append(pl.BlockSpec((1, 1, dv), lambda i, j, pt: (seq_local(i, j), 0, 0)))
        scratch = [pltpu.VMEM((DECODE_ROWS, dqk), BF16), pltpu.VMEM((DECODE_ROWS, 1), F32),
                   pltpu.VMEM((DECODE_ROWS, 1), F32), pltpu.VMEM((DECODE_ROWS, DIFF_DV), F32)]
        prefetch = [job["page_table"]]
        kernel_kwargs["decode"] = (pps, sps, job["lam_init"])
        name += "_decode"
    out = pl.pallas_call(
        functools.partial(_ffn_kernel, **kernel_kwargs),
        out_shape=tuple(out_shape),
        grid_spec=pltpu.PrefetchScalarGridSpec(
            num_scalar_prefetch=len(prefetch), grid=(b, nlt), in_specs=in_specs,
            out_specs=tuple(out_specs), scratch_shapes=scratch),
        compiler_params=_params(("arbitrary", "arbitrary") if decode else ("parallel", "parallel")),
        name=name,
    )(*prefetch, *args)
    return out[0] if len(out) == 1 else out


def _memkv_kernel(m_ref, g_ref, w_ref, kt_ref, vt_ref):
    h = _rms(m_ref[0], g_ref[0]).astype(BF16)
    kv = _dot(h, w_ref[0])
    kt_ref[0, 0] = kv[:, :MEM_W].T
    vt_ref[0, 0] = kv[:, MEM_W:].T


def _memkv(mem, gains, w):
    depth = w.shape[0]
    b, m, d = mem.shape
    out = jax.ShapeDtypeStruct((depth, b, MEM_W, m), F32)
    return pl.pallas_call(
        _memkv_kernel,
        out_shape=(out, out),
        grid=(depth, b),
        in_specs=[
            pl.BlockSpec((1, m, d), lambda i, j: (j, 0, 0)),
            pl.BlockSpec((1, 1, d), lambda i, j: (i, 0, 0)),
            pl.BlockSpec((1, d, 2 * MEM_W), lambda i, j: (i, 0, 0)),
        ],
        out_specs=(
            pl.BlockSpec((1, 1, MEM_W, m), lambda i, j: (i, j, 0, 0)),
            pl.BlockSpec((1, 1, MEM_W, m), lambda i, j: (i, j, 0, 0)),
        ),
        compiler_params=_params(("arbitrary", "arbitrary")),
        name="memkv",
    )(mem, gains.reshape(depth, 1, d), w)


def _log_sigmoid(z):
    return jnp.minimum(z, 0.0) - jnp.log1p(jnp.exp(-jnp.abs(z)))


def _gla_recurrence_inputs(h, win_ref, wg2_ref, bg2_ref):
    q = _dot(h, win_ref[:, A_Q0:A_K0])
    k = _dot(h, win_ref[:, A_K0:A_V0])
    v = _dot(h, win_ref[:, A_V0:A_G0])
    g_lr = _dot(h, win_ref[:, A_G0:A_R0])
    z = _dot(g_lr.astype(BF16), wg2_ref[...]) + bg2_ref[...]
    logg = _log_sigmoid(z) / GLA_GATE_NORM
    return q, k, v, logg


def _gla_inputs(h, win_ref, wg2_ref, bg2_ref):
    q, k, v, logg = _gla_recurrence_inputs(h, win_ref, wg2_ref, bg2_ref)
    r = _dot(h, win_ref[:, A_R0:A_M0])
    q_mem = _dot(h, win_ref[:, A_M0:A_W])
    return q, k, v, logg, r, q_mem


def _gla_output_gate(o, r, gain):
    parts = []
    for h in range(GLA_HEADS):
        sl = slice(h * GLA_DV_PAD, (h + 1) * GLA_DV_PAD)
        oh = o[:, sl]
        ms = jnp.sum(oh * oh, axis=-1, keepdims=True) * (1.0 / GLA_DV)
        parts.append(oh * lax.rsqrt(ms + EPS) * gain[:, sl])
    on = jnp.concatenate(parts, axis=1)
    return on * (r * jax.nn.sigmoid(r))


def _mem_attend_shared(qm, mkt, mvt):
    row_head = lax.broadcasted_iota(jnp.int32, mkt.shape, 0) >> 6
    out_head = lax.broadcasted_iota(jnp.int32, (qm.shape[0], MEM_W), 1) >> 6
    scores = [_dot(qm, jnp.where(row_head == h, mkt, 0.0).astype(BF16)) * (MEM_DH ** -0.5)
              for h in range(MEM_HEADS)]
    o = jnp.zeros((qm.shape[0], MEM_W), F32)
    for h in range(MEM_HEADS):
        e = jnp.exp(scores[h] - jnp.max(scores[h], axis=-1, keepdims=True))
        l = jnp.sum(e, axis=-1, keepdims=True)
        oh = _dot_nt(e.astype(BF16), mvt) / l
        o = o + jnp.where(out_head == h, oh, 0.0)
    return o


def _mem_attend_rows(qmem, mkt_ref, mvt_ref, om_scr):
    rows = qmem.shape[0]
    r8 = lax.broadcasted_iota(jnp.int32, (SUBLANES, MEM_W), 0)
    lh = lax.broadcasted_iota(jnp.int32, (SUBLANES, MEM_W), 1) >> 6
    for j in range(rows):
        qrow = jnp.broadcast_to(qmem[j:j + 1], (SUBLANES, MEM_W))
        qm = jnp.where(r8 == lh, qrow, 0.0).astype(BF16)
        s = _dot(qm, mkt_ref[j].astype(BF16)) * (MEM_DH ** -0.5)
        e = jnp.exp(s - jnp.max(s, axis=-1, keepdims=True))
        l = jnp.sum(e, axis=-1, keepdims=True)
        oh = _dot_nt(e.astype(BF16), mvt_ref[j].astype(BF16)) / l
        om_scr[j:j + 1, :] = jnp.sum(jnp.where(r8 == lh, oh, 0.0), axis=0, keepdims=True)


def _mixer_a_prompt_kernel(x_ref, g_ref, win_ref, wg2_ref, bg2_ref, on_ref, mk_ref, mv_ref,
                           wom_ref, wome_ref, xo_ref, st_ref, s_scr):
    tl = x_ref.shape[1]
    li = pl.program_id(1)

    @pl.when(li == 0)
    def _():
        s_scr[...] = jnp.zeros(s_scr.shape, F32)

    x = x_ref[0]
    h = _rms(x, g_ref[...]).astype(BF16)
    q, k, v, logg = _gla_recurrence_inputs(h, win_ref, wg2_ref, bg2_ref)
    r = _dot(h, win_ref[:, A_R0:A_M0])
    q_mem = _dot(h, win_ref[:, A_M0:A_W])

    row = lax.broadcasted_iota(jnp.int32, (tl, tl), 0)
    col = lax.broadcasted_iota(jnp.int32, (tl, tl), 1)
    tril = ((row >> 6) == (col >> 6)) & (col <= row)
    trilb = jnp.where(tril, 1.0, 0.0).astype(BF16)
    hi = logg.astype(BF16)
    lo = (logg - hi.astype(F32)).astype(BF16)
    b = _dot(trilb, hi) + _dot(trilb, lo)

    o_mem = _mem_attend_shared(q_mem.astype(BF16), mk_ref[0], mv_ref[0].astype(BF16))

    lane_chunk = lax.broadcasted_iota(jnp.int32, (GLA_DK_PAD, tl), 1) >> 6
    scale = GLA_DK ** -0.5
    nchunk = tl // GLA_CHUNK
    qdbs, intras, decs, incs = [], [], [], []
    for hd in range(GLA_HEADS):
        ks = slice(hd * GLA_DK_PAD, (hd + 1) * GLA_DK_PAD)
        vs = slice(hd * GLA_DV_PAD, (hd + 1) * GLA_DV_PAD)
        bh = b[:, ks]
        qdb = ((q[:, ks] * scale) * jnp.exp(bh)).astype(BF16)
        ki = k[:, ks] * jnp.exp(-bh)
        att = jnp.where(tril, _dot_nt(qdb, ki.astype(BF16)), 0.0)
        vh = v[:, vs].astype(BF16)
        intras.append(_dot(att.astype(BF16), vh))
        ki_t = ki.T
        b_t = bh.T
        dec_h, inc_h = [], []
        for c in range(nchunk):
            last = c * GLA_CHUNK + GLA_CHUNK - 1
            dec = jnp.exp(b_t[:, last:last + 1])
            kend_t = jnp.where(lane_chunk == c, ki_t * dec, 0.0).astype(BF16)
            dec_h.append(dec)
            inc_h.append(_dot(kend_t, vh))
        qdbs.append(qdb)
        decs.append(dec_h)
        incs.append(inc_h)

    o_parts = []
    for hd in range(GLA_HEADS):
        state = s_scr[hd]
        starts = []
        for c in range(nchunk):
            starts.append(state.astype(BF16))
            state = decs[hd][c] * state + incs[hd][c]
        s_scr[hd] = state
        inter = [_dot(qdbs[hd][c * GLA_CHUNK:(c + 1) * GLA_CHUNK], starts[c]) for c in range(nchunk)]
        o_parts.append(intras[hd] + jnp.concatenate(inter, axis=0))
    o = jnp.concatenate(o_parts, axis=1)

    o_main = _gla_output_gate(o, r, on_ref[...])
    xo_ref[0] = x + _dot(o_main.astype(BF16), wom_ref[...]) + _dot(o_mem.astype(BF16), wome_ref[...])

    @pl.when(li == pl.num_programs(1) - 1)
    def _():
        for hd in range(GLA_HEADS):
            st_ref[0, hd] = s_scr[hd][:GLA_DK, :GLA_DV]


def _mixer_a_prompt(x, gain, wa, mem_k, mem_v):
    b, l, d = x.shape
    tl = min(l, 256)
    assert l % tl == 0 and tl % GLA_CHUNK == 0
    m = mem_k.shape[2]
    return pl.pallas_call(
        _mixer_a_prompt_kernel,
        out_shape=(
            jax.ShapeDtypeStruct((b, l, d), F32),
            jax.ShapeDtypeStruct((b, GLA_HEADS, GLA_DK, GLA_DV), F32),
        ),
        grid=(b, l // tl),
        in_specs=[
            pl.BlockSpec((1, tl, d), lambda i, j: (i, j, 0)),
            _const_spec((1, d)),
            _const_spec((d, A_W)),
            _const_spec((LANES, GLA_HEADS * GLA_DK_PAD)),
            _const_spec((1, GLA_HEADS * GLA_DK_PAD)),
            _const_spec((1, GLA_HEADS * GLA_DV_PAD)),
            pl.BlockSpec((1, MEM_W, m), lambda i, j: (i, 0, 0)),
            pl.BlockSpec((1, MEM_W, m), lambda i, j: (i, 0, 0)),
            _const_spec((GLA_HEADS * GLA_DV_PAD, d)),
            _const_spec((MEM_W, d)),
        ],
        out_specs=(
            pl.BlockSpec((1, tl, d), lambda i, j: (i, j, 0)),
            pl.BlockSpec((1, GLA_HEADS, GLA_DK, GLA_DV), lambda i, j: (i, 0, 0, 0)),
        ),
        scratch_shapes=[pltpu.VMEM((GLA_HEADS, GLA_DK_PAD, GLA_DV_PAD), F32)],
        compiler_params=_params(("parallel", "arbitrary")),
        name="mixer_a_prompt",
    )(x, gain.reshape(1, d), wa["w_in"], wa["w_gate2"], wa["b_gate2"], wa["onorm"], mem_k, mem_v,
      wa["w_out_main"], wa["w_out_mem"])


def _columns(t):
    pad = jnp.zeros((LANES - t.shape[0], t.shape[1]), F32)
    return jnp.concatenate([t, pad], axis=0).T


def _mixer_a_sample_kernel(x_ref, g_ref, win_ref, wg2_ref, bg2_ref, on_ref, st_ref, mk_ref, mv_ref,
                           wom_ref, wome_ref, xo_ref, sto_ref, o_scr, om_scr):
    rows = x_ref.shape[0]
    x = x_ref[...]
    h = _rms(x, g_ref[...]).astype(BF16)
    q, k, v, logg, r, q_mem = _gla_inputs(h, win_ref, wg2_ref, bg2_ref)
    a_t = _columns(jnp.exp(logg))
    k_t = _columns(k)
    q_t = _columns(q * (GLA_DK ** -0.5))
    o_scr[...] = jnp.zeros(o_scr.shape, F32)
    for j in range(rows):
        for hd in range(GLA_HEADS):
            rs = slice(hd * GLA_DK_PAD, hd * GLA_DK_PAD + GLA_DK)
            vs = slice(hd * GLA_DV_PAD, hd * GLA_DV_PAD + GLA_DV)
            s_new = a_t[rs, j:j + 1] * st_ref[j, hd] + k_t[rs, j:j + 1] * v[j:j + 1, vs]
            sto_ref[j, hd] = s_new
            o_scr[j:j + 1, vs] = jnp.sum(q_t[rs, j:j + 1] * s_new, axis=0, keepdims=True)
    o_main = _gla_output_gate(o_scr[...], r, on_ref[...])
    _mem_attend_rows(q_mem, mk_ref, mv_ref, om_scr)
    xo_ref[...] = (x + _dot(o_main.astype(BF16), wom_ref[...])
                   + _dot(om_scr[...].astype(BF16), wome_ref[...]))


def _mixer_a_sample(x, gain, wa, state, mem_k, mem_v):
    n, d = x.shape
    rows = SUBLANES
    assert n % rows == 0
    m = mem_k.shape[2]
    return pl.pallas_call(
        _mixer_a_sample_kernel,
        out_shape=(
            jax.ShapeDtypeStruct((n, d), F32),
            jax.ShapeDtypeStruct(state.shape, F32),
        ),
        grid=(n // rows,),
        in_specs=[
            pl.BlockSpec((rows, d), lambda i: (i, 0)),
            _const_spec((1, d)),
            _const_spec((d, A_W)),
            _const_spec((LANES, GLA_HEADS * GLA_DK_PAD)),
            _const_spec((1, GLA_HEADS * GLA_DK_PAD)),
            _const_spec((1, GLA_HEADS * GLA_DV_PAD)),
            pl.BlockSpec((rows, GLA_HEADS, GLA_DK, GLA_DV), lambda i: (i, 0, 0, 0)),
            pl.BlockSpec((rows, MEM_W, m), lambda i: (i, 0, 0)),
            pl.BlockSpec((rows, MEM_W, m), lambda i: (i, 0, 0)),
            _const_spec((GLA_HEADS * GLA_DV_PAD, d)),
            _const_spec((MEM_W, d)),
        ],
        out_specs=(
            pl.BlockSpec((rows, d), lambda i: (i, 0)),
            pl.BlockSpec((rows, GLA_HEADS, GLA_DK, GLA_DV), lambda i: (i, 0, 0, 0)),
        ),
        scratch_shapes=[
            pltpu.VMEM((rows, GLA_HEADS * GLA_DV_PAD), F32),
            pltpu.VMEM((rows, MEM_W), F32),
        ],
        compiler_params=_params(("parallel",)),
        name="mixer_a_sample",
    )(x, gain.reshape(1, d), wa["w_in"], wa["w_gate2"], wa["b_gate2"], wa["onorm"], state, mem_k, mem_v,
      wa["w_out_main"], wa["w_out_mem"])


def _store_prompt_kv(h, w_ref, kt_ref, vh_ref, kb_ref, vt_ref):
    dqk = DIFF_HEADS * 2 * DIFF_DK
    k = _dot(h, w_ref[:, :dqk])
    v = _dot(h, w_ref[:, dqk:])
    kt_ref[0] = k.T
    for hd in range(DIFF_HEADS):
        vh_ref[0, hd] = v[:, hd * DIFF_DV:(hd + 1) * DIFF_DV]
    kb_ref[0] = k.astype(BF16)
    vt_ref[0] = v.T.astype(BF16)


def _prompt_kv_outputs(b, l, tm):
    dqk = DIFF_HEADS * 2 * DIFF_DK
    dv = DIFF_HEADS * DIFF_DV
    shapes = [jax.ShapeDtypeStruct((b, dqk, l), F32),
              jax.ShapeDtypeStruct((b, DIFF_HEADS, l, DIFF_DV), F32),
              jax.ShapeDtypeStruct((b, l, dqk), BF16),
              jax.ShapeDtypeStruct((b, dv, l), BF16)]
    specs = [pl.BlockSpec((1, dqk, tm), lambda i, j, *_: (i, 0, j)),
             pl.BlockSpec((1, DIFF_HEADS, tm, DIFF_DV), lambda i, j, *_: (i, 0, j, 0)),
             pl.BlockSpec((1, tm, dqk), lambda i, j, *_: (i, j, 0)),
             pl.BlockSpec((1, dv, tm), lambda i, j, *_: (i, 0, j))]
    return shapes, specs


def _kvproj_sample_kernel(x_ref, g_ref, w_ref, k_ref, v_ref):
    h = _rms(x_ref[...], g_ref[...]).astype(BF16)
    dqk = k_ref.shape[1]
    k_ref[...] = _dot(h, w_ref[:, :dqk])
    v_ref[...] = _dot(h, w_ref[:, dqk:])


def _kvproj_sample(x, gain, w):
    n, d = x.shape
    dqk = DIFF_HEADS * 2 * DIFF_DK
    dv = DIFF_HEADS * DIFF_DV
    return pl.pallas_call(
        _kvproj_sample_kernel,
        out_shape=(jax.ShapeDtypeStruct((n, dqk), F32), jax.ShapeDtypeStruct((n, dv), F32)),
        grid=(1,),
        in_specs=[pl.BlockSpec((n, d), lambda i: (0, 0)), _const_spec((1, d)), _const_spec((d, dqk + dv))],
        out_specs=(pl.BlockSpec((n, dqk), lambda i: (0, 0)), pl.BlockSpec((n, dv), lambda i: (0, 0))),
        compiler_params=_params(("arbitrary",)),
        name="kvproj_sample",
    )(x, gain.reshape(1, d), w)


def _projb_sample_kernel(x_ref, g_ref, w_ref, q_ref, qm_ref):
    h = _rms(x_ref[...], g_ref[...]).astype(BF16)
    dqk = q_ref.shape[1]
    q_ref[...] = (_dot(h, w_ref[0, :, :dqk]) * (DIFF_DK ** -0.5 * LOG2E)).astype(BF16)
    qm_ref[...] = _dot(h, w_ref[0, :, dqk:]).astype(BF16)


def _projb_sample(x, gain, w_in_b, layer):
    n, d = x.shape
    dqk = DIFF_HEADS * 2 * DIFF_DK
    return pl.pallas_call(
        _projb_sample_kernel,
        out_shape=(jax.ShapeDtypeStruct((n, dqk), BF16), jax.ShapeDtypeStruct((n, MEM_W), BF16)),
        grid=(1,),
        in_specs=[pl.BlockSpec((n, d), lambda i: (0, 0)), _const_spec((1, d)),
                  _layer_spec(w_in_b.shape[1:], layer)],
        out_specs=(pl.BlockSpec((n, dqk), lambda i: (0, 0)), pl.BlockSpec((n, MEM_W), lambda i: (0, 0))),
        compiler_params=_params(("arbitrary",)),
        name="projb_sample",
    )(x, gain.reshape(1, d), w_in_b)


def _lambda_full(lam_ref, lam_init):
    lv = lam_ref[...]
    s1 = jnp.sum(lv[0:1] * lv[1:2], axis=-1, keepdims=True)
    s2 = jnp.sum(lv[2:3] * lv[3:4], axis=-1, keepdims=True)
    return jnp.exp(s1) - jnp.exp(s2) + lam_init


ONES_ROWS = 16
DECODE_ROWS = 16
SCORE_LOOKAHEAD = 2


def _decode_init(qs_ref, sqm_scr, sm_scr, sl_scr, sacc_scr):
    nrow, width = sqm_scr.shape
    r = lax.broadcasted_iota(jnp.int32, (nrow, width), 0)
    lane_half = lax.broadcasted_iota(jnp.int32, (nrow, width), 1) >> 6
    qrow = jnp.broadcast_to(qs_ref[0].astype(F32), (nrow, width))
    sqm_scr[...] = jnp.where(r == lane_half, qrow, 0.0).astype(BF16)
    sm_scr[...] = jnp.full(sm_scr.shape, -jnp.inf, F32)
    sl_scr[...] = jnp.zeros(sl_scr.shape, F32)
    sacc_scr[...] = jnp.zeros(sacc_scr.shape, F32)


def _decode_side_work(kt_refs, v_refs, sqm_scr, sm_scr, sl_scr, sacc_scr, nslab):
    nrow = sacc_scr.shape[0]
    psize = kt_refs[0].shape[2]
    assert nslab >= 3
    box = {}

    def score_job():
        qm = sqm_scr[...]
        box["s"] = jnp.concatenate([_dot(qm, kr[0].astype(BF16)) for kr in kt_refs], axis=1)

    def softmax_job():
        m_old = sm_scr[...]
        m_new = jnp.maximum(m_old, jnp.max(box["s"], axis=-1, keepdims=True))
        box["pf"] = jnp.exp2(box["s"] - m_new)
        box["alpha"] = jnp.exp2(m_old - m_new)
        sl_scr[...] = box["alpha"] * sl_scr[...] + jnp.sum(box["pf"], axis=-1, keepdims=True)
        sm_scr[...] = m_new

    def value_job():
        vrows = DIFF_HEADS * psize
        row_head = lax.broadcasted_iota(jnp.int32, (nrow, vrows), 0) >> 1
        col_head = lax.broadcasted_iota(jnp.int32, (nrow, vrows), 1) >> (psize.bit_length() - 1)
        own = row_head == col_head
        pv = jnp.zeros(sacc_scr.shape, F32)
        for i, vr in enumerate(v_refs):
            pi = box["pf"][:, i * psize:(i + 1) * psize]
            spread = jnp.where(own, jnp.concatenate([pi] * DIFF_HEADS, axis=1), 0.0).astype(BF16)
            pv = pv + _dot(spread, vr[0].astype(BF16))
        sacc_scr[...] = box["alpha"] * sacc_scr[...] + pv

    idle = lambda: None
    return [score_job, softmax_job] + [idle] * (nslab - 3) + [value_job]


def _decode_finish(kn_ref, vn_ref, lam_ref, subrow_ref, os_ref, sqm_scr, sm_scr, sl_scr, sacc_scr, lam_init):
    nrow = sqm_scr.shape[0]
    s_self = jnp.sum(sqm_scr[...].astype(F32) * kn_ref[0], axis=-1, keepdims=True)
    m_old = sm_scr[...]
    m_fin = jnp.maximum(m_old, s_self)
    p_self = jnp.exp2(s_self - m_fin)
    a_fin = jnp.exp2(m_old - m_fin)
    l_fin = a_fin * sl_scr[...] + p_self
    vn = vn_ref[0]
    vn_rows = jnp.concatenate(
        [vn[:, (r // 2) * DIFF_DV:(r // 2 + 1) * DIFF_DV] for r in range(2 * DIFF_HEADS)]
        + [jnp.zeros((nrow - 2 * DIFF_HEADS, DIFF_DV), F32)], axis=0)
    acc = (a_fin * sacc_scr[...] + p_self * vn_rows) / l_fin
    lam = _lambda_full(lam_ref, lam_init)
    parts = []
    for hd in range(DIFF_HEADS):
        od = acc[2 * hd:2 * hd + 1] - lam * acc[2 * hd + 1:2 * hd + 2]
        ms = jnp.mean(od * od, axis=-1, keepdims=True)
        parts.append(od * lax.rsqrt(ms + EPS))
    on = (jnp.concatenate(parts, axis=1) * subrow_ref[...]) * (1.0 - lam_init)
    os_ref[0] = on.astype(BF16)


def _flash_kernel(qi_tab, ki_tab, q_ref, k_ref, vt_ref, lam_ref, sub_ref, o_ref,
                  qm_scr, m_scr, acc_scr, *, ratio, lam_init):
    step = pl.program_id(1)
    qi = qi_tab[step]
    ki = ki_tab[step]
    tq = q_ref.shape[1]
    tk = k_ref.shape[1]

    @pl.when(ki == 0)
    def _():
        for hd in range(DIFF_HEADS):
            qh = q_ref[0, :, hd * DIFF_DV:(hd + 1) * DIFF_DV].astype(F32)
            first = lax.broadcasted_iota(jnp.int32, qh.shape, 1) < DIFF_DK
            qm_scr[hd, :tq] = jnp.where(first, qh, 0.0).astype(BF16)
            qm_scr[hd, tq:] = jnp.where(first, 0.0, qh).astype(BF16)
        m_scr[...] = jnp.full(m_scr.shape, -jnp.inf, F32)
        acc_scr[...] = jnp.zeros(acc_scr.shape, F32)

    def block(masked):
        if masked:
            col = lax.broadcasted_iota(jnp.int32, (tk, 2 * tq), 1)
            kpos = ki * tk + lax.broadcasted_iota(jnp.int32, (tk, 2 * tq), 0)
            keep = kpos <= qi * tq + jnp.where(col >= tq, col - tq, col)
            bias = jnp.where(keep, 0.0, -jnp.inf)
        ones = jnp.ones((ONES_ROWS, tk), BF16)

        def scores(hd):
            st = _dot_nt(k_ref[0, :, hd * DIFF_DV:(hd + 1) * DIFF_DV], qm_scr[hd])
            return st + bias if masked else st

        pending = [scores(hd) for hd in range(SCORE_LOOKAHEAD)]
        for hd in range(DIFF_HEADS):
            hs = slice(hd * DIFF_DV, (hd + 1) * DIFF_DV)
            st = pending.pop(0)
            if hd + SCORE_LOOKAHEAD < DIFF_HEADS:
                pending.append(scores(hd + SCORE_LOOKAHEAD))
            m_old = m_scr[hd]
            m_new = jnp.maximum(m_old, jnp.max(st, axis=0, keepdims=True))
            p = jnp.exp2(st - m_new).astype(BF16)
            alpha = jnp.exp2(m_old - m_new)
            v_aug = jnp.concatenate([vt_ref[0, hs, :], ones], axis=0)
            acc_scr[hd] = alpha * acc_scr[hd] + _dot(v_aug, p)
            m_scr[hd] = m_new

    diag = ki >= qi * ratio

    @pl.when(diag)
    def _():
        block(True)

    @pl.when(jnp.logical_not(diag))
    def _():
        block(False)

    @pl.when(ki == (qi + 1) * ratio - 1)
    def _():
        lam = _lambda_full(lam_ref, lam_init)
        for hd in range(DIFF_HEADS):
            acc = acc_scr[hd]
            o = acc[:DIFF_DV] / acc[DIFF_DV:DIFF_DV + 1]
            od = o[:, :tq] - lam * o[:, tq:]
            ms = jnp.mean(od * od, axis=0, keepdims=True)
            on = (od * lax.rsqrt(ms + EPS) * sub_ref[...]) * (1.0 - lam_init)
            o_ref[0, :, hd * DIFF_DV:(hd + 1) * DIFF_DV] = on.T.astype(BF16)


def _flash_prompt(q, kb, vt, lam_vecs, subln, lam_init):
    b, l, dqk = q.shape
    dv = vt.shape[1]
    tq = min(l, 512)
    tk = min(l, 512)
    assert l % tq == 0 and tq % tk == 0
    ratio = tq // tk
    pairs = [(i, j) for i in range(l // tq) for j in range((i + 1) * ratio)]
    qi_tab = jnp.asarray([p[0] for p in pairs], jnp.int32)
    ki_tab = jnp.asarray([p[1] for p in pairs], jnp.int32)
    grid_spec = pltpu.PrefetchScalarGridSpec(
        num_scalar_prefetch=2,
        grid=(b, len(pairs)),
        in_specs=[
            pl.BlockSpec((1, tq, dqk), lambda i, s, qt, kt: (i, qt[s], 0)),
            pl.BlockSpec((1, tk, dqk), lambda i, s, qt, kt: (i, kt[s], 0)),
            pl.BlockSpec((1, dv, tk), lambda i, s, qt, kt: (i, 0, kt[s])),
            pl.BlockSpec((4, DIFF_DK), lambda i, s, qt, kt: (0, 0)),
            pl.BlockSpec((DIFF_DV, 1), lambda i, s, qt, kt: (0, 0)),
        ],
        out_specs=pl.BlockSpec((1, tq, dv), lambda i, s, qt, kt: (i, qt[s], 0)),
        scratch_shapes=[
            pltpu.VMEM((DIFF_HEADS, 2 * tq, DIFF_DV), BF16),
            pltpu.VMEM((DIFF_HEADS, 1, 2 * tq), F32),
            pltpu.VMEM((DIFF_HEADS, DIFF_DV + ONES_ROWS, 2 * tq), F32),
        ],
    )
    return pl.pallas_call(
        functools.partial(_flash_kernel, ratio=ratio, lam_init=lam_init),
        out_shape=jax.ShapeDtypeStruct((b, l, dv), BF16),
        grid_spec=grid_spec,
        compiler_params=_params(("parallel", "arbitrary")),
        name="flash_prompt",
    )(qi_tab, ki_tab, q, kb, vt, lam_vecs, subln.reshape(DIFF_DV, 1))


def _decode_jobs(q_s, k_new, v_new, cache_k, cache_v, page_table, lam_vecs, subln, lam_init, ncalls):
    n, dqk = q_s.shape
    dv = v_new.shape[1]
    n_phys, psize = cache_k.shape[:2]
    n_pages = page_table.shape[1]
    assert n % ncalls == 0 and psize & (psize - 1) == 0
    nseq = n // ncalls
    common = dict(
        q=q_s.reshape(n, 1, dqk), k_new=k_new.reshape(n, 1, dqk), v_new=v_new.reshape(n, 1, dv),
        ckt=jnp.transpose(cache_k, (0, 2, 3, 4, 1)).reshape(n_phys, dqk, psize),
        cvh=jnp.transpose(cache_v, (0, 2, 1, 3)).reshape(n_phys, DIFF_HEADS * psize, DIFF_DV),
        page_table=page_table.reshape(-1), lam_vecs=lam_vecs,
        subrow=jnp.tile(subln, DIFF_HEADS).reshape(1, dv), lam_init=lam_init,
        dqk=dqk, dv=dv, n_pages=n_pages, psize=psize, nseq=nseq)
    return [dict(common, seq0=c * nseq) for c in range(ncalls)]


def _post_sample_kernel(x_ref, om_ref, qm_ref, mk_ref, mv_ref, wo_ref, xo_ref, om_scr):
    _mem_attend_rows(qm_ref[...].astype(F32), mk_ref, mv_ref, om_scr)
    dm = om_ref.shape[1]
    xo_ref[...] = (x_ref[...] + _dot(om_ref[...], wo_ref[0, :dm, :])
                   + _dot(om_scr[...].astype(BF16), wo_ref[0, dm:, :]))


def _post_sample(x, o_main, q_mem, mem_k, mem_v, w_out_b, layer):
    n, d = x.shape
    rows = 2 * SUBLANES
    assert n % rows == 0
    dm = o_main.shape[1]
    m = mem_k.shape[2]
    return pl.pallas_call(
        _post_sample_kernel,
        out_shape=jax.ShapeDtypeStruct((n, d), F32),
        grid=(n // rows,),
        in_specs=[
            pl.BlockSpec((rows, d), lambda i: (i, 0)),
            pl.BlockSpec((rows, dm), lambda i: (i, 0)),
            pl.BlockSpec((rows, MEM_W), lambda i: (i, 0)),
            pl.BlockSpec((rows, MEM_W, m), lambda i: (i, 0, 0)),
            pl.BlockSpec((rows, MEM_W, m), lambda i: (i, 0, 0)),
            _layer_spec(w_out_b.shape[1:], layer),
        ],
        out_specs=pl.BlockSpec((rows, d), lambda i: (i, 0)),
        scratch_shapes=[pltpu.VMEM((rows, MEM_W), F32)],
        compiler_params=_params(("parallel",)),
        name="post_sample",
    )(x, o_main, q_mem, mem_k, mem_v, w_out_b)


def _pad_heads_cols(w, heads, width, padded):
    lead = w.shape[:-1]
    w = w.reshape(lead + (heads, width))
    w = jnp.pad(w, [(0, 0)] * len(lead) + [(0, 0), (0, padded - width)])
    return w.reshape(lead + (heads * padded,))


def _prep_layer_a(w_in, w_gate2, b_gate2, onorm, w_out):
    dqk = GLA_HEADS * GLA_DK
    dv = GLA_HEADS * GLA_DV
    w_in = w_in.astype(BF16)
    w_out = w_out.astype(BF16)
    o = 0
    wq = w_in[:, o:o + dqk]; o += dqk
    wk = w_in[:, o:o + dqk]; o += dqk
    wv = w_in[:, o:o + dv]; o += dv
    wg = w_in[:, o:o + GLA_GATE_RANK]; o += GLA_GATE_RANK
    wr = w_in[:, o:o + dv]; o += dv
    wm = w_in[:, o:o + MEM_W]
    w_all = jnp.concatenate([
        _pad_heads_cols(wq, GLA_HEADS, GLA_DK, GLA_DK_PAD),
        _pad_heads_cols(wk, GLA_HEADS, GLA_DK, GLA_DK_PAD),
        _pad_heads_cols(wv, GLA_HEADS, GLA_DV, GLA_DV_PAD),
        jnp.pad(wg, ((0, 0), (0, LANES - GLA_GATE_RANK))),
        _pad_heads_cols(wr, GLA_HEADS, GLA_DV, GLA_DV_PAD),
        wm,
    ], axis=1)
    wg2 = _pad_heads_cols(w_gate2, GLA_HEADS, GLA_DK, GLA_DK_PAD)
    wg2 = jnp.pad(wg2, ((0, LANES - GLA_GATE_RANK), (0, 0))).astype(BF16)
    bg2 = _pad_heads_cols(b_gate2.reshape(1, dqk), GLA_HEADS, GLA_DK, GLA_DK_PAD)
    gain = _pad_heads_cols(jnp.tile(onorm, GLA_HEADS).reshape(1, dv), GLA_HEADS, GLA_DV, GLA_DV_PAD)
    wo_main = _pad_heads_cols(w_out[:dv].T, GLA_HEADS, GLA_DV, GLA_DV_PAD).T
    return {
        "w_in": w_all, "w_gate2": wg2, "b_gate2": bg2, "onorm": gain,
        "w_out_main": wo_main, "w_out_mem": w_out[dv:],
    }


def kernel(x_prompt, x_sample, mem_prompt, state_gla, cache_k, cache_v, cache_mem_k, cache_mem_v,
           page_table, ffn1_norm, ffn1_w_gate, ffn1_w_up, ffn1_w_down, mix_norm, a_w_in, a_w_gate2,
           a_b_gate2, a_onorm, b_w_in, b_lambda_q1, b_lambda_k1, b_lambda_q2, b_lambda_k2, b_subln,
           mem_norm, w_mem_kv, w_out, ffn2_norm, ffn2_w_gate, ffn2_w_up, ffn2_w_down, kv_norm, w_kv,
           final_norm):
    bp, seq, d = x_prompt.shape
    db, dseq, _ = x_sample.shape
    assert dseq == 1
    depth = ffn1_norm.shape[0]
    n_a = a_w_in.shape[0]
    mtok = mem_prompt.shape[1]
    assert depth == n_a + 1
    lb = depth - 1
    bf = lambda t: t.astype(BF16)
    f1 = (bf(ffn1_w_gate), bf(ffn1_w_up), bf(ffn1_w_down))
    f2 = (bf(ffn2_w_gate), bf(ffn2_w_up), bf(ffn2_w_down))
    w_kv_b = bf(w_kv)
    w_mem_b = bf(w_mem_kv)
    w_out_b = bf(w_out)
    b_w_in_b = bf(b_w_in)

    mem_k_p, mem_v_p = _memkv(mem_prompt, mem_norm, w_mem_b)
    cmk = jnp.transpose(cache_mem_k, (0, 1, 3, 4, 2)).reshape(depth, db, MEM_W, mtok)
    cmv = jnp.transpose(cache_mem_v, (0, 1, 3, 4, 2)).reshape(depth, db, MEM_W, mtok)

    layer_a = [_prep_layer_a(a_w_in[i], a_w_gate2[i], a_b_gate2[i], a_onorm[i], w_out[i]) for i in range(n_a)]
    lam_init = 0.8 - 0.6 * math.exp(-0.3 * lb)
    lam_vecs = jnp.stack([b_lambda_q1[0], b_lambda_k1[0], b_lambda_q2[0], b_lambda_k2[0]])

    xs = x_sample.reshape(1, db, d)
    states_s = []
    for i in range(n_a):
        xs = _ffn(xs, ffn1_norm, *f1, i)
        xs2, st_s = _mixer_a_sample(xs[0], mix_norm[i], layer_a[i], state_gla[i], cmk[i], cmv[i])
        states_s.append(st_s)
        xs = _ffn(xs2[None], ffn2_norm, *f2, i)
    k_s, v_s = _kvproj_sample(xs[0], kv_norm, w_kv_b)
    xs = _ffn(xs, ffn1_norm, *f1, lb)
    q_s, qm_s = _projb_sample(xs[0], mix_norm[lb], b_w_in_b, 0)

    jobs = iter(_decode_jobs(q_s, k_s, v_s, cache_k, cache_v, page_table, lam_vecs, b_subln[0], lam_init,
                             ncalls=2 * depth))
    o_s_parts = []
    xp = x_prompt
    states_p = []
    for i in range(n_a):
        xp, part = _ffn(xp, ffn1_norm, *f1, i, decode=next(jobs))
        o_s_parts.append(part)
        xp, st_p = _mixer_a_prompt(xp, mix_norm[i], layer_a[i], mem_k_p[i], mem_v_p[i])
        states_p.append(st_p)
        if i == n_a - 1:
            xp, *kv_p, part = _ffn(xp, ffn2_norm, *f2, i, post="kv", post_args=(kv_norm, w_kv_b),
                                   decode=next(jobs))
        else:
            xp, part = _ffn(xp, ffn2_norm, *f2, i, decode=next(jobs))
        o_s_parts.append(part)
    xp, q_p, qm_p, part = _ffn(xp, ffn1_norm, *f1, lb, post="q", post_args=(mix_norm[lb], b_w_in_b, 0),
                               decode=next(jobs))
    o_s_parts.append(part)
    kt_p, vh_p, kb_p, vt_p = kv_p
    o_p = _flash_prompt(q_p, kb_p, vt_p, lam_vecs, b_subln[0], lam_init)
    xp, part = _ffn(xp, ffn2_norm, *f2, lb, pre="mix_out", pre_args=(o_p, qm_p, mem_k_p[lb], mem_v_p[lb], w_out_b),
                    post="final_norm", post_args=(final_norm,), decode=next(jobs))
    o_s_parts.append(part)

    o_s = jnp.concatenate(o_s_parts, axis=0).reshape(db, -1)
    xs = _post_sample(xs[0], o_s, qm_s, cmk[lb], cmv[lb], w_out_b, lb)[None]
    xs = _ffn(xs, ffn2_norm, *f2, lb, post="final_norm", post_args=(final_norm,))

    y_prompt = xp
    y_sample = xs.reshape(db, 1, d)

    def mem_rows(t):
        return jnp.transpose(t.reshape(depth, bp, MEM_HEADS, MEM_DH, mtok), (0, 1, 4, 2, 3))

    return (
        y_prompt,
        y_sample,
        jnp.stack(states_p),
        jnp.stack(states_s),
        jnp.transpose(kt_p.reshape(bp, DIFF_HEADS, 2, DIFF_DK, seq), (0, 4, 1, 2, 3)),
        jnp.transpose(vh_p, (0, 2, 1, 3)),
        k_s.reshape(db, 1, DIFF_HEADS, 2, DIFF_DK),
        v_s.reshape(db, 1, DIFF_HEADS, DIFF_DV),
        mem_rows(mem_k_p),
        mem_rows(mem_v_p),
    )
```

```python
import functools
import math

import jax
import jax.numpy as jnp
from jax import lax
from jax.experimental import pallas as pl
from jax.experimental.pallas import tpu as pltpu

F32 = jnp.float32
BF16 = jnp.bfloat16

EPS = 1e-6
GLA_HEADS = 4
GLA_DK = 96
GLA_DV = 192
GLA_GATE_RANK = 16
GLA_GATE_NORM = 16.0
DIFF_HEADS = 6
DIFF_DK = 64
DIFF_DV = 2 * DIFF_DK
MEM_HEADS = 4
MEM_DH = 64
MEM_W = MEM_HEADS * MEM_DH
LOG2E = math.log2(math.e)

LANES = 128
SUBLANES = 8
VMEM_LIMIT_BYTES = 56 * 1024 * 1024

GLA_DK_PAD = 128
GLA_DV_PAD = 256
GLA_CHUNK = 64

A_Q0 = 0
A_K0 = A_Q0 + GLA_HEADS * GLA_DK_PAD
A_V0 = A_K0 + GLA_HEADS * GLA_DK_PAD
A_G0 = A_V0 + GLA_HEADS * GLA_DV_PAD
A_R0 = A_G0 + LANES
A_M0 = A_R0 + GLA_HEADS * GLA_DV_PAD
A_W = A_M0 + MEM_W


def _dot(a, b):
    return jnp.dot(a, b, preferred_element_type=F32)


def _dot_nt(a, b):
    return lax.dot_general(a, b, (((1,), (1,)), ((), ())), preferred_element_type=F32)


def _rms(x, g):
    ms = jnp.mean(x * x, axis=-1, keepdims=True)
    return x * lax.rsqrt(ms + EPS) * g


def _const_spec(shape):
    nd = len(shape)
    return pl.BlockSpec(shape, lambda *_: (0,) * nd, pipeline_mode=pl.Buffered(1))


def _params(sem):
    return pltpu.CompilerParams(dimension_semantics=sem, vmem_limit_bytes=VMEM_LIMIT_BYTES)


FF_CHUNK = 256


def _layer_spec(shape, layer):
    nd = len(shape)
    return pl.BlockSpec((1,) + shape, lambda *_: (layer,) + (0,) * nd, pipeline_mode=pl.Buffered(1))


def _swiglu_half_step(x, gain, wg_ref, wu_ref, wd_ref, side_work=None):
    h = _rms(x, gain).astype(BF16)
    nchunk = wg_ref.shape[2] // FF_CHUNK

    def up(c):
        sl = slice(c * FF_CHUNK, (c + 1) * FF_CHUNK)
        return _dot(h, wg_ref[0, :, sl]), _dot(h, wu_ref[0, :, sl])

    acc = jnp.zeros(x.shape, F32)
    g, u = up(0)
    for c in range(nchunk):
        a = (g * jax.nn.sigmoid(g) * u).astype(BF16)
        if c + 1 < nchunk:
            g, u = up(c + 1)
        if side_work is not None:
            side_work[c]()
        acc = acc + _dot(a, wd_ref[0, c * FF_CHUNK:(c + 1) * FF_CHUNK, :])
    return x + 0.5 * acc


_N_POST_IN = {None: 0, "final_norm": 1, "kv": 2, "q": 2}
_N_POST_OUT = {None: 0, "final_norm": 0, "kv": 4, "q": 2}
_N_DECODE_IN = 7


def _ffn_kernel(*refs, pre, post, decode):
    refs = list(refs)
    if decode:
        pt_ref = refs.pop(0)
    n_in = 1 + (5 if pre else 0) + 4 + _N_POST_IN[post]
    ins, refs = refs[:n_in], refs[n_in:]
    if decode:
        pps, sps, lam_init, seq0, n_pages = decode
        qs_ref, kn_ref, vn_ref, lam_ref, subrow_ref, ck_hbm, cv_hbm = refs[:_N_DECODE_IN]
        refs = refs[_N_DECODE_IN:]
    n_out = 1 + _N_POST_OUT[post]
    outs, refs = refs[:n_out], refs[n_out:]
    if decode:
        os_ref, *decode_state, kbuf, vbuf, sem = refs
        nsteps = pl.num_programs(0) * pl.num_programs(1)
        step = pl.program_id(0) * pl.num_programs(1) + pl.program_id(1)
        group = lax.rem(step, sps)
        slot = lax.rem(step, 2)

        def page_copies(t, into):
            base = (seq0 + lax.div(t, sps)) * n_pages + lax.rem(t, sps) * pps
            copies = []
            for k in range(pps):
                page = pt_ref[base + k]
                copies.append(pltpu.make_async_copy(ck_hbm.at[page], kbuf.at[into * pps + k], sem.at[into, 0]))
                copies.append(pltpu.make_async_copy(cv_hbm.at[page], vbuf.at[into * pps + k], sem.at[into, 1]))
            return copies

        @pl.when(step == 0)
        def _():
            for c in page_copies(step, slot):
                c.start()

        @pl.when(step + 1 < nsteps)
        def _():
            for c in page_copies(step + 1, 1 - slot):
                c.start()

        for c in page_copies(step, slot):
            c.wait()
        kt_pages = [kbuf.at[slot * pps + k] for k in range(pps)]
        v_pages = [vbuf.at[slot * pps + k] for k in range(pps)]

        @pl.when(group == 0)
        def _():
            _decode_init(qs_ref, *decode_state)

    x = ins[0][0]
    ins = ins[1:]
    if pre == "mix_out":
        om_ref, qm_ref, mk_ref, mv_ref, wo_ref = ins[:5]
        ins = ins[5:]
        dm = om_ref.shape[2]
        o_mem = _mem_attend_shared(qm_ref[0], mk_ref[0], mv_ref[0].astype(BF16))
        x = x + _dot(om_ref[0], wo_ref[0, :dm, :]) + _dot(o_mem.astype(BF16), wo_ref[0, dm:, :])
    g_ref, wg_ref, wu_ref, wd_ref = ins[:4]
    ins = ins[4:]
    side_work = None
    if decode:
        side_work = _decode_side_work(kt_pages, v_pages, *decode_state, wg_ref.shape[2] // FF_CHUNK)
    y = _swiglu_half_step(x, g_ref[0], wg_ref, wu_ref, wd_ref, side_work)
    o_ref = outs[0]
    if post == "final_norm":
        o_ref[0] = _rms(y, ins[0][...])
    else:
        o_ref[0] = y
    if post in ("kv", "q"):
        pg_ref, pw_ref = ins
        h = _rms(y, pg_ref[...]).astype(BF16)
        if post == "kv":
            _store_prompt_kv(h, pw_ref, *outs[1:])
        else:
            q_ref, qm_out_ref = outs[1:]
            dqk = q_ref.shape[2]
            q_ref[0] = (_dot(h, pw_ref[0, :, :dqk]) * (DIFF_DK ** -0.5 * LOG2E)).astype(BF16)
            qm_out_ref[0] = _dot(h, pw_ref[0, :, dqk:]).astype(BF16)

    if decode:
        @pl.when(group == sps - 1)
        def _():
            _decode_finish(kn_ref, vn_ref, lam_ref, subrow_ref, os_ref, *decode_state, lam_init)


def _ffn(x, gains, wg, wu, wd, layer, pre=None, post=None, pre_args=(), post_args=(), decode=None):
    b, l, d = x.shape
    d_ff = wg.shape[2]
    tm = min(l, 256 if decode else 512)
    assert l % tm == 0 and d_ff % FF_CHUNK == 0
    depth = gains.shape[0]
    nlt = l // tm
    row = lambda width: pl.BlockSpec((1, tm, width), lambda i, j, *_: (i, j, 0))
    in_specs = [row(d)]
    args = [x]
    if pre == "mix_out":
        o_main, q_mem, mem_kt, mem_vt, w_out_b = pre_args
        m = mem_kt.shape[2]
        per_seq = pl.BlockSpec((1, MEM_W, m), lambda i, j, *_: (i, 0, 0))
        in_specs += [row(o_main.shape[2]), row(MEM_W), per_seq, per_seq, _layer_spec(w_out_b.shape[1:], layer)]
        args += [o_main, q_mem, mem_kt, mem_vt, w_out_b]
    in_specs += [_layer_spec((1, d), layer), _layer_spec((d, d_ff), layer),
                 _layer_spec((d, d_ff), layer), _layer_spec((d_ff, d), layer)]
    args += [gains.reshape(depth, 1, d), wg, wu, wd]
    out_shape = [jax.ShapeDtypeStruct((b, l, d), F32)]
    out_specs = [row(d)]
    if post == "final_norm":
        (final_gain,) = post_args
        in_specs.append(_const_spec((1, d)))
        args.append(final_gain.reshape(1, d))
    elif post == "kv":
        kv_gain, w_kv_b = post_args
        in_specs += [_const_spec((1, d)), _const_spec(w_kv_b.shape)]
        args += [kv_gain.reshape(1, d), w_kv_b]
        kv_shapes, kv_specs = _prompt_kv_outputs(b, l, tm)
        out_shape += kv_shapes
        out_specs += kv_specs
    elif post == "q":
        q_gain, w_in_b, q_layer = post_args
        dqk = DIFF_HEADS * 2 * DIFF_DK
        in_specs += [_const_spec((1, d)), _layer_spec(w_in_b.shape[1:], q_layer)]
        args += [q_gain.reshape(1, d), w_in_b]
        out_shape += [jax.ShapeDtypeStruct((b, l, dqk), BF16), jax.ShapeDtypeStruct((b, l, MEM_W), BF16)]
        out_specs += [row(dqk), row(MEM_W)]
    name = "ffn" + ("_" + pre if pre else "") + ("_" + post if post else "")
    kernel_kwargs = dict(pre=pre, post=post, decode=None)
    scratch, prefetch = [], []
    if decode:
        job = decode
        dqk, dv, n_pages, psize = job["dqk"], job["dv"], job["n_pages"], job["psize"]
        seq0, nseq = job["seq0"], job["nseq"]
        nsteps = b * nlt
        assert nsteps % nseq == 0, "each sequence's page sweep must take a whole number of grid steps"
        sps = nsteps // nseq
        assert n_pages % sps == 0
        pps = n_pages // sps

        def seq_local(i, j):
            return (i * nlt + j) // sps

        seq_spec = pl.BlockSpec((1, 1, dqk), lambda i, j, pt: (seq0 + seq_local(i, j), 0, 0))
        in_specs += [seq_spec, seq_spec, seq_spec,
                     pl.BlockSpec((4, DIFF_DK), lambda i, j, pt: (0, 0)),
                     pl.BlockSpec((1, dv), lambda i, j, pt: (0, 0)),
                     pl.BlockSpec(memory_space=pl.ANY), pl.BlockSpec(memory_space=pl.ANY)]
        args += [job["q"], job["k_new"], job["v_new"], job["lam_vecs"], job["subrow"], job["ckt"], job["cvh"]]
        out_shape.append(jax.ShapeDtypeStruct((nseq, 1, dv), BF16))
        out_specs.append(pl.BlockSpec((1, 1, dv), lambda i, j, pt: (seq_local(i, j), 0, 0)))
        page_bufs = (2 * pps,) + job["ckt"].shape[1:]
        assert job["cvh"].shape[1:] == job["ckt"].shape[1:][:1] + (DIFF_HEADS * psize, DIFF_DV)
        scratch = [pltpu.VMEM((DECODE_ROWS, dqk), BF16), pltpu.VMEM((DECODE_ROWS, 1), F32),
                   pltpu.VMEM((DECODE_ROWS, 1), F32), pltpu.VMEM((DECODE_ROWS, DIFF_DV), F32),
                   pltpu.VMEM(page_bufs, F32), pltpu.VMEM((2 * pps,) + job["cvh"].shape[1:], F32),
                   pltpu.SemaphoreType.DMA((2, 2))]
        prefetch = [job["page_table"]]
        kernel_kwargs["decode"] = (pps, sps, job["lam_init"], seq0, n_pages)
        name += "_decode"
    out = pl.pallas_call(
        functools.partial(_ffn_kernel, **kernel_kwargs),
        out_shape=tuple(out_shape),
        grid_spec=pltpu.PrefetchScalarGridSpec(
            num_scalar_prefetch=len(prefetch), grid=(b, nlt), in_specs=in_specs,
            out_specs=tuple(out_specs), scratch_shapes=scratch),
        compiler_params=_params(("arbitrary", "arbitrary") if decode else ("parallel", "parallel")),
        name=name,
    )(*prefetch, *args)
    return out[0] if len(out) == 1 else out


def _memkv_kernel(m_ref, g_ref, w_ref, kt_ref, vt_ref):
    h = _rms(m_ref[0], g_ref[0]).astype(BF16)
    kv = _dot(h, w_ref[0])
    kt_ref[0, 0] = kv[:, :MEM_W].T
    vt_ref[0, 0] = kv[:, MEM_W:].T


def _memkv(mem, gains, w):
    depth = w.shape[0]
    b, m, d = mem.shape
    out = jax.ShapeDtypeStruct((depth, b, MEM_W, m), F32)
    return pl.pallas_call(
        _memkv_kernel,
        out_shape=(out, out),
        grid=(depth, b),
        in_specs=[
            pl.BlockSpec((1, m, d), lambda i, j: (j, 0, 0)),
            pl.BlockSpec((1, 1, d), lambda i, j: (i, 0, 0)),
            pl.BlockSpec((1, d, 2 * MEM_W), lambda i, j: (i, 0, 0)),
        ],
        out_specs=(
            pl.BlockSpec((1, 1, MEM_W, m), lambda i, j: (i, j, 0, 0)),
            pl.BlockSpec((1, 1, MEM_W, m), lambda i, j: (i, j, 0, 0)),
        ),
        compiler_params=_params(("arbitrary", "arbitrary")),
        name="memkv",
    )(mem, gains.reshape(depth, 1, d), w)


def _log_sigmoid(z):
    return jnp.minimum(z, 0.0) - jnp.log1p(jnp.exp(-jnp.abs(z)))


def _gla_recurrence_inputs(h, win_ref, wg2_ref, bg2_ref):
    q = _dot(h, win_ref[:, A_Q0:A_K0])
    k = _dot(h, win_ref[:, A_K0:A_V0])
    v = _dot(h, win_ref[:, A_V0:A_G0])
    g_lr = _dot(h, win_ref[:, A_G0:A_R0])
    z = _dot(g_lr.astype(BF16), wg2_ref[...]) + bg2_ref[...]
    logg = _log_sigmoid(z) / GLA_GATE_NORM
    return q, k, v, logg


def _gla_inputs(h, win_ref, wg2_ref, bg2_ref):
    q, k, v, logg = _gla_recurrence_inputs(h, win_ref, wg2_ref, bg2_ref)
    r = _dot(h, win_ref[:, A_R0:A_M0])
    q_mem = _dot(h, win_ref[:, A_M0:A_W])
    return q, k, v, logg, r, q_mem


def _gla_output_gate(o, r, gain):
    parts = []
    for h in range(GLA_HEADS):
        sl = slice(h * GLA_DV_PAD, (h + 1) * GLA_DV_PAD)
        oh = o[:, sl]
        ms = jnp.sum(oh * oh, axis=-1, keepdims=True) * (1.0 / GLA_DV)
        parts.append(oh * lax.rsqrt(ms + EPS) * gain[:, sl])
    on = jnp.concatenate(parts, axis=1)
    return on * (r * jax.nn.sigmoid(r))


def _mem_attend_shared(qm, mkt, mvt):
    row_head = lax.broadcasted_iota(jnp.int32, mkt.shape, 0) >> 6
    out_head = lax.broadcasted_iota(jnp.int32, (qm.shape[0], MEM_W), 1) >> 6
    scores = [_dot(qm, jnp.where(row_head == h, mkt, 0.0).astype(BF16)) * (MEM_DH ** -0.5)
              for h in range(MEM_HEADS)]
    o = jnp.zeros((qm.shape[0], MEM_W), F32)
    for h in range(MEM_HEADS):
        e = jnp.exp(scores[h] - jnp.max(scores[h], axis=-1, keepdims=True))
        l = jnp.sum(e, axis=-1, keepdims=True)
        oh = _dot_nt(e.astype(BF16), mvt) / l
        o = o + jnp.where(out_head == h, oh, 0.0)
    return o


def _mem_attend_rows(qmem, mkt_ref, mvt_ref, om_scr):
    rows = qmem.shape[0]
    r8 = lax.broadcasted_iota(jnp.int32, (SUBLANES, MEM_W), 0)
    lh = lax.broadcasted_iota(jnp.int32, (SUBLANES, MEM_W), 1) >> 6
    for j in range(rows):
        qrow = jnp.broadcast_to(qmem[j:j + 1], (SUBLANES, MEM_W))
        qm = jnp.where(r8 == lh, qrow, 0.0).astype(BF16)
        s = _dot(qm, mkt_ref[j].astype(BF16)) * (MEM_DH ** -0.5)
        e = jnp.exp(s - jnp.max(s, axis=-1, keepdims=True))
        l = jnp.sum(e, axis=-1, keepdims=True)
        oh = _dot_nt(e.astype(BF16), mvt_ref[j].astype(BF16)) / l
        om_scr[j:j + 1, :] = jnp.sum(jnp.where(r8 == lh, oh, 0.0), axis=0, keepdims=True)


def _mixer_a_prompt_kernel(x_ref, g_ref, win_ref, wg2_ref, bg2_ref, on_ref, mk_ref, mv_ref,
                           wom_ref, wome_ref, xo_ref, st_ref, s_scr):
    tl = x_ref.shape[1]
    li = pl.program_id(1)

    @pl.when(li == 0)
    def _():
        s_scr[...] = jnp.zeros(s_scr.shape, F32)

    x = x_ref[0]
    h = _rms(x, g_ref[...]).astype(BF16)
    q, k, v, logg = _gla_recurrence_inputs(h, win_ref, wg2_ref, bg2_ref)
    r = _dot(h, win_ref[:, A_R0:A_M0])
    q_mem = _dot(h, win_ref[:, A_M0:A_W])

    row = lax.broadcasted_iota(jnp.int32, (tl, tl), 0)
    col = lax.broadcasted_iota(jnp.int32, (tl, tl), 1)
    tril = ((row >> 6) == (col >> 6)) & (col <= row)
    trilb = jnp.where(tril, 1.0, 0.0).astype(BF16)
    hi = logg.astype(BF16)
    lo = (logg - hi.astype(F32)).astype(BF16)
    b = _dot(trilb, hi) + _dot(trilb, lo)

    o_mem = _mem_attend_shared(q_mem.astype(BF16), mk_ref[0], mv_ref[0].astype(BF16))

    lane_chunk = lax.broadcasted_iota(jnp.int32, (GLA_DK_PAD, tl), 1) >> 6
    scale = GLA_DK ** -0.5
    nchunk = tl // GLA_CHUNK
    qdbs, intras, decs, incs = [], [], [], []
    for hd in range(GLA_HEADS):
        ks = slice(hd * GLA_DK_PAD, (hd + 1) * GLA_DK_PAD)
        vs = slice(hd * GLA_DV_PAD, (hd + 1) * GLA_DV_PAD)
        bh = b[:, ks]
        qdb = ((q[:, ks] * scale) * jnp.exp(bh)).astype(BF16)
        ki = k[:, ks] * jnp.exp(-bh)
        att = jnp.where(tril, _dot_nt(qdb, ki.astype(BF16)), 0.0)
        vh = v[:, vs].astype(BF16)
        intras.append(_dot(att.astype(BF16), vh))
        ki_t = ki.T
        b_t = bh.T
        dec_h, inc_h = [], []
        for c in range(nchunk):
            last = c * GLA_CHUNK + GLA_CHUNK - 1
            dec = jnp.exp(b_t[:, last:last + 1])
            kend_t = jnp.where(lane_chunk == c, ki_t * dec, 0.0).astype(BF16)
            dec_h.append(dec)
            inc_h.append(_dot(kend_t, vh))
        qdbs.append(qdb)
        decs.append(dec_h)
        incs.append(inc_h)

    o_parts = []
    for hd in range(GLA_HEADS):
        state = s_scr[hd]
        starts = []
        for c in range(nchunk):
            starts.append(state.astype(BF16))
            state = decs[hd][c] * state + incs[hd][c]
        s_scr[hd] = state
        inter = [_dot(qdbs[hd][c * GLA_CHUNK:(c + 1) * GLA_CHUNK], starts[c]) for c in range(nchunk)]
        o_parts.append(intras[hd] + jnp.concatenate(inter, axis=0))
    o = jnp.concatenate(o_parts, axis=1)

    o_main = _gla_output_gate(o, r, on_ref[...])
    xo_ref[0] = x + _dot(o_main.astype(BF16), wom_ref[...]) + _dot(o_mem.astype(BF16), wome_ref[...])

    @pl.when(li == pl.num_programs(1) - 1)
    def _():
        for hd in range(GLA_HEADS):
            st_ref[0, hd] = s_scr[hd][:GLA_DK, :GLA_DV]


def _mixer_a_prompt(x, gain, wa, mem_k, mem_v):
    b, l, d = x.shape
    tl = min(l, 256)
    assert l % tl == 0 and tl % GLA_CHUNK == 0
    m = mem_k.shape[2]
    return pl.pallas_call(
        _mixer_a_prompt_kernel,
        out_shape=(
            jax.ShapeDtypeStruct((b, l, d), F32),
            jax.ShapeDtypeStruct((b, GLA_HEADS, GLA_DK, GLA_DV), F32),
        ),
        grid=(b, l // tl),
        in_specs=[
            pl.BlockSpec((1, tl, d), lambda i, j: (i, j, 0)),
            _const_spec((1, d)),
            _const_spec((d, A_W)),
            _const_spec((LANES, GLA_HEADS * GLA_DK_PAD)),
            _const_spec((1, GLA_HEADS * GLA_DK_PAD)),
            _const_spec((1, GLA_HEADS * GLA_DV_PAD)),
            pl.BlockSpec((1, MEM_W, m), lambda i, j: (i, 0, 0)),
            pl.BlockSpec((1, MEM_W, m), lambda i, j: (i, 0, 0)),
            _const_spec((GLA_HEADS * GLA_DV_PAD, d)),
            _const_spec((MEM_W, d)),
        ],
        out_specs=(
            pl.BlockSpec((1, tl, d), lambda i, j: (i, j, 0)),
            pl.BlockSpec((1, GLA_HEADS, GLA_DK, GLA_DV), lambda i, j: (i, 0, 0, 0)),
        ),
        scratch_shapes=[pltpu.VMEM((GLA_HEADS, GLA_DK_PAD, GLA_DV_PAD), F32)],
        compiler_params=_params(("parallel", "arbitrary")),
        name="mixer_a_prompt",
    )(x, gain.reshape(1, d), wa["w_in"], wa["w_gate2"], wa["b_gate2"], wa["onorm"], mem_k, mem_v,
      wa["w_out_main"], wa["w_out_mem"])


def _columns(t):
    pad = jnp.zeros((LANES - t.shape[0], t.shape[1]), F32)
    return jnp.concatenate([t, pad], axis=0).T


def _mixer_a_sample_kernel(x_ref, g_ref, win_ref, wg2_ref, bg2_ref, on_ref, st_ref, mk_ref, mv_ref,
                           wom_ref, wome_ref, xo_ref, sto_ref, o_scr, om_scr):
    rows = x_ref.shape[0]
    x = x_ref[...]
    h = _rms(x, g_ref[...]).astype(BF16)
    q, k, v, logg, r, q_mem = _gla_inputs(h, win_ref, wg2_ref, bg2_ref)
    a_t = _columns(jnp.exp(logg))
    k_t = _columns(k)
    q_t = _columns(q * (GLA_DK ** -0.5))
    o_scr[...] = jnp.zeros(o_scr.shape, F32)
    for j in range(rows):
        for hd in range(GLA_HEADS):
            rs = slice(hd * GLA_DK_PAD, hd * GLA_DK_PAD + GLA_DK)
            vs = slice(hd * GLA_DV_PAD, hd * GLA_DV_PAD + GLA_DV)
            s_new = a_t[rs, j:j + 1] * st_ref[j, hd] + k_t[rs, j:j + 1] * v[j:j + 1, vs]
            sto_ref[j, hd] = s_new
            o_scr[j:j + 1, vs] = jnp.sum(q_t[rs, j:j + 1] * s_new, axis=0, keepdims=True)
    o_main = _gla_output_gate(o_scr[...], r, on_ref[...])
    _mem_attend_rows(q_mem, mk_ref, mv_ref, om_scr)
    xo_ref[...] = (x + _dot(o_main.astype(BF16), wom_ref[...])
                   + _dot(om_scr[...].astype(BF16), wome_ref[...]))


def _mixer_a_sample(x, gain, wa, state, mem_k, mem_v):
    n, d = x.shape
    rows = SUBLANES
    assert n % rows == 0
    m = mem_k.shape[2]
    return pl.pallas_call(
        _mixer_a_sample_kernel,
        out_shape=(
            jax.ShapeDtypeStruct((n, d), F32),
            jax.ShapeDtypeStruct(state.shape, F32),
        ),
        grid=(n // rows,),
        in_specs=[
            pl.BlockSpec((rows, d), lambda i: (i, 0)),
            _const_spec((1, d)),
            _const_spec((d, A_W)),
            _const_spec((LANES, GLA_HEADS * GLA_DK_PAD)),
            _const_spec((1, GLA_HEADS * GLA_DK_PAD)),
            _const_spec((1, GLA_HEADS * GLA_DV_PAD)),
            pl.BlockSpec((rows, GLA_HEADS, GLA_DK, GLA_DV), lambda i: (i, 0, 0, 0)),
            pl.BlockSpec((rows, MEM_W, m), lambda i: (i, 0, 0)),
            pl.BlockSpec((rows, MEM_W, m), lambda i: (i, 0, 0)),
            _const_spec((GLA_HEADS * GLA_DV_PAD, d)),
            _const_spec((MEM_W, d)),
        ],
        out_specs=(
            pl.BlockSpec((rows, d), lambda i: (i, 0)),
            pl.BlockSpec((rows, GLA_HEADS, GLA_DK, GLA_DV), lambda i: (i, 0, 0, 0)),
        ),
        scratch_shapes=[
            pltpu.VMEM((rows, GLA_HEADS * GLA_DV_PAD), F32),
            pltpu.VMEM((rows, MEM_W), F32),
        ],
        compiler_params=_params(("parallel",)),
        name="mixer_a_sample",
    )(x, gain.reshape(1, d), wa["w_in"], wa["w_gate2"], wa["b_gate2"], wa["onorm"], state, mem_k, mem_v,
      wa["w_out_main"], wa["w_out_mem"])


def _store_prompt_kv(h, w_ref, kt_ref, vh_ref, kb_ref, vt_ref):
    dqk = DIFF_HEADS * 2 * DIFF_DK
    k = _dot(h, w_ref[:, :dqk])
    v = _dot(h, w_ref[:, dqk:])
    kt_ref[0] = k.T
    for hd in range(DIFF_HEADS):
        vh_ref[0, hd] = v[:, hd * DIFF_DV:(hd + 1) * DIFF_DV]
    kb_ref[0] = k.astype(BF16)
    vt_ref[0] = v.T.astype(BF16)


def _prompt_kv_outputs(b, l, tm):
    dqk = DIFF_HEADS * 2 * DIFF_DK
    dv = DIFF_HEADS * DIFF_DV
    shapes = [jax.ShapeDtypeStruct((b, dqk, l), F32),
              jax.ShapeDtypeStruct((b, DIFF_HEADS, l, DIFF_DV), F32),
              jax.ShapeDtypeStruct((b, l, dqk), BF16),
              jax.ShapeDtypeStruct((b, dv, l), BF16)]
    specs = [pl.BlockSpec((1, dqk, tm), lambda i, j, *_: (i, 0, j)),
             pl.BlockSpec((1, DIFF_HEADS, tm, DIFF_DV), lambda i, j, *_: (i, 0, j, 0)),
             pl.BlockSpec((1, tm, dqk), lambda i, j, *_: (i, j, 0)),
             pl.BlockSpec((1, dv, tm), lambda i, j, *_: (i, 0, j))]
    return shapes, specs


def _kvproj_sample_kernel(x_ref, g_ref, w_ref, k_ref, v_ref):
    h = _rms(x_ref[...], g_ref[...]).astype(BF16)
    dqk = k_ref.shape[1]
    k_ref[...] = _dot(h, w_ref[:, :dqk])
    v_ref[...] = _dot(h, w_ref[:, dqk:])


def _kvproj_sample(x, gain, w):
    n, d = x.shape
    dqk = DIFF_HEADS * 2 * DIFF_DK
    dv = DIFF_HEADS * DIFF_DV
    return pl.pallas_call(
        _kvproj_sample_kernel,
        out_shape=(jax.ShapeDtypeStruct((n, dqk), F32), jax.ShapeDtypeStruct((n, dv), F32)),
        grid=(1,),
        in_specs=[pl.BlockSpec((n, d), lambda i: (0, 0)), _const_spec((1, d)), _const_spec((d, dqk + dv))],
        out_specs=(pl.BlockSpec((n, dqk), lambda i: (0, 0)), pl.BlockSpec((n, dv), lambda i: (0, 0))),
        compiler_params=_params(("arbitrary",)),
        name="kvproj_sample",
    )(x, gain.reshape(1, d), w)


def _projb_sample_kernel(x_ref, g_ref, w_ref, q_ref, qm_ref):
    h = _rms(x_ref[...], g_ref[...]).astype(BF16)
    dqk = q_ref.shape[1]
    q_ref[...] = (_dot(h, w_ref[0, :, :dqk]) * (DIFF_DK ** -0.5 * LOG2E)).astype(BF16)
    qm_ref[...] = _dot(h, w_ref[0, :, dqk:]).astype(BF16)


def _projb_sample(x, gain, w_in_b, layer):
    n, d = x.shape
    dqk = DIFF_HEADS * 2 * DIFF_DK
    return pl.pallas_call(
        _projb_sample_kernel,
        out_shape=(jax.ShapeDtypeStruct((n, dqk), BF16), jax.ShapeDtypeStruct((n, MEM_W), BF16)),
        grid=(1,),
        in_specs=[pl.BlockSpec((n, d), lambda i: (0, 0)), _const_spec((1, d)),
                  _layer_spec(w_in_b.shape[1:], layer)],
        out_specs=(pl.BlockSpec((n, dqk), lambda i: (0, 0)), pl.BlockSpec((n, MEM_W), lambda i: (0, 0))),
        compiler_params=_params(("arbitrary",)),
        name="projb_sample",
    )(x, gain.reshape(1, d), w_in_b)


def _lambda_full(lam_ref, lam_init):
    lv = lam_ref[...]
    s1 = jnp.sum(lv[0:1] * lv[1:2], axis=-1, keepdims=True)
    s2 = jnp.sum(lv[2:3] * lv[3:4], axis=-1, keepdims=True)
    return jnp.exp(s1) - jnp.exp(s2) + lam_init


ONES_ROWS = 16
DECODE_ROWS = 16
SCORE_LOOKAHEAD = 2


def _decode_init(qs_ref, sqm_scr, sm_scr, sl_scr, sacc_scr):
    nrow, width = sqm_scr.shape
    r = lax.broadcasted_iota(jnp.int32, (nrow, width), 0)
    lane_half = lax.broadcasted_iota(jnp.int32, (nrow, width), 1) >> 6
    qrow = jnp.broadcast_to(qs_ref[0].astype(F32), (nrow, width))
    sqm_scr[...] = jnp.where(r == lane_half, qrow, 0.0).astype(BF16)
    sm_scr[...] = jnp.full(sm_scr.shape, -jnp.inf, F32)
    sl_scr[...] = jnp.zeros(sl_scr.shape, F32)
    sacc_scr[...] = jnp.zeros(sacc_scr.shape, F32)


def _decode_side_work(kt_refs, v_refs, sqm_scr, sm_scr, sl_scr, sacc_scr, nslab):
    nrow = sacc_scr.shape[0]
    psize = kt_refs[0].shape[2]
    assert nslab >= 3
    box = {}

    def score_job():
        qm = sqm_scr[...]
        box["s"] = jnp.concatenate([_dot(qm, kr[0].astype(BF16)) for kr in kt_refs], axis=1)

    def softmax_job():
        m_old = sm_scr[...]
        m_new = jnp.maximum(m_old, jnp.max(box["s"], axis=-1, keepdims=True))
        box["pf"] = jnp.exp2(box["s"] - m_new)
        box["alpha"] = jnp.exp2(m_old - m_new)
        sl_scr[...] = box["alpha"] * sl_scr[...] + jnp.sum(box["pf"], axis=-1, keepdims=True)
        sm_scr[...] = m_new

    def value_job():
        vrows = DIFF_HEADS * psize
        row_head = lax.broadcasted_iota(jnp.int32, (nrow, vrows), 0) >> 1
        col_head = lax.broadcasted_iota(jnp.int32, (nrow, vrows), 1) >> (psize.bit_length() - 1)
        own = row_head == col_head
        pv = jnp.zeros(sacc_scr.shape, F32)
        for i, vr in enumerate(v_refs):
            pi = box["pf"][:, i * psize:(i + 1) * psize]
            spread = jnp.where(own, jnp.concatenate([pi] * DIFF_HEADS, axis=1), 0.0).astype(BF16)
            pv = pv + _dot(spread, vr[0].astype(BF16))
        sacc_scr[...] = box["alpha"] * sacc_scr[...] + pv

    idle = lambda: None
    return [score_job, softmax_job] + [idle] * (nslab - 3) + [value_job]


def _decode_finish(kn_ref, vn_ref, lam_ref, subrow_ref, os_ref, sqm_scr, sm_scr, sl_scr, sacc_scr, lam_init):
    nrow = sqm_scr.shape[0]
    s_self = jnp.sum(sqm_scr[...].astype(F32) * kn_ref[0], axis=-1, keepdims=True)
    m_old = sm_scr[...]
    m_fin = jnp.maximum(m_old, s_self)
    p_self = jnp.exp2(s_self - m_fin)
    a_fin = jnp.exp2(m_old - m_fin)
    l_fin = a_fin * sl_scr[...] + p_self
    vn = vn_ref[0]
    vn_rows = jnp.concatenate(
        [vn[:, (r // 2) * DIFF_DV:(r // 2 + 1) * DIFF_DV] for r in range(2 * DIFF_HEADS)]
        + [jnp.zeros((nrow - 2 * DIFF_HEADS, DIFF_DV), F32)], axis=0)
    acc = (a_fin * sacc_scr[...] + p_self * vn_rows) / l_fin
    lam = _lambda_full(lam_ref, lam_init)
    parts = []
    for hd in range(DIFF_HEADS):
        od = acc[2 * hd:2 * hd + 1] - lam * acc[2 * hd + 1:2 * hd + 2]
        ms = jnp.mean(od * od, axis=-1, keepdims=True)
        parts.append(od * lax.rsqrt(ms + EPS))
    on = (jnp.concatenate(parts, axis=1) * subrow_ref[...]) * (1.0 - lam_init)
    os_ref[0] = on.astype(BF16)


def _flash_kernel(qi_tab, ki_tab, q_ref, k_ref, vt_ref, lam_ref, sub_ref, o_ref,
                  qm_scr, m_scr, acc_scr, *, ratio, lam_init):
    step = pl.program_id(1)
    qi = qi_tab[step]
    ki = ki_tab[step]
    tq = q_ref.shape[1]
    tk = k_ref.shape[1]

    @pl.when(ki == 0)
    def _():
        for hd in range(DIFF_HEADS):
            qh = q_ref[0, :, hd * DIFF_DV:(hd + 1) * DIFF_DV].astype(F32)
            first = lax.broadcasted_iota(jnp.int32, qh.shape, 1) < DIFF_DK
            qm_scr[hd, :tq] = jnp.where(first, qh, 0.0).astype(BF16)
            qm_scr[hd, tq:] = jnp.where(first, 0.0, qh).astype(BF16)
        m_scr[...] = jnp.full(m_scr.shape, -jnp.inf, F32)
        acc_scr[...] = jnp.zeros(acc_scr.shape, F32)

    def block(masked):
        if masked:
            col = lax.broadcasted_iota(jnp.int32, (tk, 2 * tq), 1)
            kpos = ki * tk + lax.broadcasted_iota(jnp.int32, (tk, 2 * tq), 0)
            keep = kpos <= qi * tq + jnp.where(col >= tq, col - tq, col)
            bias = jnp.where(keep, 0.0, -jnp.inf)
        ones = jnp.ones((ONES_ROWS, tk), BF16)

        def scores(hd):
            st = _dot_nt(k_ref[0, :, hd * DIFF_DV:(hd + 1) * DIFF_DV], qm_scr[hd])
            return st + bias if masked else st

        pending = [scores(hd) for hd in range(SCORE_LOOKAHEAD)]
        for hd in range(DIFF_HEADS):
            hs = slice(hd * DIFF_DV, (hd + 1) * DIFF_DV)
            st = pending.pop(0)
            if hd + SCORE_LOOKAHEAD < DIFF_HEADS:
                pending.append(scores(hd + SCORE_LOOKAHEAD))
            m_old = m_scr[hd]
            m_new = jnp.maximum(m_old, jnp.max(st, axis=0, keepdims=True))
            p = jnp.exp2(st - m_new).astype(BF16)
            alpha = jnp.exp2(m_old - m_new)
            v_aug = jnp.concatenate([vt_ref[0, hs, :], ones], axis=0)
            acc_scr[hd] = alpha * acc_scr[hd] + _dot(v_aug, p)
            m_scr[hd] = m_new

    diag = ki >= qi * ratio

    @pl.when(diag)
    def _():
        block(True)

    @pl.when(jnp.logical_not(diag))
    def _():
        block(False)

    @pl.when(ki == (qi + 1) * ratio - 1)
    def _():
        lam = _lambda_full(lam_ref, lam_init)
        for hd in range(DIFF_HEADS):
            acc = acc_scr[hd]
            o = acc[:DIFF_DV] / acc[DIFF_DV:DIFF_DV + 1]
            od = o[:, :tq] - lam * o[:, tq:]
            ms = jnp.mean(od * od, axis=0, keepdims=True)
            on = (od * lax.rsqrt(ms + EPS) * sub_ref[...]) * (1.0 - lam_init)
            o_ref[0, :, hd * DIFF_DV:(hd + 1) * DIFF_DV] = on.T.astype(BF16)


def _flash_prompt(q, kb, vt, lam_vecs, subln, lam_init):
    b, l, dqk = q.shape
    dv = vt.shape[1]
    tq = min(l, 512)
    tk = min(l, 512)
    assert l % tq == 0 and tq % tk == 0
    ratio = tq // tk
    pairs = [(i, j) for i in range(l // tq) for j in range((i + 1) * ratio)]
    qi_tab = jnp.asarray([p[0] for p in pairs], jnp.int32)
    ki_tab = jnp.asarray([p[1] for p in pairs], jnp.int32)
    grid_spec = pltpu.PrefetchScalarGridSpec(
        num_scalar_prefetch=2,
        grid=(b, len(pairs)),
        in_specs=[
            pl.BlockSpec((1, tq, dqk), lambda i, s, qt, kt: (i, qt[s], 0)),
            pl.BlockSpec((1, tk, dqk), lambda i, s, qt, kt: (i, kt[s], 0)),
            pl.BlockSpec((1, dv, tk), lambda i, s, qt, kt: (i, 0, kt[s])),
            pl.BlockSpec((4, DIFF_DK), lambda i, s, qt, kt: (0, 0)),
            pl.BlockSpec((DIFF_DV, 1), lambda i, s, qt, kt: (0, 0)),
        ],
        out_specs=pl.BlockSpec((1, tq, dv), lambda i, s, qt, kt: (i, qt[s], 0)),
        scratch_shapes=[
            pltpu.VMEM((DIFF_HEADS, 2 * tq, DIFF_DV), BF16),
            pltpu.VMEM((DIFF_HEADS, 1, 2 * tq), F32),
            pltpu.VMEM((DIFF_HEADS, DIFF_DV + ONES_ROWS, 2 * tq), F32),
        ],
    )
    return pl.pallas_call(
        functools.partial(_flash_kernel, ratio=ratio, lam_init=lam_init),
        out_shape=jax.ShapeDtypeStruct((b, l, dv), BF16),
        grid_spec=grid_spec,
        compiler_params=_params(("parallel", "arbitrary")),
        name="flash_prompt",
    )(qi_tab, ki_tab, q, kb, vt, lam_vecs, subln.reshape(DIFF_DV, 1))


def _decode_jobs(q_s, k_new, v_new, cache_k, cache_v, page_table, lam_vecs, subln, lam_init, ncalls):
    n, dqk = q_s.shape
    dv = v_new.shape[1]
    n_phys, psize = cache_k.shape[:2]
    n_pages = page_table.shape[1]
    assert n % ncalls == 0 and psize & (psize - 1) == 0
    nseq = n // ncalls
    common = dict(
        q=q_s.reshape(n, 1, dqk), k_new=k_new.reshape(n, 1, dqk), v_new=v_new.reshape(n, 1, dv),
        ckt=jnp.transpose(cache_k, (0, 2, 3, 4, 1)).reshape(n_phys, 1, dqk, psize),
        cvh=jnp.transpose(cache_v, (0, 2, 1, 3)).reshape(n_phys, 1, DIFF_HEADS * psize, DIFF_DV),
        page_table=page_table.reshape(-1), lam_vecs=lam_vecs,
        subrow=jnp.tile(subln, DIFF_HEADS).reshape(1, dv), lam_init=lam_init,
        dqk=dqk, dv=dv, n_pages=n_pages, psize=psize, nseq=nseq)
    return [dict(common, seq0=c * nseq) for c in range(ncalls)]


def _post_sample_kernel(x_ref, om_ref, qm_ref, mk_ref, mv_ref, wo_ref, xo_ref, om_scr):
    _mem_attend_rows(qm_ref[...].astype(F32), mk_ref, mv_ref, om_scr)
    dm = om_ref.shape[1]
    xo_ref[...] = (x_ref[...] + _dot(om_ref[...], wo_ref[0, :dm, :])
                   + _dot(om_scr[...].astype(BF16), wo_ref[0, dm:, :]))


def _post_sample(x, o_main, q_mem, mem_k, mem_v, w_out_b, layer):
    n, d = x.shape
    rows = 2 * SUBLANES
    assert n % rows == 0
    dm = o_main.shape[1]
    m = mem_k.shape[2]
    return pl.pallas_call(
        _post_sample_kernel,
        out_shape=jax.ShapeDtypeStruct((n, d), F32),
        grid=(n // rows,),
        in_specs=[
            pl.BlockSpec((rows, d), lambda i: (i, 0)),
            pl.BlockSpec((rows, dm), lambda i: (i, 0)),
            pl.BlockSpec((rows, MEM_W), lambda i: (i, 0)),
            pl.BlockSpec((rows, MEM_W, m), lambda i: (i, 0, 0)),
            pl.BlockSpec((rows, MEM_W, m), lambda i: (i, 0, 0)),
            _layer_spec(w_out_b.shape[1:], layer),
        ],
        out_specs=pl.BlockSpec((rows, d), lambda i: (i, 0)),
        scratch_shapes=[pltpu.VMEM((rows, MEM_W), F32)],
        compiler_params=_params(("parallel",)),
        name="post_sample",
    )(x, o_main, q_mem, mem_k, mem_v, w_out_b)


def _pad_heads_cols(w, heads, width, padded):
    lead = w.shape[:-1]
    w = w.reshape(lead + (heads, width))
    w = jnp.pad(w, [(0, 0)] * len(lead) + [(0, 0), (0, padded - width)])
    return w.reshape(lead + (heads * padded,))


def _prep_layer_a(w_in, w_gate2, b_gate2, onorm, w_out):
    dqk = GLA_HEADS * GLA_DK
    dv = GLA_HEADS * GLA_DV
    w_in = w_in.astype(BF16)
    w_out = w_out.astype(BF16)
    o = 0
    wq = w_in[:, o:o + dqk]; o += dqk
    wk = w_in[:, o:o + dqk]; o += dqk
    wv = w_in[:, o:o + dv]; o += dv
    wg = w_in[:, o:o + GLA_GATE_RANK]; o += GLA_GATE_RANK
    wr = w_in[:, o:o + dv]; o += dv
    wm = w_in[:, o:o + MEM_W]
    w_all = jnp.concatenate([
        _pad_heads_cols(wq, GLA_HEADS, GLA_DK, GLA_DK_PAD),
        _pad_heads_cols(wk, GLA_HEADS, GLA_DK, GLA_DK_PAD),
        _pad_heads_cols(wv, GLA_HEADS, GLA_DV, GLA_DV_PAD),
        jnp.pad(wg, ((0, 0), (0, LANES - GLA_GATE_RANK))),
        _pad_heads_cols(wr, GLA_HEADS, GLA_DV, GLA_DV_PAD),
        wm,
    ], axis=1)
    wg2 = _pad_heads_cols(w_gate2, GLA_HEADS, GLA_DK, GLA_DK_PAD)
    wg2 = jnp.pad(wg2, ((0, LANES - GLA_GATE_RANK), (0, 0))).astype(BF16)
    bg2 = _pad_heads_cols(b_gate2.reshape(1, dqk), GLA_HEADS, GLA_DK, GLA_DK_PAD)
    gain = _pad_heads_cols(jnp.tile(onorm, GLA_HEADS).reshape(1, dv), GLA_HEADS, GLA_DV, GLA_DV_PAD)
    wo_main = _pad_heads_cols(w_out[:dv].T, GLA_HEADS, GLA_DV, GLA_DV_PAD).T
    return {
        "w_in": w_all, "w_gate2": wg2, "b_gate2": bg2, "onorm": gain,
        "w_out_main": wo_main, "w_out_mem": w_out[dv:],
    }


def kernel(x_prompt, x_sample, mem_prompt, state_gla, cache_k, cache_v, cache_mem_k, cache_mem_v,
           page_table, ffn1_norm, ffn1_w_gate, ffn1_w_up, ffn1_w_down, mix_norm, a_w_in, a_w_gate2,
           a_b_gate2, a_onorm, b_w_in, b_lambda_q1, b_lambda_k1, b_lambda_q2, b_lambda_k2, b_subln,
           mem_norm, w_mem_kv, w_out, ffn2_norm, ffn2_w_gate, ffn2_w_up, ffn2_w_down, kv_norm, w_kv,
           final_norm):
    bp, seq, d = x_prompt.shape
    db, dseq, _ = x_sample.shape
    assert dseq == 1
    depth = ffn1_norm.shape[0]
    n_a = a_w_in.shape[0]
    mtok = mem_prompt.shape[1]
    assert depth == n_a + 1
    lb = depth - 1
    bf = lambda t: t.astype(BF16)
    f1 = (bf(ffn1_w_gate), bf(ffn1_w_up), bf(ffn1_w_down))
    f2 = (bf(ffn2_w_gate), bf(ffn2_w_up), bf(ffn2_w_down))
    w_kv_b = bf(w_kv)
    w_mem_b = bf(w_mem_kv)
    w_out_b = bf(w_out)
    b_w_in_b = bf(b_w_in)

    mem_k_p, mem_v_p = _memkv(mem_prompt, mem_norm, w_mem_b)
    cmk = jnp.transpose(cache_mem_k, (0, 1, 3, 4, 2)).reshape(depth, db, MEM_W, mtok)
    cmv = jnp.transpose(cache_mem_v, (0, 1, 3, 4, 2)).reshape(depth, db, MEM_W, mtok)

    layer_a = [_prep_layer_a(a_w_in[i], a_w_gate2[i], a_b_gate2[i], a_onorm[i], w_out[i]) for i in range(n_a)]
    lam_init = 0.8 - 0.6 * math.exp(-0.3 * lb)
    lam_vecs = jnp.stack([b_lambda_q1[0], b_lambda_k1[0], b_lambda_q2[0], b_lambda_k2[0]])

    xs = x_sample.reshape(1, db, d)
    states_s = []
    for i in range(n_a):
        xs = _ffn(xs, ffn1_norm, *f1, i)
        xs2, st_s = _mixer_a_sample(xs[0], mix_norm[i], layer_a[i], state_gla[i], cmk[i], cmv[i])
        states_s.append(st_s)
        xs = _ffn(xs2[None], ffn2_norm, *f2, i)
    k_s, v_s = _kvproj_sample(xs[0], kv_norm, w_kv_b)
    xs = _ffn(xs, ffn1_norm, *f1, lb)
    q_s, qm_s = _projb_sample(xs[0], mix_norm[lb], b_w_in_b, 0)

    jobs = iter(_decode_jobs(q_s, k_s, v_s, cache_k, cache_v, page_table, lam_vecs, b_subln[0], lam_init,
                             ncalls=2 * depth))
    o_s_parts = []
    xp = x_prompt
    states_p = []
    for i in range(n_a):
        xp, part = _ffn(xp, ffn1_norm, *f1, i, decode=next(jobs))
        o_s_parts.append(part)
        xp, st_p = _mixer_a_prompt(xp, mix_norm[i], layer_a[i], mem_k_p[i], mem_v_p[i])
        states_p.append(st_p)
        if i == n_a - 1:
            xp, *kv_p, part = _ffn(xp, ffn2_norm, *f2, i, post="kv", post_args=(kv_norm, w_kv_b),
                                   decode=next(jobs))
        else:
            xp, part = _ffn(xp, ffn2_norm, *f2, i, decode=next(jobs))
        o_s_parts.append(part)
    xp, q_p, qm_p, part = _ffn(xp, ffn1_norm, *f1, lb, post="q", post_args=(mix_norm[lb], b_w_in_b, 0),
                               decode=next(jobs))
    o_s_parts.append(part)
    kt_p, vh_p, kb_p, vt_p = kv_p
    o_p = _flash_prompt(q_p, kb_p, vt_p, lam_vecs, b_subln[0], lam_init)
    xp, part = _ffn(xp, ffn2_norm, *f2, lb, pre="mix_out", pre_args=(o_p, qm_p, mem_k_p[lb], mem_v_p[lb], w_out_b),
                    post="final_norm", post_args=(final_norm,), decode=next(jobs))
    o_s_parts.append(part)

    o_s = jnp.concatenate(o_s_parts, axis=0).reshape(db, -1)
    xs = _post_sample(xs[0], o_s, qm_s, cmk[lb], cmv[lb], w_out_b, lb)[None]
    xs = _ffn(xs, ffn2_norm, *f2, lb, post="final_norm", post_args=(final_norm,))

    y_prompt = xp
    y_sample = xs.reshape(db, 1, d)

    def mem_rows(t):
        return jnp.transpose(t.reshape(depth, bp, MEM_HEADS, MEM_DH, mtok), (0, 1, 4, 2, 3))

    return (
        y_prompt,
        y_sample,
        jnp.stack(states_p),
        jnp.stack(states_s),
        jnp.transpose(kt_p.reshape(bp, DIFF_HEADS, 2, DIFF_DK, seq), (0, 4, 1, 2, 3)),
        jnp.transpose(vh_p, (0, 2, 1, 3)),
        k_s.reshape(db, 1, DIFF_HEADS, 2, DIFF_DK),
        v_s.reshape(db, 1, DIFF_HEADS, DIFF_DV),
        mem_rows(mem_k_p),
        mem_rows(mem_v_p),
    )
```

```python
import functools
import math

import jax
import jax.numpy as jnp
from jax import lax
from jax.experimental import pallas as pl
from jax.experimental.pallas import tpu as pltpu

F32 = jnp.float32
BF16 = jnp.bfloat16

EPS = 1e-6
GLA_HEADS = 4
GLA_DK = 96
GLA_DV = 192
GLA_GATE_RANK = 16
GLA_GATE_NORM = 16.0
DIFF_HEADS = 6
DIFF_DK = 64
DIFF_DV = 2 * DIFF_DK
MEM_HEADS = 4
MEM_DH = 64
MEM_W = MEM_HEADS * MEM_DH
LOG2E = math.log2(math.e)

LANES = 128
SUBLANES = 8
VMEM_LIMIT_BYTES = 56 * 1024 * 1024

GLA_DK_PAD = 128
GLA_DV_PAD = 256
GLA_CHUNK = 64

A_Q0 = 0
A_K0 = A_Q0 + GLA_HEADS * GLA_DK_PAD
A_V0 = A_K0 + GLA_HEADS * GLA_DK_PAD
A_G0 = A_V0 + GLA_HEADS * GLA_DV_PAD
A_R0 = A_G0 + LANES
A_M0 = A_R0 + GLA_HEADS * GLA_DV_PAD
A_W = A_M0 + MEM_W


def _dot(a, b):
    return jnp.dot(a, b, preferred_element_type=F32)


def _dot_nt(a, b):
    return lax.dot_general(a, b, (((1,), (1,)), ((), ())), preferred_element_type=F32)


def _rms(x, g):
    ms = jnp.mean(x * x, axis=-1, keepdims=True)
    return x * lax.rsqrt(ms + EPS) * g


def _const_spec(shape):
    nd = len(shape)
    return pl.BlockSpec(shape, lambda *_: (0,) * nd, pipeline_mode=pl.Buffered(1))


def _params(sem):
    return pltpu.CompilerParams(dimension_semantics=sem, vmem_limit_bytes=VMEM_LIMIT_BYTES)


FF_CHUNK = 256


def _layer_spec(shape, layer):
    nd = len(shape)
    return pl.BlockSpec((1,) + shape, lambda *_: (layer,) + (0,) * nd, pipeline_mode=pl.Buffered(1))


def _swiglu_half_step(x, gain, wg_ref, wu_ref, wd_ref, side_work=None):
    h = _rms(x, gain).astype(BF16)
    nchunk = wg_ref.shape[2] // FF_CHUNK

    def up(c):
        sl = slice(c * FF_CHUNK, (c + 1) * FF_CHUNK)
        return _dot(h, wg_ref[0, :, sl]), _dot(h, wu_ref[0, :, sl])

    acc = jnp.zeros(x.shape, F32)
    g, u = up(0)
    for c in range(nchunk):
        a = (g * jax.nn.sigmoid(g) * u).astype(BF16)
        if c + 1 < nchunk:
            g, u = up(c + 1)
        if side_work is not None:
            side_work[c]()
        acc = acc + _dot(a, wd_ref[0, c * FF_CHUNK:(c + 1) * FF_CHUNK, :])
    return x + 0.5 * acc


_N_POST_IN = {None: 0, "final_norm": 1, "kv": 2, "q": 2}
_N_POST_OUT = {None: 0, "final_norm": 0, "kv": 4, "q": 2}
_N_DECODE_IN = 7


def _ffn_kernel(*refs, pre, post, decode):
    refs = list(refs)
    if decode:
        pt_ref = refs.pop(0)
    n_in = 1 + (5 if pre else 0) + 4 + _N_POST_IN[post]
    ins, refs = refs[:n_in], refs[n_in:]
    if decode:
        pps, sps, lam_init, seq0, n_pages = decode
        qs_ref, kn_ref, vn_ref, lam_ref, subrow_ref, ck_hbm, cv_hbm = refs[:_N_DECODE_IN]
        refs = refs[_N_DECODE_IN:]
    n_out = 1 + _N_POST_OUT[post]
    outs, refs = refs[:n_out], refs[n_out:]
    if decode:
        os_ref, *decode_state, kbuf, vbuf, sem = refs
        nsteps = pl.num_programs(0) * pl.num_programs(1)
        step = pl.program_id(0) * pl.num_programs(1) + pl.program_id(1)
        group = lax.rem(step, sps)
        slot = lax.rem(step, 2)

        def page_copies(t, into):
            base = (seq0 + lax.div(t, sps)) * n_pages + lax.rem(t, sps) * pps
            copies = []
            for k in range(pps):
                page = pt_ref[base + k]
                copies.append(pltpu.make_async_copy(ck_hbm.at[page], kbuf.at[into * pps + k], sem.at[into, 0]))
                copies.append(pltpu.make_async_copy(cv_hbm.at[page], vbuf.at[into * pps + k], sem.at[into, 1]))
            return copies

        @pl.when(step == 0)
        def _():
            for c in page_copies(step, slot):
                c.start()

        @pl.when(step + 1 < nsteps)
        def _():
            for c in page_copies(step + 1, 1 - slot):
                c.start()

        for c in page_copies(step, slot):
            c.wait()
        kt_pages = [kbuf.at[slot * pps + k] for k in range(pps)]
        v_pages = [vbuf.at[slot * pps + k] for k in range(pps)]

        @pl.when(group == 0)
        def _():
            _decode_init(qs_ref, *decode_state)

    x = ins[0][0]
    ins = ins[1:]
    if pre == "mix_out":
        om_ref, qm_ref, mk_ref, mv_ref, wo_ref = ins[:5]
        ins = ins[5:]
        dm = om_ref.shape[2]
        o_mem = _mem_attend_shared(qm_ref[0], mk_ref[0], mv_ref[0].astype(BF16))
        x = x + _dot(om_ref[0], wo_ref[0, :dm, :]) + _dot(o_mem.astype(BF16), wo_ref[0, dm:, :])
    g_ref, wg_ref, wu_ref, wd_ref = ins[:4]
    ins = ins[4:]
    side_work = None
    if decode:
        side_work = _decode_side_work(kt_pages, v_pages, *decode_state, wg_ref.shape[2] // FF_CHUNK)
    y = _swiglu_half_step(x, g_ref[0], wg_ref, wu_ref, wd_ref, side_work)
    o_ref = outs[0]
    if post == "final_norm":
        o_ref[0] = _rms(y, ins[0][...])
    else:
        o_ref[0] = y
    if post in ("kv", "q"):
        pg_ref, pw_ref = ins
        h = _rms(y, pg_ref[...]).astype(BF16)
        if post == "kv":
            _store_prompt_kv(h, pw_ref, *outs[1:])
        else:
            q_ref, qm_out_ref = outs[1:]
            dqk = q_ref.shape[2]
            q_ref[0] = (_dot(h, pw_ref[0, :, :dqk]) * (DIFF_DK ** -0.5 * LOG2E)).astype(BF16)
            qm_out_ref[0] = _dot(h, pw_ref[0, :, dqk:]).astype(BF16)

    if decode:
        @pl.when(group == sps - 1)
        def _():
            _decode_finish(kn_ref, vn_ref, lam_ref, subrow_ref, os_ref, *decode_state, lam_init)


def _ffn(x, gains, wg, wu, wd, layer, pre=None, post=None, pre_args=(), post_args=(), decode=None):
    b, l, d = x.shape
    d_ff = wg.shape[2]
    tm = min(l, 256 if decode else 512)
    assert l % tm == 0 and d_ff % FF_CHUNK == 0
    depth = gains.shape[0]
    nlt = l // tm
    row = lambda width: pl.BlockSpec((1, tm, width), lambda i, j, *_: (i, j, 0))
    in_specs = [row(d)]
    args = [x]
    if pre == "mix_out":
        o_main, q_mem, mem_kt, mem_vt, w_out_b = pre_args
        m = mem_kt.shape[2]
        per_seq = pl.BlockSpec((1, MEM_W, m), lambda i, j, *_: (i, 0, 0))
        in_specs += [row(o_main.shape[2]), row(MEM_W), per_seq, per_seq, _layer_spec(w_out_b.shape[1:], layer)]
        args += [o_main, q_mem, mem_kt, mem_vt, w_out_b]
    in_specs += [_layer_spec((1, d), layer), _layer_spec((d, d_ff), layer),
                 _layer_spec((d, d_ff), layer), _layer_spec((d_ff, d), layer)]
    args += [gains.reshape(depth, 1, d), wg, wu, wd]
    out_shape = [jax.ShapeDtypeStruct((b, l, d), F32)]
    out_specs = [row(d)]
    if post == "final_norm":
        (final_gain,) = post_args
        in_specs.append(_const_spec((1, d)))
        args.append(final_gain.reshape(1, d))
    elif post == "kv":
        kv_gain, w_kv_b = post_args
        in_specs += [_const_spec((1, d)), _const_spec(w_kv_b.shape)]
        args += [kv_gain.reshape(1, d), w_kv_b]
        kv_shapes, kv_specs = _prompt_kv_outputs(b, l, tm)
        out_shape += kv_shapes
        out_specs += kv_specs
    elif post == "q":
        q_gain, w_in_b, q_layer = post_args
        dqk = DIFF_HEADS * 2 * DIFF_DK
        in_specs += [_const_spec((1, d)), _layer_spec(w_in_b.shape[1:], q_layer)]
        args += [q_gain.reshape(1, d), w_in_b]
        out_shape += [jax.ShapeDtypeStruct((b, l, dqk), BF16), jax.ShapeDtypeStruct((b, l, MEM_W), BF16)]
        out_specs += [row(dqk), row(MEM_W)]
    name = "ffn" + ("_" + pre if pre else "") + ("_" + post if post else "")
    kernel_kwargs = dict(pre=pre, post=post, decode=None)
    scratch, prefetch = [], []
    if decode:
        job = decode
        dqk, dv, n_pages, psize = job["dqk"], job["dv"], job["n_pages"], job["psize"]
        seq0, nseq = job["seq0"], job["nseq"]
        nsteps = b * nlt
        assert nsteps % nseq == 0, "each sequence's page sweep must take a whole number of grid steps"
        sps = nsteps // nseq
        assert n_pages % sps == 0
        pps = n_pages // sps

        def seq_local(i, j):
            return (i * nlt + j) // sps

        seq_spec = pl.BlockSpec((1, 1, dqk), lambda i, j, pt: (seq0 + seq_local(i, j), 0, 0))
        in_specs += [seq_spec, seq_spec, seq_spec,
                     pl.BlockSpec((4, DIFF_DK), lambda i, j, pt: (0, 0)),
                     pl.BlockSpec((1, dv), lambda i, j, pt: (0, 0)),
                     pl.BlockSpec(memory_space=pl.ANY), pl.BlockSpec(memory_space=pl.ANY)]
        args += [job["q"], job["k_new"], job["v_new"], job["lam_vecs"], job["subrow"], job["ckt"], job["cvh"]]
        out_shape.append(jax.ShapeDtypeStruct((nseq, 1, dv), BF16))
        out_specs.append(pl.BlockSpec((1, 1, dv), lambda i, j, pt: (seq_local(i, j), 0, 0)))
        page_bufs = (2 * pps,) + job["ckt"].shape[1:]
        assert job["cvh"].shape[1:] == job["ckt"].shape[1:][:1] + (DIFF_HEADS * psize, DIFF_DV)
        scratch = [pltpu.VMEM((DECODE_ROWS, dqk), BF16), pltpu.VMEM((DECODE_ROWS, 1), F32),
                   pltpu.VMEM((DECODE_ROWS, 1), F32), pltpu.VMEM((DECODE_ROWS, DIFF_DV), F32),
                   pltpu.VMEM(page_bufs, F32), pltpu.VMEM((2 * pps,) + job["cvh"].shape[1:], F32),
                   pltpu.SemaphoreType.DMA((2, 2))]
        prefetch = [job["page_table"]]
        kernel_kwargs["decode"] = (pps, sps, job["lam_init"], seq0, n_pages)
        name += "_decode"
    out = pl.pallas_call(
        functools.partial(_ffn_kernel, **kernel_kwargs),
        out_shape=tuple(out_shape),
        grid_spec=pltpu.PrefetchScalarGridSpec(
            num_scalar_prefetch=len(prefetch), grid=(b, nlt), in_specs=in_specs,
            out_specs=tuple(out_specs), scratch_shapes=scratch),
        compiler_params=_params(("arbitrary", "arbitrary") if decode else ("parallel", "parallel")),
        name=name,
    )(*prefetch, *args)
    return out[0] if len(out) == 1 else out


def _memkv_kernel(m_ref, g_ref, w_ref, kt_ref, vt_ref):
    h = _rms(m_ref[0], g_ref[0]).astype(BF16)
    kv = _dot(h, w_ref[0])
    kt_ref[0, 0] = kv[:, :MEM_W].T
    vt_ref[0, 0] = kv[:, MEM_W:].T


def _memkv(mem, gains, w):
    depth = w.shape[0]
    b, m, d = mem.shape
    out = jax.ShapeDtypeStruct((depth, b, MEM_W, m), F32)
    return pl.pallas_call(
        _memkv_kernel,
        out_shape=(out, out),
        grid=(depth, b),
        in_specs=[
            pl.BlockSpec((1, m, d), lambda i, j: (j, 0, 0)),
            pl.BlockSpec((1, 1, d), lambda i, j: (i, 0, 0)),
            pl.BlockSpec((1, d, 2 * MEM_W), lambda i, j: (i, 0, 0)),
        ],
        out_specs=(
            pl.BlockSpec((1, 1, MEM_W, m), lambda i, j: (i, j, 0, 0)),
            pl.BlockSpec((1, 1, MEM_W, m), lambda i, j: (i, j, 0, 0)),
        ),
        compiler_params=_params(("arbitrary", "arbitrary")),
        name="memkv",
    )(mem, gains.reshape(depth, 1, d), w)


def _log_sigmoid(z):
    return jnp.minimum(z, 0.0) - jnp.log1p(jnp.exp(-jnp.abs(z)))


def _gla_recurrence_inputs(h, win_ref, wg2_ref, bg2_ref):
    q = _dot(h, win_ref[:, A_Q0:A_K0])
    k = _dot(h, win_ref[:, A_K0:A_V0])
    v = _dot(h, win_ref[:, A_V0:A_G0])
    g_lr = _dot(h, win_ref[:, A_G0:A_R0])
    z = _dot(g_lr.astype(BF16), wg2_ref[...]) + bg2_ref[...]
    logg = _log_sigmoid(z) / GLA_GATE_NORM
    return q, k, v, logg


def _gla_inputs(h, win_ref, wg2_ref, bg2_ref):
    q, k, v, logg = _gla_recurrence_inputs(h, win_ref, wg2_ref, bg2_ref)
    r = _dot(h, win_ref[:, A_R0:A_M0])
    q_mem = _dot(h, win_ref[:, A_M0:A_W])
    return q, k, v, logg, r, q_mem


def _gla_output_gate(o, r, gain):
    parts = []
    for h in range(GLA_HEADS):
        sl = slice(h * GLA_DV_PAD, (h + 1) * GLA_DV_PAD)
        oh = o[:, sl]
        ms = jnp.sum(oh * oh, axis=-1, keepdims=True) * (1.0 / GLA_DV)
        parts.append(oh * lax.rsqrt(ms + EPS) * gain[:, sl])
    on = jnp.concatenate(parts, axis=1)
    return on * (r * jax.nn.sigmoid(r))


def _mem_attend_shared(qm, mkt, mvt):
    row_head = lax.broadcasted_iota(jnp.int32, mkt.shape, 0) >> 6
    out_head = lax.broadcasted_iota(jnp.int32, (qm.shape[0], MEM_W), 1) >> 6
    scores = [_dot(qm, jnp.where(row_head == h, mkt, 0.0).astype(BF16)) * (MEM_DH ** -0.5)
              for h in range(MEM_HEADS)]
    o = jnp.zeros((qm.shape[0], MEM_W), F32)
    for h in range(MEM_HEADS):
        e = jnp.exp(scores[h] - jnp.max(scores[h], axis=-1, keepdims=True))
        l = jnp.sum(e, axis=-1, keepdims=True)
        oh = _dot_nt(e.astype(BF16), mvt) / l
        o = o + jnp.where(out_head == h, oh, 0.0)
    return o


def _mem_attend_rows(qmem, mkt_ref, mvt_ref, om_scr):
    rows = qmem.shape[0]
    r8 = lax.broadcasted_iota(jnp.int32, (SUBLANES, MEM_W), 0)
    lh = lax.broadcasted_iota(jnp.int32, (SUBLANES, MEM_W), 1) >> 6
    for j in range(rows):
        qrow = jnp.broadcast_to(qmem[j:j + 1], (SUBLANES, MEM_W))
        qm = jnp.where(r8 == lh, qrow, 0.0).astype(BF16)
        s = _dot(qm, mkt_ref[j].astype(BF16)) * (MEM_DH ** -0.5)
        e = jnp.exp(s - jnp.max(s, axis=-1, keepdims=True))
        l = jnp.sum(e, axis=-1, keepdims=True)
        oh = _dot_nt(e.astype(BF16), mvt_ref[j].astype(BF16)) / l
        om_scr[j:j + 1, :] = jnp.sum(jnp.where(r8 == lh, oh, 0.0), axis=0, keepdims=True)


def _mixer_a_prompt_kernel(x_ref, g_ref, win_ref, wg2_ref, bg2_ref, on_ref, mk_ref, mv_ref,
                           wom_ref, wome_ref, xo_ref, st_ref, s_scr):
    tl = x_ref.shape[1]
    li = pl.program_id(1)

    @pl.when(li == 0)
    def _():
        s_scr[...] = jnp.zeros(s_scr.shape, F32)

    x = x_ref[0]
    h = _rms(x, g_ref[...]).astype(BF16)
    q, k, v, logg = _gla_recurrence_inputs(h, win_ref, wg2_ref, bg2_ref)
    r = _dot(h, win_ref[:, A_R0:A_M0])
    q_mem = _dot(h, win_ref[:, A_M0:A_W])

    row = lax.broadcasted_iota(jnp.int32, (tl, tl), 0)
    col = lax.broadcasted_iota(jnp.int32, (tl, tl), 1)
    tril = ((row >> 6) == (col >> 6)) & (col <= row)
    trilb = jnp.where(tril, 1.0, 0.0).astype(BF16)
    hi = logg.astype(BF16)
    lo = (logg - hi.astype(F32)).astype(BF16)
    b = _dot(trilb, hi) + _dot(trilb, lo)

    o_mem = _mem_attend_shared(q_mem.astype(BF16), mk_ref[0], mv_ref[0].astype(BF16))

    lane_chunk = lax.broadcasted_iota(jnp.int32, (GLA_DK_PAD, tl), 1) >> 6
    scale = GLA_DK ** -0.5
    nchunk = tl // GLA_CHUNK
    qdbs, intras, decs, incs = [], [], [], []
    for hd in range(GLA_HEADS):
        ks = slice(hd * GLA_DK_PAD, (hd + 1) * GLA_DK_PAD)
        vs = slice(hd * GLA_DV_PAD, (hd + 1) * GLA_DV_PAD)
        bh = b[:, ks]
        qdb = ((q[:, ks] * scale) * jnp.exp(bh)).astype(BF16)
        ki = k[:, ks] * jnp.exp(-bh)
        att = jnp.where(tril, _dot_nt(qdb, ki.astype(BF16)), 0.0)
        vh = v[:, vs].astype(BF16)
        intras.append(_dot(att.astype(BF16), vh))
        ki_t = ki.T
        b_t = bh.T
        dec_h, inc_h = [], []
        for c in range(nchunk):
            last = c * GLA_CHUNK + GLA_CHUNK - 1
            dec = jnp.exp(b_t[:, last:last + 1])
            kend_t = jnp.where(lane_chunk == c, ki_t * dec, 0.0).astype(BF16)
            dec_h.append(dec)
            inc_h.append(_dot(kend_t, vh))
        qdbs.append(qdb)
        decs.append(dec_h)
        incs.append(inc_h)

    o_parts = []
    for hd in range(GLA_HEADS):
        state = s_scr[hd]
        starts = []
        for c in range(nchunk):
            starts.append(state.astype(BF16))
            state = decs[hd][c] * state + incs[hd][c]
        s_scr[hd] = state
        inter = [_dot(qdbs[hd][c * GLA_CHUNK:(c + 1) * GLA_CHUNK], starts[c]) for c in range(nchunk)]
        o_parts.append(intras[hd] + jnp.concatenate(inter, axis=0))
    o = jnp.concatenate(o_parts, axis=1)

    o_main = _gla_output_gate(o, r, on_ref[...])
    xo_ref[0] = x + _dot(o_main.astype(BF16), wom_ref[...]) + _dot(o_mem.astype(BF16), wome_ref[...])

    @pl.when(li == pl.num_programs(1) - 1)
    def _():
        for hd in range(GLA_HEADS):
            st_ref[0, hd] = s_scr[hd][:GLA_DK, :GLA_DV]


def _mixer_a_prompt(x, gain, wa, mem_k, mem_v):
    b, l, d = x.shape
    tl = min(l, 256)
    assert l % tl == 0 and tl % GLA_CHUNK == 0
    m = mem_k.shape[2]
    return pl.pallas_call(
        _mixer_a_prompt_kernel,
        out_shape=(
            jax.ShapeDtypeStruct((b, l, d), F32),
            jax.ShapeDtypeStruct((b, GLA_HEADS, GLA_DK, GLA_DV), F32),
        ),
        grid=(b, l // tl),
        in_specs=[
            pl.BlockSpec((1, tl, d), lambda i, j: (i, j, 0)),
            _const_spec((1, d)),
            _const_spec((d, A_W)),
            _const_spec((LANES, GLA_HEADS * GLA_DK_PAD)),
            _const_spec((1, GLA_HEADS * GLA_DK_PAD)),
            _const_spec((1, GLA_HEADS * GLA_DV_PAD)),
            pl.BlockSpec((1, MEM_W, m), lambda i, j: (i, 0, 0)),
            pl.BlockSpec((1, MEM_W, m), lambda i, j: (i, 0, 0)),
            _const_spec((GLA_HEADS * GLA_DV_PAD, d)),
            _const_spec((MEM_W, d)),
        ],
        out_specs=(
            pl.BlockSpec((1, tl, d), lambda i, j: (i, j, 0)),
            pl.BlockSpec((1, GLA_HEADS, GLA_DK, GLA_DV), lambda i, j: (i, 0, 0, 0)),
        ),
        scratch_shapes=[pltpu.VMEM((GLA_HEADS, GLA_DK_PAD, GLA_DV_PAD), F32)],
        compiler_params=_params(("parallel", "arbitrary")),
        name="mixer_a_prompt",
    )(x, gain.reshape(1, d), wa["w_in"], wa["w_gate2"], wa["b_gate2"], wa["onorm"], mem_k, mem_v,
      wa["w_out_main"], wa["w_out_mem"])


def _columns(t):
    pad = jnp.zeros((LANES - t.shape[0], t.shape[1]), F32)
    return jnp.concatenate([t, pad], axis=0).T


def _mixer_a_sample_kernel(x_ref, g_ref, win_ref, wg2_ref, bg2_ref, on_ref, st_ref, mk_ref, mv_ref,
                           wom_ref, wome_ref, xo_ref, sto_ref, o_scr, om_scr):
    rows = x_ref.shape[0]
    x = x_ref[...]
    h = _rms(x, g_ref[...]).astype(BF16)
    q, k, v, logg, r, q_mem = _gla_inputs(h, win_ref, wg2_ref, bg2_ref)
    a_t = _columns(jnp.exp(logg))
    k_t = _columns(k)
    q_t = _columns(q * (GLA_DK ** -0.5))
    o_scr[...] = jnp.zeros(o_scr.shape, F32)
    for j in range(rows):
        for hd in range(GLA_HEADS):
            rs = slice(hd * GLA_DK_PAD, hd * GLA_DK_PAD + GLA_DK)
            vs = slice(hd * GLA_DV_PAD, hd * GLA_DV_PAD + GLA_DV)
            s_new = a_t[rs, j:j + 1] * st_ref[j, hd] + k_t[rs, j:j + 1] * v[j:j + 1, vs]
            sto_ref[j, hd] = s_new
            o_scr[j:j + 1, vs] = jnp.sum(q_t[rs, j:j + 1] * s_new, axis=0, keepdims=True)
    o_main = _gla_output_gate(o_scr[...], r, on_ref[...])
    _mem_attend_rows(q_mem, mk_ref, mv_ref, om_scr)
    xo_ref[...] = (x + _dot(o_main.astype(BF16), wom_ref[...])
                   + _dot(om_scr[...].astype(BF16), wome_ref[...]))


def _mixer_a_sample(x, gain, wa, state, mem_k, mem_v):
    n, d = x.shape
    rows = SUBLANES
    assert n % rows == 0
    m = mem_k.shape[2]
    return pl.pallas_call(
        _mixer_a_sample_kernel,
        out_shape=(
            jax.ShapeDtypeStruct((n, d), F32),
            jax.ShapeDtypeStruct(state.shape, F32),
        ),
        grid=(n // rows,),
        in_specs=[
            pl.BlockSpec((rows, d), lambda i: (i, 0)),
            _const_spec((1, d)),
            _const_spec((d, A_W)),
            _const_spec((LANES, GLA_HEADS * GLA_DK_PAD)),
            _const_spec((1, GLA_HEADS * GLA_DK_PAD)),
            _const_spec((1, GLA_HEADS * GLA_DV_PAD)),
            pl.BlockSpec((rows, GLA_HEADS, GLA_DK, GLA_DV), lambda i: (i, 0, 0, 0)),
            pl.BlockSpec((rows, MEM_W, m), lambda i: (i, 0, 0)),
            pl.BlockSpec((rows, MEM_W, m), lambda i: (i, 0, 0)),
            _const_spec((GLA_HEADS * GLA_DV_PAD, d)),
            _const_spec((MEM_W, d)),
        ],
        out_specs=(
            pl.BlockSpec((rows, d), lambda i: (i, 0)),
            pl.BlockSpec((rows, GLA_HEADS, GLA_DK, GLA_DV), lambda i: (i, 0, 0, 0)),
        ),
        scratch_shapes=[
            pltpu.VMEM((rows, GLA_HEADS * GLA_DV_PAD), F32),
            pltpu.VMEM((rows, MEM_W), F32),
        ],
        compiler_params=_params(("parallel",)),
        name="mixer_a_sample",
    )(x, gain.reshape(1, d), wa["w_in"], wa["w_gate2"], wa["b_gate2"], wa["onorm"], state, mem_k, mem_v,
      wa["w_out_main"], wa["w_out_mem"])


def _store_prompt_kv(h, w_ref, kt_ref, vh_ref, kb_ref, vt_ref):
    dqk = DIFF_HEADS * 2 * DIFF_DK
    k = _dot(h, w_ref[:, :dqk])
    v = _dot(h, w_ref[:, dqk:])
    kt_ref[0] = k.T
    for hd in range(DIFF_HEADS):
        vh_ref[0, hd] = v[:, hd * DIFF_DV:(hd + 1) * DIFF_DV]
    kb_ref[0] = k.astype(BF16)
    vt_ref[0] = v.T.astype(BF16)


def _prompt_kv_outputs(b, l, tm):
    dqk = DIFF_HEADS * 2 * DIFF_DK
    dv = DIFF_HEADS * DIFF_DV
    shapes = [jax.ShapeDtypeStruct((b, dqk, l), F32),
              jax.ShapeDtypeStruct((b, DIFF_HEADS, l, DIFF_DV), F32),
              jax.ShapeDtypeStruct((b, l, dqk), BF16),
              jax.ShapeDtypeStruct((b, dv, l), BF16)]
    specs = [pl.BlockSpec((1, dqk, tm), lambda i, j, *_: (i, 0, j)),
             pl.BlockSpec((1, DIFF_HEADS, tm, DIFF_DV), lambda i, j, *_: (i, 0, j, 0)),
             pl.BlockSpec((1, tm, dqk), lambda i, j, *_: (i, j, 0)),
             pl.BlockSpec((1, dv, tm), lambda i, j, *_: (i, 0, j))]
    return shapes, specs


def _kvproj_sample_kernel(x_ref, g_ref, w_ref, k_ref, v_ref):
    h = _rms(x_ref[...], g_ref[...]).astype(BF16)
    dqk = k_ref.shape[1]
    k_ref[...] = _dot(h, w_ref[:, :dqk])
    v_ref[...] = _dot(h, w_ref[:, dqk:])


def _kvproj_sample(x, gain, w):
    n, d = x.shape
    dqk = DIFF_HEADS * 2 * DIFF_DK
    dv = DIFF_HEADS * DIFF_DV
    return pl.pallas_call(
        _kvproj_sample_kernel,
        out_shape=(jax.ShapeDtypeStruct((n, dqk), F32), jax.ShapeDtypeStruct((n, dv), F32)),
        grid=(1,),
        in_specs=[pl.BlockSpec((n, d), lambda i: (0, 0)), _const_spec((1, d)), _const_spec((d, dqk + dv))],
        out_specs=(pl.BlockSpec((n, dqk), lambda i: (0, 0)), pl.BlockSpec((n, dv), lambda i: (0, 0))),
        compiler_params=_params(("arbitrary",)),
        name="kvproj_sample",
    )(x, gain.reshape(1, d), w)


def _projb_sample_kernel(x_ref, g_ref, w_ref, q_ref, qm_ref):
    h = _rms(x_ref[...], g_ref[...]).astype(BF16)
    dqk = q_ref.shape[1]
    q_ref[...] = (_dot(h, w_ref[0, :, :dqk]) * (DIFF_DK ** -0.5 * LOG2E)).astype(BF16)
    qm_ref[...] = _dot(h, w_ref[0, :, dqk:]).astype(BF16)


def _projb_sample(x, gain, w_in_b, layer):
    n, d = x.shape
    dqk = DIFF_HEADS * 2 * DIFF_DK
    return pl.pallas_call(
        _projb_sample_kernel,
        out_shape=(jax.ShapeDtypeStruct((n, dqk), BF16), jax.ShapeDtypeStruct((n, MEM_W), BF16)),
        grid=(1,),
        in_specs=[pl.BlockSpec((n, d), lambda i: (0, 0)), _const_spec((1, d)),
                  _layer_spec(w_in_b.shape[1:], layer)],
        out_specs=(pl.BlockSpec((n, dqk), lambda i: (0, 0)), pl.BlockSpec((n, MEM_W), lambda i: (0, 0))),
        compiler_params=_params(("arbitrary",)),
        name="projb_sample",
    )(x, gain.reshape(1, d), w_in_b)


def _lambda_full(lam_ref, lam_init):
    lv = lam_ref[...]
    s1 = jnp.sum(lv[0:1] * lv[1:2], axis=-1, keepdims=True)
    s2 = jnp.sum(lv[2:3] * lv[3:4], axis=-1, keepdims=True)
    return jnp.exp(s1) - jnp.exp(s2) + lam_init


ONES_ROWS = 16
DECODE_ROWS = 16
SCORE_LOOKAHEAD = 3


def _decode_init(qs_ref, sqm_scr, sm_scr, sl_scr, sacc_scr):
    nrow, width = sqm_scr.shape
    r = lax.broadcasted_iota(jnp.int32, (nrow, width), 0)
    lane_half = lax.broadcasted_iota(jnp.int32, (nrow, width), 1) >> 6
    qrow = jnp.broadcast_to(qs_ref[0].astype(F32), (nrow, width))
    sqm_scr[...] = jnp.where(r == lane_half, qrow, 0.0).astype(BF16)
    sm_scr[...] = jnp.full(sm_scr.shape, -jnp.inf, F32)
    sl_scr[...] = jnp.zeros(sl_scr.shape, F32)
    sacc_scr[...] = jnp.zeros(sacc_scr.shape, F32)


def _decode_side_work(kt_refs, v_refs, sqm_scr, sm_scr, sl_scr, sacc_scr, nslab):
    nrow = sacc_scr.shape[0]
    psize = kt_refs[0].shape[2]
    assert nslab >= 3
    box = {}

    def score_job():
        qm = sqm_scr[...]
        box["s"] = jnp.concatenate([_dot(qm, kr[0].astype(BF16)) for kr in kt_refs], axis=1)

    def softmax_job():
        m_old = sm_scr[...]
        m_new = jnp.maximum(m_old, jnp.max(box["s"], axis=-1, keepdims=True))
        box["pf"] = jnp.exp2(box["s"] - m_new)
        box["alpha"] = jnp.exp2(m_old - m_new)
        sl_scr[...] = box["alpha"] * sl_scr[...] + jnp.sum(box["pf"], axis=-1, keepdims=True)
        sm_scr[...] = m_new

    def value_job():
        vrows = DIFF_HEADS * psize
        row_head = lax.broadcasted_iota(jnp.int32, (nrow, vrows), 0) >> 1
        col_head = lax.broadcasted_iota(jnp.int32, (nrow, vrows), 1) >> (psize.bit_length() - 1)
        own = row_head == col_head
        pv = jnp.zeros(sacc_scr.shape, F32)
        for i, vr in enumerate(v_refs):
            pi = box["pf"][:, i * psize:(i + 1) * psize]
            spread = jnp.where(own, jnp.concatenate([pi] * DIFF_HEADS, axis=1), 0.0).astype(BF16)
            pv = pv + _dot(spread, vr[0].astype(BF16))
        sacc_scr[...] = box["alpha"] * sacc_scr[...] + pv

    idle = lambda: None
    return [score_job, softmax_job] + [idle] * (nslab - 3) + [value_job]


def _decode_finish(kn_ref, vn_ref, lam_ref, subrow_ref, os_ref, sqm_scr, sm_scr, sl_scr, sacc_scr, lam_init):
    nrow = sqm_scr.shape[0]
    s_self = jnp.sum(sqm_scr[...].astype(F32) * kn_ref[0], axis=-1, keepdims=True)
    m_old = sm_scr[...]
    m_fin = jnp.maximum(m_old, s_self)
    p_self = jnp.exp2(s_self - m_fin)
    a_fin = jnp.exp2(m_old - m_fin)
    l_fin = a_fin * sl_scr[...] + p_self
    vn = vn_ref[0]
    vn_rows = jnp.concatenate(
        [vn[:, (r // 2) * DIFF_DV:(r // 2 + 1) * DIFF_DV] for r in range(2 * DIFF_HEADS)]
        + [jnp.zeros((nrow - 2 * DIFF_HEADS, DIFF_DV), F32)], axis=0)
    acc = (a_fin * sacc_scr[...] + p_self * vn_rows) / l_fin
    lam = _lambda_full(lam_ref, lam_init)
    parts = []
    for hd in range(DIFF_HEADS):
        od = acc[2 * hd:2 * hd + 1] - lam * acc[2 * hd + 1:2 * hd + 2]
        ms = jnp.mean(od * od, axis=-1, keepdims=True)
        parts.append(od * lax.rsqrt(ms + EPS))
    on = (jnp.concatenate(parts, axis=1) * subrow_ref[...]) * (1.0 - lam_init)
    os_ref[0] = on.astype(BF16)


def _flash_kernel(qi_tab, ki_tab, q_ref, k_ref, vt_ref, lam_ref, sub_ref, o_ref,
                  qm_scr, m_scr, acc_scr, *, ratio, lam_init):
    step = pl.program_id(1)
    qi = qi_tab[step]
    ki = ki_tab[step]
    tq = q_ref.shape[1]
    tk = k_ref.shape[1]

    @pl.when(ki == 0)
    def _():
        for hd in range(DIFF_HEADS):
            qh = q_ref[0, :, hd * DIFF_DV:(hd + 1) * DIFF_DV].astype(F32)
            first = lax.broadcasted_iota(jnp.int32, qh.shape, 1) < DIFF_DK
            qm_scr[hd, :tq] = jnp.where(first, qh, 0.0).astype(BF16)
            qm_scr[hd, tq:] = jnp.where(first, 0.0, qh).astype(BF16)
        m_scr[...] = jnp.full(m_scr.shape, -jnp.inf, F32)
        acc_scr[...] = jnp.zeros(acc_scr.shape, F32)

    def block(masked):
        if masked:
            col = lax.broadcasted_iota(jnp.int32, (tk, 2 * tq), 1)
            kpos = ki * tk + lax.broadcasted_iota(jnp.int32, (tk, 2 * tq), 0)
            keep = kpos <= qi * tq + jnp.where(col >= tq, col - tq, col)
            bias = jnp.where(keep, 0.0, -jnp.inf)
        ones = jnp.ones((ONES_ROWS, tk), BF16)

        njob = 2 * DIFF_HEADS

        def scores(job):
            hd, c = divmod(job, 2)
            st = _dot_nt(k_ref[0, :, hd * DIFF_DV:(hd + 1) * DIFF_DV], qm_scr[hd, c * tq:(c + 1) * tq])
            return st + bias[:, :tq] if masked else st

        pending = [scores(job) for job in range(SCORE_LOOKAHEAD)]
        for job in range(njob):
            hd, c = divmod(job, 2)
            hs = slice(hd * DIFF_DV, (hd + 1) * DIFF_DV)
            cols = slice(c * tq, (c + 1) * tq)
            st = pending.pop(0)
            if job + SCORE_LOOKAHEAD < njob:
                pending.append(scores(job + SCORE_LOOKAHEAD))
            m_old = m_scr[hd, :, cols]
            m_new = jnp.maximum(m_old, jnp.max(st, axis=0, keepdims=True))
            p = jnp.exp2(st - m_new).astype(BF16)
            alpha = jnp.exp2(m_old - m_new)
            v_aug = jnp.concatenate([vt_ref[0, hs, :], ones], axis=0)
            acc_scr[hd, :, cols] = alpha * acc_scr[hd, :, cols] + _dot(v_aug, p)
            m_scr[hd, :, cols] = m_new

    diag = ki >= qi * ratio

    @pl.when(diag)
    def _():
        block(True)

    @pl.when(jnp.logical_not(diag))
    def _():
        block(False)

    @pl.when(ki == (qi + 1) * ratio - 1)
    def _():
        lam = _lambda_full(lam_ref, lam_init)
        for hd in range(DIFF_HEADS):
            acc = acc_scr[hd]
            o = acc[:DIFF_DV] / acc[DIFF_DV:DIFF_DV + 1]
            od = o[:, :tq] - lam * o[:, tq:]
            ms = jnp.mean(od * od, axis=0, keepdims=True)
            on = (od * lax.rsqrt(ms + EPS) * sub_ref[...]) * (1.0 - lam_init)
            o_ref[0, :, hd * DIFF_DV:(hd + 1) * DIFF_DV] = on.T.astype(BF16)


def _flash_prompt(q, kb, vt, lam_vecs, subln, lam_init):
    b, l, dqk = q.shape
    dv = vt.shape[1]
    tq = min(l, 512)
    tk = min(l, 512)
    assert l % tq == 0 and tq % tk == 0
    ratio = tq // tk
    pairs = [(i, j) for i in range(l // tq) for j in range((i + 1) * ratio)]
    qi_tab = jnp.asarray([p[0] for p in pairs], jnp.int32)
    ki_tab = jnp.asarray([p[1] for p in pairs], jnp.int32)
    grid_spec = pltpu.PrefetchScalarGridSpec(
        num_scalar_prefetch=2,
        grid=(b, len(pairs)),
        in_specs=[
            pl.BlockSpec((1, tq, dqk), lambda i, s, qt, kt: (i, qt[s], 0)),
            pl.BlockSpec((1, tk, dqk), lambda i, s, qt, kt: (i, kt[s], 0)),
            pl.BlockSpec((1, dv, tk), lambda i, s, qt, kt: (i, 0, kt[s])),
            pl.BlockSpec((4, DIFF_DK), lambda i, s, qt, kt: (0, 0)),
            pl.BlockSpec((DIFF_DV, 1), lambda i, s, qt, kt: (0, 0)),
        ],
        out_specs=pl.BlockSpec((1, tq, dv), lambda i, s, qt, kt: (i, qt[s], 0)),
        scratch_shapes=[
            pltpu.VMEM((DIFF_HEADS, 2 * tq, DIFF_DV), BF16),
            pltpu.VMEM((DIFF_HEADS, 1, 2 * tq), F32),
            pltpu.VMEM((DIFF_HEADS, DIFF_DV + ONES_ROWS, 2 * tq), F32),
        ],
    )
    return pl.pallas_call(
        functools.partial(_flash_kernel, ratio=ratio, lam_init=lam_init),
        out_shape=jax.ShapeDtypeStruct((b, l, dv), BF16),
        grid_spec=grid_spec,
        compiler_params=_params(("parallel", "arbitrary")),
        name="flash_prompt",
    )(qi_tab, ki_tab, q, kb, vt, lam_vecs, subln.reshape(DIFF_DV, 1))


def _decode_jobs(q_s, k_new, v_new, cache_k, cache_v, page_table, lam_vecs, subln, lam_init, ncalls):
    n, dqk = q_s.shape
    dv = v_new.shape[1]
    n_phys, psize = cache_k.shape[:2]
    n_pages = page_table.shape[1]
    assert n % ncalls == 0 and psize & (psize - 1) == 0
    nseq = n // ncalls
    common = dict(
        q=q_s.reshape(n, 1, dqk), k_new=k_new.reshape(n, 1, dqk), v_new=v_new.reshape(n, 1, dv),
        ckt=jnp.transpose(cache_k, (0, 2, 3, 4, 1)).reshape(n_phys, 1, dqk, psize),
        cvh=jnp.transpose(cache_v, (0, 2, 1, 3)).reshape(n_phys, 1, DIFF_HEADS * psize, DIFF_DV),
        page_table=page_table.reshape(-1), lam_vecs=lam_vecs,
        subrow=jnp.tile(subln, DIFF_HEADS).reshape(1, dv), lam_init=lam_init,
        dqk=dqk, dv=dv, n_pages=n_pages, psize=psize, nseq=nseq)
    return [dict(common, seq0=c * nseq) for c in range(ncalls)]


def _post_sample_kernel(x_ref, om_ref, qm_ref, mk_ref, mv_ref, wo_ref, xo_ref, om_scr):
    _mem_attend_rows(qm_ref[...].astype(F32), mk_ref, mv_ref, om_scr)
    dm = om_ref.shape[1]
    xo_ref[...] = (x_ref[...] + _dot(om_ref[...], wo_ref[0, :dm, :])
                   + _dot(om_scr[...].astype(BF16), wo_ref[0, dm:, :]))


def _post_sample(x, o_main, q_mem, mem_k, mem_v, w_out_b, layer):
    n, d = x.shape
    rows = 2 * SUBLANES
    assert n % rows == 0
    dm = o_main.shape[1]
    m = mem_k.shape[2]
    return pl.pallas_call(
        _post_sample_kernel,
        out_shape=jax.ShapeDtypeStruct((n, d), F32),
        grid=(n // rows,),
        in_specs=[
            pl.BlockSpec((rows, d), lambda i: (i, 0)),
            pl.BlockSpec((rows, dm), lambda i: (i, 0)),
            pl.BlockSpec((rows, MEM_W), lambda i: (i, 0)),
            pl.BlockSpec((rows, MEM_W, m), lambda i: (i, 0, 0)),
            pl.BlockSpec((rows, MEM_W, m), lambda i: (i, 0, 0)),
            _layer_spec(w_out_b.shape[1:], layer),
        ],
        out_specs=pl.BlockSpec((rows, d), lambda i: (i, 0)),
        scratch_shapes=[pltpu.VMEM((rows, MEM_W), F32)],
        compiler_params=_params(("parallel",)),
        name="post_sample",
    )(x, o_main, q_mem, mem_k, mem_v, w_out_b)


def _pad_heads_cols(w, heads, width, padded):
    lead = w.shape[:-1]
    w = w.reshape(lead + (heads, width))
    w = jnp.pad(w, [(0, 0)] * len(lead) + [(0, 0), (0, padded - width)])
    return w.reshape(lead + (heads * padded,))


def _prep_layer_a(w_in, w_gate2, b_gate2, onorm, w_out):
    dqk = GLA_HEADS * GLA_DK
    dv = GLA_HEADS * GLA_DV
    w_in = w_in.astype(BF16)
    w_out = w_out.astype(BF16)
    o = 0
    wq = w_in[:, o:o + dqk]; o += dqk
    wk = w_in[:, o:o + dqk]; o += dqk
    wv = w_in[:, o:o + dv]; o += dv
    wg = w_in[:, o:o + GLA_GATE_RANK]; o += GLA_GATE_RANK
    wr = w_in[:, o:o + dv]; o += dv
    wm = w_in[:, o:o + MEM_W]
    w_all = jnp.concatenate([
        _pad_heads_cols(wq, GLA_HEADS, GLA_DK, GLA_DK_PAD),
        _pad_heads_cols(wk, GLA_HEADS, GLA_DK, GLA_DK_PAD),
        _pad_heads_cols(wv, GLA_HEADS, GLA_DV, GLA_DV_PAD),
        jnp.pad(wg, ((0, 0), (0, LANES - GLA_GATE_RANK))),
        _pad_heads_cols(wr, GLA_HEADS, GLA_DV, GLA_DV_PAD),
        wm,
    ], axis=1)
    wg2 = _pad_heads_cols(w_gate2, GLA_HEADS, GLA_DK, GLA_DK_PAD)
    wg2 = jnp.pad(wg2, ((0, LANES - GLA_GATE_RANK), (0, 0))).astype(BF16)
    bg2 = _pad_heads_cols(b_gate2.reshape(1, dqk), GLA_HEADS, GLA_DK, GLA_DK_PAD)
    gain = _pad_heads_cols(jnp.tile(onorm, GLA_HEADS).reshape(1, dv), GLA_HEADS, GLA_DV, GLA_DV_PAD)
    wo_main = _pad_heads_cols(w_out[:dv].T, GLA_HEADS, GLA_DV, GLA_DV_PAD).T
    return {
        "w_in": w_all, "w_gate2": wg2, "b_gate2": bg2, "onorm": gain,
        "w_out_main": wo_main, "w_out_mem": w_out[dv:],
    }


def kernel(x_prompt, x_sample, mem_prompt, state_gla, cache_k, cache_v, cache_mem_k, cache_mem_v,
           page_table, ffn1_norm, ffn1_w_gate, ffn1_w_up, ffn1_w_down, mix_norm, a_w_in, a_w_gate2,
           a_b_gate2, a_onorm, b_w_in, b_lambda_q1, b_lambda_k1, b_lambda_q2, b_lambda_k2, b_subln,
           mem_norm, w_mem_kv, w_out, ffn2_norm, ffn2_w_gate, ffn2_w_up, ffn2_w_down, kv_norm, w_kv,
           final_norm):
    bp, seq, d = x_prompt.shape
    db, dseq, _ = x_sample.shape
    assert dseq == 1
    depth = ffn1_norm.shape[0]
    n_a = a_w_in.shape[0]
    mtok = mem_prompt.shape[1]
    assert depth == n_a + 1
    lb = depth - 1
    bf = lambda t: t.astype(BF16)
    f1 = (bf(ffn1_w_gate), bf(ffn1_w_up), bf(ffn1_w_down))
    f2 = (bf(ffn2_w_gate), bf(ffn2_w_up), bf(ffn2_w_down))
    w_kv_b = bf(w_kv)
    w_mem_b = bf(w_mem_kv)
    w_out_b = bf(w_out)
    b_w_in_b = bf(b_w_in)

    mem_k_p, mem_v_p = _memkv(mem_prompt, mem_norm, w_mem_b)
    cmk = jnp.transpose(cache_mem_k, (0, 1, 3, 4, 2)).reshape(depth, db, MEM_W, mtok)
    cmv = jnp.transpose(cache_mem_v, (0, 1, 3, 4, 2)).reshape(depth, db, MEM_W, mtok)

    layer_a = [_prep_layer_a(a_w_in[i], a_w_gate2[i], a_b_gate2[i], a_onorm[i], w_out[i]) for i in range(n_a)]
    lam_init = 0.8 - 0.6 * math.exp(-0.3 * lb)
    lam_vecs = jnp.stack([b_lambda_q1[0], b_lambda_k1[0], b_lambda_q2[0], b_lambda_k2[0]])

    xs = x_sample.reshape(1, db, d)
    states_s = []
    for i in range(n_a):
        xs = _ffn(xs, ffn1_norm, *f1, i)
        xs2, st_s = _mixer_a_sample(xs[0], mix_norm[i], layer_a[i], state_gla[i], cmk[i], cmv[i])
        states_s.append(st_s)
        xs = _ffn(xs2[None], ffn2_norm, *f2, i)
    k_s, v_s = _kvproj_sample(xs[0], kv_norm, w_kv_b)
    xs = _ffn(xs, ffn1_norm, *f1, lb)
    q_s, qm_s = _projb_sample(xs[0], mix_norm[lb], b_w_in_b, 0)

    jobs = iter(_decode_jobs(q_s, k_s, v_s, cache_k, cache_v, page_table, lam_vecs, b_subln[0], lam_init,
                             ncalls=2 * depth))
    o_s_parts = []
    xp = x_prompt
    states_p = []
    for i in range(n_a):
        xp, part = _ffn(xp, ffn1_norm, *f1, i, decode=next(jobs))
        o_s_parts.append(part)
        xp, st_p = _mixer_a_prompt(xp, mix_norm[i], layer_a[i], mem_k_p[i], mem_v_p[i])
        states_p.append(st_p)
        if i == n_a - 1:
            xp, *kv_p, part = _ffn(xp, ffn2_norm, *f2, i, post="kv", post_args=(kv_norm, w_kv_b),
                                   decode=next(jobs))
        else:
            xp, part = _ffn(xp, ffn2_norm, *f2, i, decode=next(jobs))
        o_s_parts.append(part)
    xp, q_p, qm_p, part = _ffn(xp, ffn1_norm, *f1, lb, post="q", post_args=(mix_norm[lb], b_w_in_b, 0),
                               decode=next(jobs))
    o_s_parts.append(part)
    kt_p, vh_p, kb_p, vt_p = kv_p
    o_p = _flash_prompt(q_p, kb_p, vt_p, lam_vecs, b_subln[0], lam_init)
    xp, part = _ffn(xp, ffn2_norm, *f2, lb, pre="mix_out", pre_args=(o_p, qm_p, mem_k_p[lb], mem_v_p[lb], w_out_b),
                    post="final_norm", post_args=(final_norm,), decode=next(jobs))
    o_s_parts.append(part)

    o_s = jnp.concatenate(o_s_parts, axis=0).reshape(db, -1)
    xs = _post_sample(xs[0], o_s, qm_s, cmk[lb], cmv[lb], w_out_b, lb)[None]
    xs = _ffn(xs, ffn2_norm, *f2, lb, post="final_norm", post_args=(final_norm,))

    y_prompt = xp
    y_sample = xs.reshape(db, 1, d)

    def mem_rows(t):
        return jnp.transpose(t.reshape(depth, bp, MEM_HEADS, MEM_DH, mtok), (0, 1, 4, 2, 3))

    return (
        y_prompt,
        y_sample,
        jnp.stack(states_p),
        jnp.stack(states_s),
        jnp.transpose(kt_p.reshape(bp, DIFF_HEADS, 2, DIFF_DK, seq), (0, 4, 1, 2, 3)),
        jnp.transpose(vh_p, (0, 2, 1, 3)),
        k_s.reshape(db, 1, DIFF_HEADS, 2, DIFF_DK),
        v_s.reshape(db, 1, DIFF_HEADS, DIFF_DV),
        mem_rows(mem_k_p),
        mem_rows(mem_v_p),
    )
```
